```python
import math
import jax, jax.numpy as jnp
from jax import lax
import numpy as np

D_MODEL = 1024
BATCH = 8
SEQ = 2048
DEPTH = 2
DEC_BATCH = 128
DEC_SEQ = 8
PAST_LEN = 2048
PAGE_SIZE = 128

D_MIX = D_MODEL
N_ATT_HEADS = 4
DV = D_MIX // 2 // N_ATT_HEADS
DK = DV // 2
D_ATT = N_ATT_HEADS * DV
N_CH = 4
CH_DIM = D_MIX // 2 // N_CH
D_CMLP = N_CH * CH_DIM
CHUNK = 128
Q_BLOCK = 128
EPS = 1e-6
SPLIT_SIZES = (N_ATT_HEADS * 2 * DK, N_ATT_HEADS * 2 * DK, D_ATT, D_ATT, D_CMLP, D_CMLP, D_CMLP)
D_IN = sum(SPLIT_SIZES)
SPLIT_IDX = tuple(int(i) for i in np.cumsum(SPLIT_SIZES)[:-1])

kernel_name = "hymba_diffattn_chunkgmlp_decode_step"


def rmsnorm(x, g):
    xf = x.astype(jnp.float32)
    y = xf * lax.rsqrt(jnp.mean(xf * xf, axis=-1, keepdims=True) + EPS)
    return (y * g.astype(jnp.float32)).astype(x.dtype)


def layernorm(x, g, b):
    xf = x.astype(jnp.float32)
    mu = jnp.mean(xf, axis=-1, keepdims=True)
    xc = xf - mu
    y = xc * lax.rsqrt(jnp.mean(xc * xc, axis=-1, keepdims=True) + EPS)
    return (y * g.astype(jnp.float32) + b.astype(jnp.float32)).astype(x.dtype)


def alibi_slopes(n):
    return jnp.asarray([2.0 ** (-8.0 * (i + 1) / n) for i in range(n)], dtype=jnp.float32)


def lambda_init(layer_idx):
    return 0.8 - 0.6 * math.exp(-0.3 * layer_idx)


def diff_attn(q, k, v, qpos, kpos, lam):
    s = jnp.einsum('bqhcd,bkhcd->bhcqk', q.astype(jnp.float32), k.astype(jnp.float32)) * (DK ** -0.5)
    dist = (qpos[:, None] - kpos[None, :]).astype(jnp.float32)
    bias = -alibi_slopes(N_ATT_HEADS)[:, None, None, None] * dist[None, None]
    s = jnp.where(kpos[None, :] <= qpos[:, None], s + bias, -jnp.inf)
    p = jax.nn.softmax(s, axis=-1)
    pd = p[:, :, 0] - lam * p[:, :, 1]
    return jnp.einsum('bhqk,bkhd->bqhd', pd, v.astype(jnp.float32))


def attend_prompt(q, k, v, lam):
    B, T = q.shape[0], q.shape[1]
    nb = T // Q_BLOCK
    qb = q.reshape(B, nb, Q_BLOCK, N_ATT_HEADS, 2, DK).transpose(1, 0, 2, 3, 4, 5)
    kpos = jnp.arange(T)

    def block(args):
        i, qi = args
        qpos = i * Q_BLOCK + jnp.arange(Q_BLOCK)
        return diff_attn(qi, k, v, qpos, kpos, lam)

    out = lax.map(block, (jnp.arange(nb), qb))
    return out.transpose(1, 0, 2, 3, 4).reshape(B, T, N_ATT_HEADS, DV)


def attend_sample(q, k, v, lam, ck, cv, page_table):
    Bd, Td = q.shape[0], q.shape[1]
    past = page_table.shape[1] * PAGE_SIZE
    kp = ck[page_table].reshape(Bd, past, N_ATT_HEADS, 2, DK).astype(k.dtype)
    vp = cv[page_table].reshape(Bd, past, N_ATT_HEADS, DV).astype(v.dtype)
    kk = jnp.concatenate([kp, k], axis=1)
    vv = jnp.concatenate([vp, v], axis=1)
    qpos = past + jnp.arange(Td)
    kpos = jnp.arange(past + Td)
    return diff_attn(q, kk, vv, qpos, kpos, lam)


def chunk_mix(vn, ws, bs):
    B, T = vn.shape[0], vn.shape[1]
    c = min(T, CHUNK)
    vb = vn.reshape(B, T // c, c, N_CH, CH_DIM)
    out = jnp.einsum('hts,bcshd->bcthd', ws[:, :c, :c], vb) + bs[:, :c].T[None, None, :, :, None]
    return out.reshape(B, T, N_CH, CH_DIM)


def mixer_layer(x, l, attend, norm_g, w_in, lam_q1, lam_k1, lam_q2, lam_k2,
                subln_g, ln_v_g, ln_v_b, w_s, b_s, w_out):
    B, T = x.shape[0], x.shape[1]
    h = rmsnorm(x, norm_g[l])
    z = jnp.einsum('btd,de->bte', h, w_in[l])
    q, k, v, g_a, u, vc, g_c = jnp.split(z, SPLIT_IDX, axis=-1)
    q = q.reshape(B, T, N_ATT_HEADS, 2, DK)
    k = k.reshape(B, T, N_ATT_HEADS, 2, DK)
    v = v.reshape(B, T, N_ATT_HEADS, DV)
    lam_i = lambda_init(l)
    lam = (jnp.exp(jnp.sum(lam_q1[l].astype(jnp.float32) * lam_k1[l].astype(jnp.float32)))
           - jnp.exp(jnp.sum(lam_q2[l].astype(jnp.float32) * lam_k2[l].astype(jnp.float32)))
           + lam_i)
    att = attend(q, k, v, lam)
    att = rmsnorm(att, subln_g[l]) * (1.0 - lam_i)
    att = att.reshape(B, T, D_ATT).astype(x.dtype) * jax.nn.silu(g_a)
    vn = layernorm(vc.reshape(B, T, N_CH, CH_DIM), ln_v_g[l], ln_v_b[l])
    mask = jnp.tril(jnp.ones((CHUNK, CHUNK), dtype=w_s.dtype))
    mixed = chunk_mix(vn, w_s[l] * mask, b_s[l])
    cm = (u * mixed.reshape(B, T, D_CMLP).astype(u.dtype)) * jax.nn.silu(g_c)
    y = jnp.einsum('bte,ed->btd', jnp.concatenate([att, cm], axis=-1), w_out[l])
    return x + y, k.reshape(B, T, N_ATT_HEADS, 2 * DK), v, vn


def setup_inputs(seed: int = 0) -> dict:
    key = jax.random.key(seed)
    ks = jax.random.split(key, 20)
    n_pages = PAST_LEN // PAGE_SIZE
    n_used = DEC_BATCH * n_pages
    n_pool = n_used + n_used // 4
    f32 = jnp.float32
    page_table = jax.random.permutation(ks[0], n_pool)[:n_used].reshape(DEC_BATCH, n_pages).astype(jnp.int32)
    return {
        "x_prompt": jax.random.normal(ks[1], (BATCH, SEQ, D_MODEL), f32),
        "x_sample": jax.random.normal(ks[2], (DEC_BATCH, DEC_SEQ, D_MODEL), f32),
        "cache_k": jax.random.normal(ks[3], (DEPTH, n_pool, PAGE_SIZE, N_ATT_HEADS, 2 * DK), f32),
        "cache_v": jax.random.normal(ks[4], (DEPTH, n_pool, PAGE_SIZE, N_ATT_HEADS, DV), f32),
        "page_table": page_table,
        "norm_g": 1.0 + 0.02 * jax.random.normal(ks[5], (DEPTH, D_MODEL), f32),
        "w_in": jax.random.normal(ks[6], (DEPTH, D_MODEL, D_IN), f32) * D_MODEL ** -0.5,
        "lam_q1": 0.1 * jax.random.normal(ks[7], (DEPTH, DK), f32),
        "lam_k1": 0.1 * jax.random.normal(ks[8], (DEPTH, DK), f32),
        "lam_q2": 0.1 * jax.random.normal(ks[9], (DEPTH, DK), f32),
        "lam_k2": 0.1 * jax.random.normal(ks[10], (DEPTH, DK), f32),
        "subln_g": 1.0 + 0.02 * jax.random.normal(ks[11], (DEPTH, DV), f32),
        "ln_v_g": 1.0 + 0.02 * jax.random.normal(ks[12], (DEPTH, N_CH, CH_DIM), f32),
        "ln_v_b": 0.02 * jax.random.normal(ks[13], (DEPTH, N_CH, CH_DIM), f32),
        "w_s": jax.random.normal(ks[14], (DEPTH, N_CH, CHUNK, CHUNK), f32) * CHUNK ** -0.5,
        "b_s": 1.0 + 0.02 * jax.random.normal(ks[15], (DEPTH, N_CH, CHUNK), f32),
        "w_out": jax.random.normal(ks[16], (DEPTH, D_MIX, D_MODEL), f32) * D_MIX ** -0.5,
        "final_g": 1.0 + 0.02 * jax.random.normal(ks[17], (D_MODEL,), f32),
    }


def reference(x_prompt, x_sample, cache_k, cache_v, page_table, norm_g, w_in,
              lam_q1, lam_k1, lam_q2, lam_k2, subln_g, ln_v_g, ln_v_b, w_s, b_s,
              w_out, final_g):
    hp, hs = x_prompt, x_sample
    kp_l, vp_l, ks_l, vs_l, cs_l = [], [], [], [], []
    for l in range(DEPTH):
        ck, cv = cache_k[l], cache_v[l]
        hp, kp, vp, _ = mixer_layer(
            hp, l, attend_prompt, norm_g, w_in, lam_q1, lam_k1, lam_q2, lam_k2,
            subln_g, ln_v_g, ln_v_b, w_s, b_s, w_out)
        att_s = lambda q, k, v, lam, ck=ck, cv=cv: attend_sample(q, k, v, lam, ck, cv, page_table)
        hs, ks_, vs_, vns = mixer_layer(
            hs, l, att_s, norm_g, w_in, lam_q1, lam_k1, lam_q2, lam_k2,
            subln_g, ln_v_g, ln_v_b, w_s, b_s, w_out)
        kp_l.append(kp); vp_l.append(vp)
        ks_l.append(ks_); vs_l.append(vs_); cs_l.append(vns)
    y_prompt = rmsnorm(hp, final_g)
    y_sample = rmsnorm(hs, final_g)
    k_prompt = jnp.stack(kp_l)
    v_prompt = jnp.stack(vp_l)
    k_sample = jnp.stack(ks_l)
    v_sample = jnp.stack(vs_l)
    cmlp_v_sample = jnp.stack(cs_l)
    return (y_prompt, y_sample, k_prompt, v_prompt, k_sample, v_sample, cmlp_v_sample)
```

```python
import functools
import math

import numpy as np
import jax
import jax.numpy as jnp
from jax import lax
from jax.experimental import pallas as pl
from jax.experimental.pallas import tpu as pltpu

F32 = jnp.float32
BF16 = jnp.bfloat16

D_MODEL = 1024
N_HEADS = 4
DK = 64
HEAD_W = 128
D_GRP = N_HEADS * HEAD_W
N_GRP = 7
CHUNK = 128
PAGE = 128
EPS = 1e-6
NEG = -1e30
ONES_ROWS = 16

ROW_TILE = 512
ATT_BLK = 256
VMEM_LIMIT = 56 * 1024 * 1024


def _lambda_init(layer):
    return 0.8 - 0.6 * math.exp(-0.3 * layer)


def _alibi_slope(h):
    return 2.0 ** (-8.0 * (h + 1) / N_HEADS)


def _lam_value(lq1, lk1, lq2, lk2, lam_init):
    a = jnp.sum(lq1[...] * lk1[...], axis=-1, keepdims=True)
    b = jnp.sum(lq2[...] * lk2[...], axis=-1, keepdims=True)
    return jnp.exp(a) - jnp.exp(b) + lam_init


def _silu(x):
    return x * jax.nn.sigmoid(x)


def _inproj_kernel(x_ref, g_ref, w_ref, lng_ref, lnb_ref,
                   q_ref, k_ref, v_ref, sga_ref, ugc_ref, vn_ref):
    x = x_ref[...]
    ms = jnp.mean(x * x, axis=-1, keepdims=True)
    h = (x * lax.rsqrt(ms + EPS) * g_ref[...]).astype(BF16)

    def proj(i):
        return jnp.dot(h, w_ref[:, i * D_GRP:(i + 1) * D_GRP], preferred_element_type=F32)

    q_ref[...] = (proj(0) * (DK ** -0.5)).astype(q_ref.dtype)
    k_ref[...] = proj(1)
    v_ref[...] = proj(2)
    sga_ref[...] = _silu(proj(3)).astype(sga_ref.dtype)
    ugc_ref[...] = (proj(4) * _silu(proj(6))).astype(ugc_ref.dtype)
    vc = proj(5)
    for c in range(N_HEADS):
        sl = slice(c * HEAD_W, (c + 1) * HEAD_W)
        blk = vc[:, sl]
        mu = jnp.mean(blk, axis=-1, keepdims=True)
        xc = blk - mu
        var = jnp.mean(xc * xc, axis=-1, keepdims=True)
        y = xc * lax.rsqrt(var + EPS)
        vn_ref[:, sl] = (y * lng_ref[:, sl] + lnb_ref[:, sl]).astype(vn_ref.dtype)


def _inproj(x, g, w, lng, lnb, act_dtype):
    m = x.shape[0]
    tm = ROW_TILE
    row = lambda width: pl.BlockSpec((tm, width), lambda i: (i, 0))
    const = lambda shape: pl.BlockSpec(shape, lambda i: (0, 0))
    out = lambda dt: jax.ShapeDtypeStruct((m, D_GRP), dt)
    return pl.pallas_call(
        _inproj_kernel,
        grid=(m // tm,),
        in_specs=[row(D_MODEL), const((1, D_MODEL)), const((D_MODEL, N_GRP * D_GRP)),
                  const((1, D_GRP)), const((1, D_GRP))],
        out_specs=[row(D_GRP)] * 6,
        out_shape=[out(act_dtype), out(F32), out(F32), out(act_dtype), out(act_dtype), out(act_dtype)],
        compiler_params=pltpu.CompilerParams(dimension_semantics=("parallel",),
                                             vmem_limit_bytes=VMEM_LIMIT),
        name="inproj",
    )(x, g, w, lng, lnb)


def _prompt_attn_kernel(q_ref, k_ref, v_ref, sga_ref, sg_ref, lq1, lk1, lq2, lk2,
                        o_ref, kb_sc, vt_sc, m_sc, acc_sc, *, lam_init, tb, nblk):
    h = pl.program_id(1)
    i = pl.program_id(2)

    @pl.when(i == 0)
    def _():
        for jb in range(nblk):
            rows = slice(jb * tb, (jb + 1) * tb)
            kb_sc[jb] = k_ref[rows, :].astype(BF16)
            vt_sc[jb, 0:HEAD_W, :] = v_ref[rows, :].T.astype(BF16)
            vt_sc[jb, HEAD_W:HEAD_W + ONES_ROWS, :] = jnp.ones((ONES_ROWS, tb), BF16)

    slope = jnp.where(h == 0, _alibi_slope(0),
                      jnp.where(h == 1, _alibi_slope(1),
                                jnp.where(h == 2, _alibi_slope(2), _alibi_slope(3)))).astype(F32)

    q = q_ref[...]
    lane = lax.broadcasted_iota(jnp.int32, (tb, HEAD_W), 1)
    zero = jnp.zeros_like(q)
    qz = jnp.concatenate([jnp.where(lane < DK, q, zero), jnp.where(lane >= DK, q, zero)], axis=0)

    key_i = lax.broadcasted_iota(jnp.int32, (tb, 2 * tb), 0)
    qry_i = lax.broadcasted_iota(jnp.int32, (tb, 2 * tb), 1)
    qry_i = jnp.where(qry_i >= tb, qry_i - tb, qry_i)
    bias0 = key_i.astype(F32) * slope

    m_sc[...] = jnp.full(m_sc.shape, NEG, F32)
    acc_sc[...] = jnp.zeros(acc_sc.shape, F32)

    def step(j, diag):
        s = lax.dot_general(kb_sc[j], qz, (((1,), (1,)), ((), ())),
                            preferred_element_type=F32) + bias0
        if diag:
            s = jnp.where(key_i <= qry_i, s, NEG)
        tj = slope * (j * tb).astype(F32)
        m_old = m_sc[...]
        m_new = jnp.maximum(m_old, jnp.max(s, axis=0, keepdims=True) + tj)
        alpha = jnp.exp(m_old - m_new)
        p = jnp.exp(s - (m_new - tj)).astype(BF16)
        acc_sc[...] = acc_sc[...] * alpha + jnp.dot(vt_sc[j], p, preferred_element_type=F32)
        m_sc[...] = m_new

    def body(j, carry):
        step(j, False)
        return carry

    lax.fori_loop(0, i, body, 0)
    step(i, True)

    lam = _lam_value(lq1, lk1, lq2, lk2, lam_init)
    inv = 1.0 / acc_sc[HEAD_W:HEAD_W + 1, :]
    o1 = acc_sc[0:HEAD_W, 0:tb] * inv[:, 0:tb]
    o2 = acc_sc[0:HEAD_W, tb:2 * tb] * inv[:, tb:2 * tb]
    o = (o1 - lam * o2).T
    ms = jnp.mean(o * o, axis=-1, keepdims=True)
    att = o * lax.rsqrt(ms + EPS) * sg_ref[...] * (1.0 - lam_init)
    o_ref[...] = (att * sga_ref[...].astype(F32)).astype(o_ref.dtype)


def _prompt_attn(q, k, v, sga, sg, lams, lam_init, batch, seq):
    tb = ATT_BLK
    nblk = seq // tb
    qspec = pl.BlockSpec((tb, HEAD_W), lambda b, h, i: (b * nblk + i, h))
    kvspec = pl.BlockSpec((seq, HEAD_W), lambda b, h, i: (b, h))
    const = lambda shape: pl.BlockSpec(shape, lambda b, h, i: (0, 0))
    return pl.pallas_call(
        functools.partial(_prompt_attn_kernel, lam_init=lam_init, tb=tb, nblk=nblk),
        grid=(batch, N_HEADS, nblk),
        in_specs=[qspec, kvspec, kvspec, qspec, const((1, HEAD_W))] + [const((1, DK))] * 4,
        out_specs=qspec,
        out_shape=jax.ShapeDtypeStruct(q.shape, BF16),
        scratch_shapes=[pltpu.VMEM((nblk, tb, HEAD_W), BF16),
                        pltpu.VMEM((nblk, HEAD_W + ONES_ROWS, tb), BF16),
                        pltpu.VMEM((1, 2 * tb), F32),
                        pltpu.VMEM((HEAD_W + ONES_ROWS, 2 * tb), F32)],
        compiler_params=pltpu.CompilerParams(
            dimension_semantics=("parallel", "parallel", "arbitrary"),
            vmem_limit_bytes=VMEM_LIMIT),
        name="prompt_attn",
    )(q, k, v, sga, sg, *lams)


def _sample_bias(n_pages, dec_seq):
    past = n_pages * PAGE
    t = np.arange(dec_seq)
    bias = np.full((2, N_HEADS, dec_seq, past + PAGE), NEG, np.float32)
    for h in range(N_HEADS):
        sl = _alibi_slope(h)
        bias[:, h, :, :past] = -sl * (past + t[:, None] - np.arange(past)[None, :])
        self_b = np.where(t[None, :] <= t[:, None], -sl * (t[:, None] - t[None, :]), NEG)
        bias[:, h, :, past:past + dec_seq] = self_b
    return bias.reshape(2 * N_HEADS * dec_seq, past + PAGE)


def _sample_attn_kernel(pt_ref, q_ref, kn_ref, vn_ref, sga_ref, bias_ref, sg_ref, lq1, lk1, lq2, lk2,
                        *rest, lam_init, n_pages, dec_seq):
    k_pages = rest[:n_pages]
    v_pages = rest[n_pages:2 * n_pages]
    o_ref = rest[2 * n_pages]
    del pt_ref
    n_rows = 2 * N_HEADS * dec_seq
    half = N_HEADS * dec_seq

    q = q_ref[...]
    qt = jnp.concatenate([q] * (2 * N_HEADS), axis=0)
    row = lax.broadcasted_iota(jnp.int32, (n_rows, D_GRP), 0)
    lane = lax.broadcasted_iota(jnp.int32, (n_rows, D_GRP), 1)
    grp = 2 * ((row // dec_seq) % N_HEADS) + row // half
    qrows = jnp.where(lane // DK == grp, qt, 0.0).astype(BF16)

    pad = jnp.zeros((PAGE - dec_seq, D_GRP), F32)
    k_self = jnp.concatenate([kn_ref[...], pad], axis=0)
    v_self = jnp.concatenate([vn_ref[...], pad], axis=0)

    nt = (((1,), (1,)), ((), ()))
    s_parts = [lax.dot_general(qrows, kp[...].astype(BF16), nt, preferred_element_type=F32)
               for kp in k_pages]
    s_parts.append(lax.dot_general(qrows, k_self.astype(BF16), nt, preferred_element_type=F32))
    s = jnp.concatenate(s_parts, axis=1) + bias_ref[...]
    m = jnp.max(s, axis=-1, keepdims=True)
    p = jnp.exp(s - m)
    pn = p * (1.0 / jnp.sum(p, axis=-1, keepdims=True))
    lam = _lam_value(lq1, lk1, lq2, lk2, lam_init)
    pd = (pn[0:half, :] - lam * pn[half:n_rows, :]).astype(BF16)

    res = jnp.dot(pd[:, n_pages * PAGE:], v_self.astype(BF16), preferred_element_type=F32)
    for j, vp in enumerate(v_pages):
        res = res + jnp.dot(pd[:, j * PAGE:(j + 1) * PAGE], vp[...].astype(BF16),
                            preferred_element_type=F32)
    for h in range(N_HEADS):
        sl = slice(h * HEAD_W, (h + 1) * HEAD_W)
        o = res[h * dec_seq:(h + 1) * dec_seq, sl]
        ms = jnp.mean(o * o, axis=-1, keepdims=True)
        att = o * lax.rsqrt(ms + EPS) * sg_ref[...] * (1.0 - lam_init)
        o_ref[:, sl] = (att * sga_ref[:, sl]).astype(o_ref.dtype)


def _sample_attn(pt, q, kn, vn, sga, cache_k, cache_v, layer, sg, lams, lam_init, dec_batch, dec_seq, n_pages):
    bias = jnp.asarray(_sample_bias(n_pages, dec_seq))
    rowspec = pl.BlockSpec((dec_seq, D_GRP), lambda b, pt: (b, 0))
    const = lambda shape: pl.BlockSpec(shape, lambda b, pt: (0, 0))

    def page_spec(j):
        return pl.BlockSpec((None, None, PAGE, D_GRP),
                            lambda b, pt: (layer, pt[b * n_pages + j], 0, 0))

    pages = [page_spec(j) for j in range(n_pages)]
    grid_spec = pltpu.PrefetchScalarGridSpec(
        num_scalar_prefetch=1,
        grid=(dec_batch,),
        in_specs=[rowspec, rowspec, rowspec, rowspec, const(bias.shape), const((1, HEAD_W))]
                 + [const((1, DK))] * 4 + pages + pages,
        out_specs=rowspec,
    )
    return pl.pallas_call(
        functools.partial(_sample_attn_kernel, lam_init=lam_init, n_pages=n_pages, dec_seq=dec_seq),
        grid_spec=grid_spec,
        out_shape=jax.ShapeDtypeStruct(q.shape, F32),
        compiler_params=pltpu.CompilerParams(dimension_semantics=("parallel",),
                                             vmem_limit_bytes=VMEM_LIMIT),
        name="sample_attn",
    )(pt, q, kn, vn, sga, bias, sg, *lams, *([cache_k] * n_pages), *([cache_v] * n_pages))


def _mix_out_kernel(x_ref, att_ref, ugc_ref, vn_ref, ws_ref, bsb_ref, wout_ref, fg_ref,
                    o_ref, cat_sc, *, final):
    tm = x_ref.shape[0]
    nc = tm // CHUNK
    r = lax.broadcasted_iota(jnp.int32, (CHUNK, CHUNK), 0)
    c = lax.broadcasted_iota(jnp.int32, (CHUNK, CHUNK), 1)
    cat_sc[:, 0:D_GRP] = att_ref[...].astype(BF16)
    for h in range(N_HEADS):
        sl = slice(h * HEAD_W, (h + 1) * HEAD_W)
        ws = jnp.where(c <= r, ws_ref[h], 0.0).astype(BF16)
        vn = jnp.concatenate([vn_ref[ci * CHUNK:(ci + 1) * CHUNK, sl].astype(BF16)
                              for ci in range(nc)], axis=1)
        mixed = jnp.dot(ws, vn, preferred_element_type=F32)
        for ci in range(nc):
            rows = slice(ci * CHUNK, (ci + 1) * CHUNK)
            mc = mixed[:, ci * HEAD_W:(ci + 1) * HEAD_W] + bsb_ref[:, sl]
            cat_sc[rows, D_GRP + h * HEAD_W:D_GRP + (h + 1) * HEAD_W] = (
                ugc_ref[rows, sl].astype(F32) * mc).astype(BF16)
    y = x_ref[...] + jnp.dot(cat_sc[...], wout_ref[...], preferred_element_type=F32)
    if final:
        ms = jnp.mean(y * y, axis=-1, keepdims=True)
        y = y * lax.rsqrt(ms + EPS) * fg_ref[...]
    o_ref[...] = y


def _mix_out(x, att, ugc, vn, ws, bsb, wout, fg, final):
    m = x.shape[0]
    tm = ROW_TILE
    row = lambda width: pl.BlockSpec((tm, width), lambda i: (i, 0))
    return pl.pallas_call(
        functools.partial(_mix_out_kernel, final=final),
        grid=(m // tm,),
        in_specs=[row(D_MODEL), row(D_GRP), row(D_GRP), row(D_GRP),
                  pl.BlockSpec((N_HEADS, CHUNK, CHUNK), lambda i: (0, 0, 0)),
                  pl.BlockSpec((CHUNK, D_GRP), lambda i: (0, 0)),
                  pl.BlockSpec((2 * D_GRP, D_MODEL), lambda i: (0, 0)),
                  pl.BlockSpec((1, D_MODEL), lambda i: (0, 0))],
        out_specs=row(D_MODEL),
        out_shape=jax.ShapeDtypeStruct((m, D_MODEL), F32),
        scratch_shapes=[pltpu.VMEM((tm, 2 * D_GRP), BF16)],
        compiler_params=pltpu.CompilerParams(dimension_semantics=("parallel",),
                                             vmem_limit_bytes=VMEM_LIMIT),
        name="mix_out",
    )(x, att, ugc, vn, ws, bsb, wout, fg)


def kernel(x_prompt, x_sample, cache_k, cache_v, page_table, norm_g, w_in, lam_q1, lam_k1,
           lam_q2, lam_k2, subln_g, ln_v_g, ln_v_b, w_s, b_s, w_out, final_g):
    batch, seq, _ = x_prompt.shape
    dec_batch, dec_seq, _ = x_sample.shape
    depth, n_pool = cache_k.shape[0], cache_k.shape[1]
    n_pages = page_table.shape[1]
    assert seq % ATT_BLK == 0 and (batch * seq) % ROW_TILE == 0 and (dec_batch * dec_seq) % ROW_TILE == 0
    assert CHUNK % dec_seq == 0 and dec_seq % 8 == 0

    hp = x_prompt.reshape(batch * seq, D_MODEL)
    hs = x_sample.reshape(dec_batch * dec_seq, D_MODEL)
    ck = cache_k.reshape(depth, n_pool, PAGE, D_GRP)
    cv = cache_v.reshape(depth, n_pool, PAGE, D_GRP)
    pt = page_table.reshape(-1)
    w_in_b = w_in.astype(BF16)
    w_out_b = w_out.astype(BF16)
    fg = final_g.reshape(1, D_MODEL)
    eye = jnp.eye(CHUNK // dec_seq, dtype=F32)

    kp_l, vp_l, ks_l, vs_l, cs_l = [], [], [], [], []
    for l in range(depth):
        lam_init = _lambda_init(l)
        g = norm_g[l].reshape(1, D_MODEL)
        lng = ln_v_g[l].reshape(1, D_GRP)
        lnb = ln_v_b[l].reshape(1, D_GRP)
        sg = subln_g[l].reshape(1, HEAD_W)
        lams = [a[l].reshape(1, DK) for a in (lam_q1, lam_k1, lam_q2, lam_k2)]
        final = l == depth - 1

        q, k, v, sga, ugc, vn = _inproj(hp, g, w_in_b[l], lng, lnb, BF16)
        att = _prompt_attn(q, k, v, sga, sg, lams, lam_init, batch, seq)
        bsb = jnp.repeat(b_s[l].T, HEAD_W, axis=1)
        hp = _mix_out(hp, att, ugc, vn, w_s[l], bsb, w_out_b[l], fg, final)
        kp_l.append(k)
        vp_l.append(v)

        q, k, v, sga, ugc, vn = _inproj(hs, g, w_in_b[l], lng, lnb, F32)
        att = _sample_attn(pt, q, k, v, sga, ck, cv, l, sg, lams, lam_init, dec_batch, dec_seq, n_pages)
        ws8 = w_s[l][:, :dec_seq, :dec_seq]
        ws_bd = (eye[None, :, None, :, None] * ws8[:, None, :, None, :]).reshape(N_HEADS, CHUNK, CHUNK)
        bsb = jnp.tile(jnp.repeat(b_s[l][:, :dec_seq].T, HEAD_W, axis=1), (CHUNK // dec_seq, 1))
        hs = _mix_out(hs, att, ugc, vn, ws_bd, bsb, w_out_b[l], fg, final)
        ks_l.append(k)
        vs_l.append(v)
        cs_l.append(vn)

    shape_p = (depth, batch, seq, N_HEADS, HEAD_W)
    shape_s = (depth, dec_batch, dec_seq, N_HEADS, HEAD_W)
    return (hp.reshape(batch, seq, D_MODEL),
            hs.reshape(dec_batch, dec_seq, D_MODEL),
            jnp.stack(kp_l).reshape(shape_p),
            jnp.stack(vp_l).reshape(shape_p),
            jnp.stack(ks_l).reshape(shape_s),
            jnp.stack(vs_l).reshape(shape_s),
            jnp.stack(cs_l).reshape(shape_s))
```

```python
import functools
import math

import numpy as np
import jax
import jax.numpy as jnp
from jax import lax
from jax.experimental import pallas as pl
from jax.experimental.pallas import tpu as pltpu

F32 = jnp.float32
BF16 = jnp.bfloat16

D_MODEL = 1024
N_HEADS = 4
DK = 64
HEAD_W = 128
D_GRP = N_HEADS * HEAD_W
N_GRP = 7
CHUNK = 128
PAGE = 128
EPS = 1e-6
NEG = -1e30
ONES_ROWS = 16
POS_LANES = 128

ROW_TILE = 512
ATT_BLK = 256
VMEM_LIMIT = 56 * 1024 * 1024


def _lambda_init(layer):
    return 0.8 - 0.6 * math.exp(-0.3 * layer)


def _alibi_slope(h):
    return 2.0 ** (-8.0 * (h + 1) / N_HEADS)


def _lam_value(lq1, lk1, lq2, lk2, lam_init):
    a = jnp.sum(lq1[...] * lk1[...], axis=-1, keepdims=True)
    b = jnp.sum(lq2[...] * lk2[...], axis=-1, keepdims=True)
    return jnp.exp(a) - jnp.exp(b) + lam_init


def _silu(x):
    return x * jax.nn.sigmoid(x)


def _head_rows(h, n_rows, first_row=0):
    return pl.ds(first_row * N_HEADS + h, n_rows, stride=N_HEADS)


def _inproj_kernel(*refs, n_alias, with_vn_out):
    x_ref, g_ref, w_ref, lng_ref, lnb_ref = refs[:5]
    outs = refs[5 + n_alias:]
    q_ref, k_ref, v_ref, sga_ref, ugc_ref, vn_ref = outs[:6]
    tm = x_ref.shape[0]

    x = x_ref[...]
    ms = jnp.mean(x * x, axis=-1, keepdims=True)
    h = (x * lax.rsqrt(ms + EPS) * g_ref[...]).astype(BF16)

    def proj(i):
        return jnp.dot(h, w_ref[:, i * D_GRP:(i + 1) * D_GRP], preferred_element_type=F32)

    q_ref[...] = (proj(0) * (DK ** -0.5)).astype(q_ref.dtype)
    k = proj(1)
    v = proj(2)
    for c in range(N_HEADS):
        sl = slice(c * HEAD_W, (c + 1) * HEAD_W)
        k_ref[_head_rows(c, tm), :] = k[:, sl]
        v_ref[_head_rows(c, tm), :] = v[:, sl]
    sga_ref[...] = _silu(proj(3)).astype(sga_ref.dtype)
    ugc_ref[...] = (proj(4) * _silu(proj(6))).astype(ugc_ref.dtype)
    vc = proj(5)
    for c in range(N_HEADS):
        sl = slice(c * HEAD_W, (c + 1) * HEAD_W)
        blk = vc[:, sl]
        mu = jnp.mean(blk, axis=-1, keepdims=True)
        xc = blk - mu
        var = jnp.mean(xc * xc, axis=-1, keepdims=True)
        vn = xc * lax.rsqrt(var + EPS) * lng_ref[:, sl] + lnb_ref[:, sl]
        vn_ref[:, sl] = vn.astype(vn_ref.dtype)
        if with_vn_out:
            outs[6][_head_rows(c, tm), :] = vn


def _inproj(x, g, w, lng, lnb, act_dtype, layer, depth, prev, with_vn_out):
    m = x.shape[0]
    tm = ROW_TILE
    row = lambda width: pl.BlockSpec((tm, width), lambda i: (i, 0))
    const = lambda shape: pl.BlockSpec(shape, lambda i: (0, 0))
    act = lambda: jax.ShapeDtypeStruct((m, D_GRP), act_dtype)
    stacked = jax.ShapeDtypeStruct((depth, m * N_HEADS, HEAD_W), F32)
    stacked_spec = pl.BlockSpec((None, tm * N_HEADS, HEAD_W), lambda i: (layer, i, 0))
    n_stacked = 3 if with_vn_out else 2
    out_shape = [act(), stacked, stacked, act(), act(), act()] + [stacked] * (n_stacked - 2)
    out_specs = [row(D_GRP), stacked_spec, stacked_spec, row(D_GRP), row(D_GRP), row(D_GRP)]
    out_specs += [stacked_spec] * (n_stacked - 2)
    stacked_out_idx = [1, 2] + ([6] if with_vn_out else [])
    prev = list(prev) if prev is not None else []
    aliases = {5 + n: stacked_out_idx[n] for n in range(len(prev))}
    return pl.pallas_call(
        functools.partial(_inproj_kernel, n_alias=len(prev), with_vn_out=with_vn_out),
        grid=(m // tm,),
        in_specs=[row(D_MODEL), const((1, D_MODEL)), const((D_MODEL, N_GRP * D_GRP)),
                  const((1, D_GRP)), const((1, D_GRP))]
                 + [pl.BlockSpec(memory_space=pl.ANY)] * len(prev),
        out_specs=out_specs,
        out_shape=out_shape,
        input_output_aliases=aliases,
        compiler_params=pltpu.CompilerParams(dimension_semantics=("parallel",),
                                             vmem_limit_bytes=VMEM_LIMIT),
        name="inproj",
    )(x, g, w, lng, lnb, *prev)


def _prompt_attn_kernel(q_ref, k_ref, v_ref, sga_ref, sg_ref, lq1, lk1, lq2, lk2,
                        o_ref, kb_sc, vt_sc, *stat_sc, lam_init, tb, nblk):
    m_sc = stat_sc[:N_HEADS]
    acc_sc = stat_sc[N_HEADS:]
    i = pl.program_id(1)

    @pl.when(i == 0)
    def _():
        pos = lax.broadcasted_iota(jnp.int32, (tb, POS_LANES), 0)
        lane = lax.broadcasted_iota(jnp.int32, (tb, POS_LANES), 1)
        for jb in range(nblk):
            kpos = pos + jb * tb
            hi = (kpos // PAGE) * PAGE
            aug = jnp.where(lane == 0, hi, jnp.where(lane == 1, kpos - hi, 0)).astype(F32).astype(BF16)
            for h in range(N_HEADS):
                rows = _head_rows(h, tb, jb * tb)
                kb_sc[h, jb, :, 0:HEAD_W] = k_ref[rows, :].astype(BF16)
                kb_sc[h, jb, :, HEAD_W:HEAD_W + POS_LANES] = aug
                vt_sc[h, jb, 0:HEAD_W, :] = v_ref[rows, :].T.astype(BF16)
                vt_sc[h, jb, HEAD_W:HEAD_W + ONES_ROWS, :] = jnp.ones((ONES_ROWS, tb), BF16)

    lane = lax.broadcasted_iota(jnp.int32, (tb, HEAD_W), 1)
    qz = []
    for h in range(N_HEADS):
        q = q_ref[:, h * HEAD_W:(h + 1) * HEAD_W]
        zero = jnp.zeros_like(q)
        slope = jnp.where(lane < 2, _alibi_slope(h), 0.0).astype(BF16)
        q2 = jnp.concatenate([jnp.where(lane < DK, q, zero), jnp.where(lane >= DK, q, zero)], axis=0)
        qz.append(jnp.concatenate([q2, jnp.concatenate([slope, slope], axis=0)], axis=1))
        m_sc[h][...] = jnp.full(m_sc[h].shape, NEG, F32)
        acc_sc[h][...] = jnp.zeros(acc_sc[h].shape, F32)

    key_i = lax.broadcasted_iota(jnp.int32, (tb, 2 * tb), 0)
    qry_i = lax.broadcasted_iota(jnp.int32, (tb, 2 * tb), 1)
    qry_i = jnp.where(qry_i >= tb, qry_i - tb, qry_i)

    def steps(j, diag):
        heads = range(N_HEADS)
        s = [lax.dot_general(kb_sc[h, j], qz[h], (((1,), (1,)), ((), ())),
                             preferred_element_type=F32) for h in heads]
        if diag:
            s = [jnp.where(key_i <= qry_i, sh, NEG) for sh in s]
        m_old = [m_sc[h][...] for h in heads]
        m_new = [jnp.maximum(m_old[h], jnp.max(s[h], axis=0, keepdims=True)) for h in heads]
        p = [jnp.exp(s[h] - m_new[h]).astype(BF16) for h in heads]
        for h in heads:
            alpha = jnp.exp(m_old[h] - m_new[h])
            acc_sc[h][...] = acc_sc[h][...] * alpha + jnp.dot(vt_sc[h, j], p[h],
                                                              preferred_element_type=F32)
            m_sc[h][...] = m_new[h]

    def body(j, carry):
        steps(j, False)
        return carry

    lax.fori_loop(0, i, body, 0)
    steps(i, True)

    lam = _lam_value(lq1, lk1, lq2, lk2, lam_init)
    for h in range(N_HEADS):
        sl = slice(h * HEAD_W, (h + 1) * HEAD_W)
        acc = acc_sc[h]
        inv = 1.0 / acc[HEAD_W:HEAD_W + 1, :]
        o1 = acc[0:HEAD_W, 0:tb] * inv[:, 0:tb]
        o2 = acc[0:HEAD_W, tb:2 * tb] * inv[:, tb:2 * tb]
        o = (o1 - lam * o2).T
        ms = jnp.mean(o * o, axis=-1, keepdims=True)
        att = o * lax.rsqrt(ms + EPS) * sg_ref[...] * (1.0 - lam_init)
        o_ref[:, sl] = (att * sga_ref[:, sl].astype(F32)).astype(o_ref.dtype)


def _prompt_attn(q, kbuf, vbuf, layer, sga, sg, lams, lam_init, batch, seq):
    tb = ATT_BLK
    nblk = seq // tb
    qspec = pl.BlockSpec((tb, D_GRP), lambda b, i: (b * nblk + i, 0))
    kvspec = pl.BlockSpec((None, seq * N_HEADS, HEAD_W), lambda b, i: (layer, b, 0))
    const = lambda shape: pl.BlockSpec(shape, lambda b, i: (0, 0))
    stats = ([pltpu.VMEM((1, 2 * tb), F32)] * N_HEADS
             + [pltpu.VMEM((HEAD_W + ONES_ROWS, 2 * tb), F32)] * N_HEADS)
    return pl.pallas_call(
        functools.partial(_prompt_attn_kernel, lam_init=lam_init, tb=tb, nblk=nblk),
        grid=(batch, nblk),
        in_specs=[qspec, kvspec, kvspec, qspec, const((1, HEAD_W))] + [const((1, DK))] * 4,
        out_specs=qspec,
        out_shape=jax.ShapeDtypeStruct(q.shape, BF16),
        scratch_shapes=[pltpu.VMEM((N_HEADS, nblk, tb, HEAD_W + POS_LANES), BF16),
                        pltpu.VMEM((N_HEADS, nblk, HEAD_W + ONES_ROWS, tb), BF16)] + stats,
        compiler_params=pltpu.CompilerParams(
            dimension_semantics=("parallel", "arbitrary"),
            vmem_limit_bytes=VMEM_LIMIT),
        name="prompt_attn",
    )(q, kbuf, vbuf, sga, sg, *lams)


def _sample_bias(n_pages, dec_seq):
    past = n_pages * PAGE
    t = np.arange(dec_seq)
    bias = np.full((2, N_HEADS, dec_seq, past + PAGE), NEG, np.float32)
    for h in range(N_HEADS):
        sl = _alibi_slope(h)
        bias[:, h, :, :past] = -sl * (past + t[:, None] - np.arange(past)[None, :])
        self_b = np.where(t[None, :] <= t[:, None], -sl * (t[:, None] - t[None, :]), NEG)
        bias[:, h, :, past:past + dec_seq] = self_b
    return bias.reshape(2 * N_HEADS * dec_seq, past + PAGE)


def _sample_attn_kernel(pt_ref, q_ref, kn_ref, vn_ref, sga_ref, bias_ref, sg_ref, lq1, lk1, lq2, lk2,
                        *rest, lam_init, n_pages, dec_seq):
    k_pages = rest[:n_pages]
    v_pages = rest[n_pages:2 * n_pages]
    o_ref, kall_sc, vall_sc = rest[2 * n_pages:]
    del pt_ref
    n_rows = 2 * N_HEADS * dec_seq
    half = N_HEADS * dec_seq
    past = n_pages * PAGE

    pad = jnp.zeros((PAGE - dec_seq, HEAD_W), F32)
    for h in range(N_HEADS):
        sl = slice(h * HEAD_W, (h + 1) * HEAD_W)
        for j in range(n_pages):
            rows = slice(j * PAGE, (j + 1) * PAGE)
            kall_sc[rows, sl] = k_pages[j][_head_rows(h, PAGE), :].astype(BF16)
            vall_sc[rows, sl] = v_pages[j][_head_rows(h, PAGE), :].astype(BF16)
        k_self = jnp.concatenate([kn_ref[_head_rows(h, dec_seq), :], pad], axis=0)
        v_self = jnp.concatenate([vn_ref[_head_rows(h, dec_seq), :], pad], axis=0)
        kall_sc[past:past + PAGE, sl] = k_self.astype(BF16)
        vall_sc[past:past + PAGE, sl] = v_self.astype(BF16)

    q = q_ref[...]
    qt = jnp.concatenate([q] * (2 * N_HEADS), axis=0)
    row = lax.broadcasted_iota(jnp.int32, (n_rows, D_GRP), 0)
    lane = lax.broadcasted_iota(jnp.int32, (n_rows, D_GRP), 1)
    grp = 2 * ((row // dec_seq) % N_HEADS) + row // half
    qrows = jnp.where(lane // DK == grp, qt, 0.0).astype(BF16)

    s = lax.dot_general(qrows, kall_sc[...], (((1,), (1,)), ((), ())),
                        preferred_element_type=F32) + bias_ref[...]
    m = jnp.max(s, axis=-1, keepdims=True)
    p = jnp.exp(s - m)
    pn = p * (1.0 / jnp.sum(p, axis=-1, keepdims=True))
    lam = _lam_value(lq1, lk1, lq2, lk2, lam_init)
    pd = (pn[0:half, :] - lam * pn[half:n_rows, :]).astype(BF16)
    res = jnp.dot(pd, vall_sc[...], preferred_element_type=F32)
    for h in range(N_HEADS):
        sl = slice(h * HEAD_W, (h + 1) * HEAD_W)
        o = res[h * dec_seq:(h + 1) * dec_seq, sl]
        ms = jnp.mean(o * o, axis=-1, keepdims=True)
        att = o * lax.rsqrt(ms + EPS) * sg_ref[...] * (1.0 - lam_init)
        o_ref[:, sl] = (att * sga_ref[:, sl]).astype(o_ref.dtype)


def _sample_attn(pt, q, knbuf, vnbuf, sga, cache_k, cache_v, layer, sg, lams, lam_init,
                 dec_batch, dec_seq, n_pages):
    bias = jnp.asarray(_sample_bias(n_pages, dec_seq))
    rowspec = pl.BlockSpec((dec_seq, D_GRP), lambda b, pt: (b, 0))
    newspec = pl.BlockSpec((None, dec_seq * N_HEADS, HEAD_W), lambda b, pt: (layer, b, 0))
    const = lambda shape: pl.BlockSpec(shape, lambda b, pt: (0, 0))

    def page_spec(j):
        return pl.BlockSpec((None, None, PAGE * N_HEADS, HEAD_W),
                            lambda b, pt: (layer, pt[b * n_pages + j], 0, 0))

    pages = [page_spec(j) for j in range(n_pages)]
    grid_spec = pltpu.PrefetchScalarGridSpec(
        num_scalar_prefetch=1,
        grid=(dec_batch,),
        in_specs=[rowspec, newspec, newspec, rowspec, const(bias.shape), const((1, HEAD_W))]
                 + [const((1, DK))] * 4 + pages + pages,
        out_specs=rowspec,
        scratch_shapes=[pltpu.VMEM(((n_pages + 1) * PAGE, D_GRP), BF16)] * 2,
    )
    return pl.pallas_call(
        functools.partial(_sample_attn_kernel, lam_init=lam_init, n_pages=n_pages, dec_seq=dec_seq),
        grid_spec=grid_spec,
        out_shape=jax.ShapeDtypeStruct(q.shape, F32),
        compiler_params=pltpu.CompilerParams(dimension_semantics=("parallel",),
                                             vmem_limit_bytes=VMEM_LIMIT),
        name="sample_attn",
    )(pt, q, knbuf, vnbuf, sga, bias, sg, *lams, *([cache_k] * n_pages), *([cache_v] * n_pages))


def _mix_out_kernel(x_ref, att_ref, ugc_ref, vn_ref, ws_ref, bsb_ref, wout_ref, fg_ref,
                    o_ref, cat_sc, *, final):
    tm = x_ref.shape[0]
    nc = tm // CHUNK
    r = lax.broadcasted_iota(jnp.int32, (CHUNK, CHUNK), 0)
    c = lax.broadcasted_iota(jnp.int32, (CHUNK, CHUNK), 1)
    cat_sc[:, 0:D_GRP] = att_ref[...].astype(BF16)
    for h in range(N_HEADS):
        sl = slice(h * HEAD_W, (h + 1) * HEAD_W)
        ws = jnp.where(c <= r, ws_ref[h], 0.0).astype(BF16)
        vn = jnp.concatenate([vn_ref[ci * CHUNK:(ci + 1) * CHUNK, sl].astype(BF16)
                              for ci in range(nc)], axis=1)
        mixed = jnp.dot(ws, vn, preferred_element_type=F32)
        for ci in range(nc):
            rows = slice(ci * CHUNK, (ci + 1) * CHUNK)
            mc = mixed[:, ci * HEAD_W:(ci + 1) * HEAD_W] + bsb_ref[:, sl]
            cat_sc[rows, D_GRP + h * HEAD_W:D_GRP + (h + 1) * HEAD_W] = (
                ugc_ref[rows, sl].astype(F32) * mc).astype(BF16)
    y = x_ref[...] + jnp.dot(cat_sc[...], wout_ref[...], preferred_element_type=F32)
    if final:
        ms = jnp.mean(y * y, axis=-1, keepdims=True)
        y = y * lax.rsqrt(ms + EPS) * fg_ref[...]
    o_ref[...] = y


def _mix_out(x, att, ugc, vn, ws, bsb, wout, fg, final):
    m = x.shape[0]
    tm = ROW_TILE
    row = lambda width: pl.BlockSpec((tm, width), lambda i: (i, 0))
    return pl.pallas_call(
        functools.partial(_mix_out_kernel, final=final),
        grid=(m // tm,),
        in_specs=[row(D_MODEL), row(D_GRP), row(D_GRP), row(D_GRP),
                  pl.BlockSpec((N_HEADS, CHUNK, CHUNK), lambda i: (0, 0, 0)),
                  pl.BlockSpec((CHUNK, D_GRP), lambda i: (0, 0)),
                  pl.BlockSpec((2 * D_GRP, D_MODEL), lambda i: (0, 0)),
                  pl.BlockSpec((1, D_MODEL), lambda i: (0, 0))],
        out_specs=row(D_MODEL),
        out_shape=jax.ShapeDtypeStruct((m, D_MODEL), F32),
        scratch_shapes=[pltpu.VMEM((tm, 2 * D_GRP), BF16)],
        compiler_params=pltpu.CompilerParams(dimension_semantics=("parallel",),
                                             vmem_limit_bytes=VMEM_LIMIT),
        name="mix_out",
    )(x, att, ugc, vn, ws, bsb, wout, fg)


def kernel(x_prompt, x_sample, cache_k, cache_v, page_table, norm_g, w_in, lam_q1, lam_k1,
           lam_q2, lam_k2, subln_g, ln_v_g, ln_v_b, w_s, b_s, w_out, final_g):
    batch, seq, _ = x_prompt.shape
    dec_batch, dec_seq, _ = x_sample.shape
    depth, n_pool = cache_k.shape[0], cache_k.shape[1]
    n_pages = page_table.shape[1]
    assert seq % ATT_BLK == 0 and (batch * seq) % ROW_TILE == 0 and (dec_batch * dec_seq) % ROW_TILE == 0
    assert CHUNK % dec_seq == 0 and dec_seq % 8 == 0 and ATT_BLK % PAGE == 0

    hp = x_prompt.reshape(batch * seq, D_MODEL)
    hs = x_sample.reshape(dec_batch * dec_seq, D_MODEL)
    ck = cache_k.reshape(depth, n_pool, PAGE * N_HEADS, HEAD_W)
    cv = cache_v.reshape(depth, n_pool, PAGE * N_HEADS, HEAD_W)
    pt = page_table.reshape(-1)
    w_in_b = w_in.astype(BF16)
    w_out_b = w_out.astype(BF16)
    fg = final_g.reshape(1, D_MODEL)
    eye = jnp.eye(CHUNK // dec_seq, dtype=F32)

    prompt_bufs, sample_bufs = None, None
    for l in range(depth):
        lam_init = _lambda_init(l)
        g = norm_g[l].reshape(1, D_MODEL)
        lng = ln_v_g[l].reshape(1, D_GRP)
        lnb = ln_v_b[l].reshape(1, D_GRP)
        sg = subln_g[l].reshape(1, HEAD_W)
        lams = [a[l].reshape(1, DK) for a in (lam_q1, lam_k1, lam_q2, lam_k2)]
        final = l == depth - 1

        q, kbuf, vbuf, sga, ugc, vn = _inproj(hp, g, w_in_b[l], lng, lnb, BF16, l, depth,
                                              prompt_bufs, False)
        prompt_bufs = (kbuf, vbuf)
        att = _prompt_attn(q, kbuf, vbuf, l, sga, sg, lams, lam_init, batch, seq)
        bsb = jnp.repeat(b_s[l].T, HEAD_W, axis=1)
        hp = _mix_out(hp, att, ugc, vn, w_s[l], bsb, w_out_b[l], fg, final)

        q, kbuf, vbuf, sga, ugc, vn, vnbuf = _inproj(hs, g, w_in_b[l], lng, lnb, F32, l, depth,
                                                     sample_bufs, True)
        sample_bufs = (kbuf, vbuf, vnbuf)
        att = _sample_attn(pt, q, kbuf, vbuf, sga, ck, cv, l, sg, lams, lam_init,
                           dec_batch, dec_seq, n_pages)
        ws8 = w_s[l][:, :dec_seq, :dec_seq]
        ws_bd = (eye[None, :, None, :, None] * ws8[:, None, :, None, :]).reshape(N_HEADS, CHUNK, CHUNK)
        bsb = jnp.tile(jnp.repeat(b_s[l][:, :dec_seq].T, HEAD_W, axis=1), (CHUNK // dec_seq, 1))
        hs = _mix_out(hs, att, ugc, vn, ws_bd, bsb, w_out_b[l], fg, final)

    shape_p = (depth, batch, seq, N_HEADS, HEAD_W)
    shape_s = (depth, dec_batch, dec_seq, N_HEADS, HEAD_W)
    return (hp.reshape(batch, seq, D_MODEL),
            hs.reshape(dec_batch, dec_seq, D_MODEL),
            prompt_bufs[0].reshape(shape_p),
            prompt_bufs[1].reshape(shape_p),
            sample_bufs[0].reshape(shape_s),
            sample_bufs[1].reshape(shape_s),
            sample_bufs[2].reshape(shape_s))
```

```python
import functools
import math

import numpy as np
import jax
import jax.numpy as jnp
from jax import lax
from jax.experimental import pallas as pl
from jax.experimental.pallas import tpu as pltpu

F32 = jnp.float32
BF16 = jnp.bfloat16

D_MODEL = 1024
N_HEADS = 4
DK = 64
HEAD_W = 128
D_GRP = N_HEADS * HEAD_W
N_GRP = 7
CHUNK = 128
PAGE = 128
EPS = 1e-6
NEG = -1e30
ONES_ROWS = 16
POS_LANES = 128
SLOPE_PARTS = 3
LOG2E = np.float32(math.log2(math.e))

ROW_TILE = 512
ATT_BLK = 256
VMEM_LIMIT = 56 * 1024 * 1024


def _lambda_init(layer):
    return 0.8 - 0.6 * math.exp(-0.3 * layer)


def _alibi_slope(h):
    return 2.0 ** (-8.0 * (h + 1) / N_HEADS)


def _slope_parts(h):
    target = np.float32(_alibi_slope(h)) * LOG2E
    rest, parts = target, []
    for _ in range(SLOPE_PARTS):
        part = np.asarray(rest, dtype=BF16).astype(np.float32)
        parts.append(float(part))
        rest = np.float32(rest - part)
    assert rest == 0.0
    return parts


def _lam_value(lq1, lk1, lq2, lk2, lam_init):
    a = jnp.sum(lq1[...] * lk1[...], axis=-1, keepdims=True)
    b = jnp.sum(lq2[...] * lk2[...], axis=-1, keepdims=True)
    return jnp.exp(a) - jnp.exp(b) + lam_init


def _silu(x):
    return x * jax.nn.sigmoid(x)


def _head_rows(h, n_rows, first_row=0):
    return pl.ds(first_row * N_HEADS + h, n_rows, stride=N_HEADS)


def _inproj_kernel(*refs, n_alias, with_vn_out):
    x_ref, g_ref, w_ref, lng_ref, lnb_ref = refs[:5]
    outs = refs[5 + n_alias:]
    q_ref, k_ref, v_ref, sga_ref, ugc_ref, vn_ref = outs[:6]
    tm = x_ref.shape[0]

    x = x_ref[...]
    ms = jnp.mean(x * x, axis=-1, keepdims=True)
    h = (x * lax.rsqrt(ms + EPS) * g_ref[...]).astype(BF16)

    def proj(i):
        return jnp.dot(h, w_ref[:, i * D_GRP:(i + 1) * D_GRP], preferred_element_type=F32)

    q_ref[...] = (proj(0) * float(np.float32(DK ** -0.5) * LOG2E)).astype(q_ref.dtype)
    k = proj(1)
    v = proj(2)
    for c in range(N_HEADS):
        sl = slice(c * HEAD_W, (c + 1) * HEAD_W)
        k_ref[_head_rows(c, tm), :] = k[:, sl]
        v_ref[_head_rows(c, tm), :] = v[:, sl]
    sga_ref[...] = _silu(proj(3)).astype(sga_ref.dtype)
    ugc_ref[...] = (proj(4) * _silu(proj(6))).astype(ugc_ref.dtype)
    vc = proj(5)
    for c in range(N_HEADS):
        sl = slice(c * HEAD_W, (c + 1) * HEAD_W)
        blk = vc[:, sl]
        mu = jnp.mean(blk, axis=-1, keepdims=True)
        xc = blk - mu
        var = jnp.mean(xc * xc, axis=-1, keepdims=True)
        vn = xc * lax.rsqrt(var + EPS) * lng_ref[:, sl] + lnb_ref[:, sl]
        vn_ref[:, sl] = vn.astype(vn_ref.dtype)
        if with_vn_out:
            outs[6][_head_rows(c, tm), :] = vn


def _inproj(x, g, w, lng, lnb, act_dtype, layer, depth, prev, with_vn_out):
    m = x.shape[0]
    tm = ROW_TILE
    row = lambda width: pl.BlockSpec((tm, width), lambda i: (i, 0))
    const = lambda shape: pl.BlockSpec(shape, lambda i: (0, 0))
    act = lambda: jax.ShapeDtypeStruct((m, D_GRP), act_dtype)
    stacked = jax.ShapeDtypeStruct((depth, m * N_HEADS, HEAD_W), F32)
    stacked_spec = pl.BlockSpec((None, tm * N_HEADS, HEAD_W), lambda i: (layer, i, 0))
    n_stacked = 3 if with_vn_out else 2
    out_shape = [act(), stacked, stacked, act(), act(), act()] + [stacked] * (n_stacked - 2)
    out_specs = [row(D_GRP), stacked_spec, stacked_spec, row(D_GRP), row(D_GRP), row(D_GRP)]
    out_specs += [stacked_spec] * (n_stacked - 2)
    stacked_out_idx = [1, 2] + ([6] if with_vn_out else [])
    prev = list(prev) if prev is not None else []
    aliases = {5 + n: stacked_out_idx[n] for n in range(len(prev))}
    return pl.pallas_call(
        functools.partial(_inproj_kernel, n_alias=len(prev), with_vn_out=with_vn_out),
        grid=(m // tm,),
        in_specs=[row(D_MODEL), const((1, D_MODEL)), const((D_MODEL, N_GRP * D_GRP)),
                  const((1, D_GRP)), const((1, D_GRP))]
                 + [pl.BlockSpec(memory_space=pl.ANY)] * len(prev),
        out_specs=out_specs,
        out_shape=out_shape,
        input_output_aliases=aliases,
        compiler_params=pltpu.CompilerParams(dimension_semantics=("parallel",),
                                             vmem_limit_bytes=VMEM_LIMIT),
        name="inproj",
    )(x, g, w, lng, lnb, *prev)


def _prompt_attn_kernel(q_ref, k_ref, v_ref, sga_ref, sg_ref, lq1, lk1, lq2, lk2,
                        o_ref, kb_sc, vt_sc, *stat_sc, lam_init, tb, nblk):
    m_sc = stat_sc[:N_HEADS]
    acc_sc = stat_sc[N_HEADS:]
    i = pl.program_id(1)

    @pl.when(i == 0)
    def _():
        pos = lax.broadcasted_iota(jnp.int32, (tb, POS_LANES), 0)
        lane = lax.broadcasted_iota(jnp.int32, (tb, POS_LANES), 1)
        for jb in range(nblk):
            kpos = pos + jb * tb
            hi = (kpos // PAGE) * PAGE
            aug = jnp.where(lane >= 2 * SLOPE_PARTS, 0, jnp.where(lane % 2 == 0, hi, kpos - hi))
            aug = aug.astype(F32).astype(BF16)
            for h in range(N_HEADS):
                rows = _head_rows(h, tb, jb * tb)
                kb_sc[h, jb * tb:(jb + 1) * tb, 0:HEAD_W] = k_ref[rows, :].astype(BF16)
                kb_sc[h, jb * tb:(jb + 1) * tb, HEAD_W:HEAD_W + POS_LANES] = aug
                vt_sc[h, jb, 0:HEAD_W, :] = v_ref[rows, :].T.astype(BF16)
                vt_sc[h, jb, HEAD_W:HEAD_W + ONES_ROWS, :] = jnp.ones((ONES_ROWS, tb), BF16)

    lane = lax.broadcasted_iota(jnp.int32, (tb, HEAD_W), 1)
    qz = []
    for h in range(N_HEADS):
        q = q_ref[:, h * HEAD_W:(h + 1) * HEAD_W]
        zero = jnp.zeros_like(q)
        slope = jnp.zeros((tb, POS_LANES), F32)
        for n, part in enumerate(_slope_parts(h)):
            slope = jnp.where(lane // 2 == n, part, slope)
        slope = slope.astype(BF16)
        q2 = jnp.concatenate([jnp.where(lane < DK, q, zero), jnp.where(lane >= DK, q, zero)], axis=0)
        qz.append(jnp.concatenate([q2, jnp.concatenate([slope, slope], axis=0)], axis=1))
        m_sc[h][...] = jnp.full(m_sc[h].shape, NEG, F32)
        acc_sc[h][...] = jnp.zeros(acc_sc[h].shape, F32)

    def steps(first_blk, nkb, masked):
        heads = range(N_HEADS)
        rows = pl.ds(pl.multiple_of(first_blk * tb, tb), nkb * tb)
        s = [lax.dot_general(kb_sc[h, rows, :], qz[h], (((1,), (1,)), ((), ())),
                             preferred_element_type=F32) for h in heads]
        if masked:
            key_i = lax.broadcasted_iota(jnp.int32, (nkb * tb, 2 * tb), 0)
            qry_i = lax.broadcasted_iota(jnp.int32, (nkb * tb, 2 * tb), 1)
            qry_i = jnp.where(qry_i >= tb, qry_i - tb, qry_i) + (nkb - 1) * tb
            s = [jnp.where(key_i <= qry_i, sh, NEG) for sh in s]
        m_old = [m_sc[h][...] for h in heads]
        m_new = [jnp.maximum(m_old[h], jnp.max(s[h], axis=0, keepdims=True)) for h in heads]
        p = [jnp.exp2(s[h] - m_new[h]).astype(BF16) for h in heads]
        for h in heads:
            pv = jnp.dot(vt_sc[h, first_blk], p[h][0:tb, :], preferred_element_type=F32)
            for kb in range(1, nkb):
                pv = pv + jnp.dot(vt_sc[h, first_blk + kb], p[h][kb * tb:(kb + 1) * tb, :],
                                  preferred_element_type=F32)
            acc_sc[h][...] = acc_sc[h][...] * jnp.exp2(m_old[h] - m_new[h]) + pv
            m_sc[h][...] = m_new[h]

    def body(jj, carry):
        steps(2 * jj, 2, False)
        return carry

    lax.fori_loop(0, i // 2, body, 0)

    @pl.when(i % 2 == 1)
    def _():
        steps(i - 1, 2, True)

    @pl.when(i % 2 == 0)
    def _():
        steps(i, 1, True)

    lam = _lam_value(lq1, lk1, lq2, lk2, lam_init)
    for h in range(N_HEADS):
        sl = slice(h * HEAD_W, (h + 1) * HEAD_W)
        acc = acc_sc[h]
        inv = 1.0 / acc[HEAD_W:HEAD_W + 1, :]
        o1 = acc[0:HEAD_W, 0:tb] * inv[:, 0:tb]
        o2 = acc[0:HEAD_W, tb:2 * tb] * inv[:, tb:2 * tb]
        o = (o1 - lam * o2).T
        ms = jnp.mean(o * o, axis=-1, keepdims=True)
        att = o * lax.rsqrt(ms + EPS) * sg_ref[...] * (1.0 - lam_init)
        o_ref[:, sl] = (att * sga_ref[:, sl].astype(F32)).astype(o_ref.dtype)


def _prompt_attn(q, kbuf, vbuf, layer, sga, sg, lams, lam_init, batch, seq):
    tb = ATT_BLK
    nblk = seq // tb
    qspec = pl.BlockSpec((tb, D_GRP), lambda b, i: (b * nblk + i, 0))
    kvspec = pl.BlockSpec((None, seq * N_HEADS, HEAD_W), lambda b, i: (layer, b, 0))
    const = lambda shape: pl.BlockSpec(shape, lambda b, i: (0, 0))
    stats = ([pltpu.VMEM((1, 2 * tb), F32)] * N_HEADS
             + [pltpu.VMEM((HEAD_W + ONES_ROWS, 2 * tb), F32)] * N_HEADS)
    return pl.pallas_call(
        functools.partial(_prompt_attn_kernel, lam_init=lam_init, tb=tb, nblk=nblk),
        grid=(batch, nblk),
        in_specs=[qspec, kvspec, kvspec, qspec, const((1, HEAD_W))] + [const((1, DK))] * 4,
        out_specs=qspec,
        out_shape=jax.ShapeDtypeStruct(q.shape, BF16),
        scratch_shapes=[pltpu.VMEM((N_HEADS, seq, HEAD_W + POS_LANES), BF16),
                        pltpu.VMEM((N_HEADS, nblk, HEAD_W + ONES_ROWS, tb), BF16)] + stats,
        compiler_params=pltpu.CompilerParams(
            dimension_semantics=("parallel", "arbitrary"),
            vmem_limit_bytes=VMEM_LIMIT),
        name="prompt_attn",
    )(q, kbuf, vbuf, sga, sg, *lams)


def _sample_bias(n_pages, dec_seq):
    past = n_pages * PAGE
    t = np.arange(dec_seq)
    bias = np.full((2, N_HEADS, dec_seq, past + PAGE), NEG, np.float32)
    for h in range(N_HEADS):
        sl = _alibi_slope(h)
        bias[:, h, :, :past] = -sl * (past + t[:, None] - np.arange(past)[None, :])
        self_b = np.where(t[None, :] <= t[:, None], -sl * (t[:, None] - t[None, :]), NEG)
        bias[:, h, :, past:past + dec_seq] = self_b
    bias = np.where(bias > NEG, bias * LOG2E, NEG).astype(np.float32)
    return bias.reshape(2 * N_HEADS * dec_seq, past + PAGE)


def _sample_attn_kernel(pt_ref, q_ref, kn_ref, vn_ref, sga_ref, bias_ref, sg_ref, lq1, lk1, lq2, lk2,
                        *rest, lam_init, n_pages, dec_seq):
    k_pages = rest[:n_pages]
    v_pages = rest[n_pages:2 * n_pages]
    o_ref, kall_sc, vall_sc = rest[2 * n_pages:]
    del pt_ref
    n_rows = 2 * N_HEADS * dec_seq
    half = N_HEADS * dec_seq
    past = n_pages * PAGE

    pad = jnp.zeros((PAGE - dec_seq, HEAD_W), F32)
    for h in range(N_HEADS):
        sl = slice(h * HEAD_W, (h + 1) * HEAD_W)
        for j in range(n_pages):
            rows = slice(j * PAGE, (j + 1) * PAGE)
            kall_sc[rows, sl] = k_pages[j][_head_rows(h, PAGE), :].astype(BF16)
            vall_sc[rows, sl] = v_pages[j][_head_rows(h, PAGE), :].astype(BF16)
        k_self = jnp.concatenate([kn_ref[_head_rows(h, dec_seq), :], pad], axis=0)
        v_self = jnp.concatenate([vn_ref[_head_rows(h, dec_seq), :], pad], axis=0)
        kall_sc[past:past + PAGE, sl] = k_self.astype(BF16)
        vall_sc[past:past + PAGE, sl] = v_self.astype(BF16)

    q = q_ref[...]
    qt = jnp.concatenate([q] * (2 * N_HEADS), axis=0)
    row = lax.broadcasted_iota(jnp.int32, (n_rows, D_GRP), 0)
    lane = lax.broadcasted_iota(jnp.int32, (n_rows, D_GRP), 1)
    grp = 2 * ((row // dec_seq) % N_HEADS) + row // half
    qrows = jnp.where(lane // DK == grp, qt, 0.0).astype(BF16)

    s = lax.dot_general(qrows, kall_sc[...], (((1,), (1,)), ((), ())),
                        preferred_element_type=F32) + bias_ref[...]
    m = jnp.max(s, axis=-1, keepdims=True)
    p = jnp.exp2(s - m)
    pn = p * (1.0 / jnp.sum(p, axis=-1, keepdims=True))
    lam = _lam_value(lq1, lk1, lq2, lk2, lam_init)
    pd = (pn[0:half, :] - lam * pn[half:n_rows, :]).astype(BF16)
    res = jnp.dot(pd, vall_sc[...], preferred_element_type=F32)
    for h in range(N_HEADS):
        sl = slice(h * HEAD_W, (h + 1) * HEAD_W)
        o = res[h * dec_seq:(h + 1) * dec_seq, sl]
        ms = jnp.mean(o * o, axis=-1, keepdims=True)
        att = o * lax.rsqrt(ms + EPS) * sg_ref[...] * (1.0 - lam_init)
        o_ref[:, sl] = (att * sga_ref[:, sl]).astype(o_ref.dtype)


def _sample_attn(pt, q, knbuf, vnbuf, sga, cache_k, cache_v, layer, sg, lams, lam_init,
                 dec_batch, dec_seq, n_pages):
    bias = jnp.asarray(_sample_bias(n_pages, dec_seq))
    rowspec = pl.BlockSpec((dec_seq, D_GRP), lambda b, pt: (b, 0))
    newspec = pl.BlockSpec((None, dec_seq * N_HEADS, HEAD_W), lambda b, pt: (layer, b, 0))
    const = lambda shape: pl.BlockSpec(shape, lambda b, pt: (0, 0))

    def page_spec(j):
        return pl.BlockSpec((None, None, PAGE * N_HEADS, HEAD_W),
                            lambda b, pt: (layer, pt[b * n_pages + j], 0, 0))

    pages = [page_spec(j) for j in range(n_pages)]
    grid_spec = pltpu.PrefetchScalarGridSpec(
        num_scalar_prefetch=1,
        grid=(dec_batch,),
        in_specs=[rowspec, newspec, newspec, rowspec, const(bias.shape), const((1, HEAD_W))]
                 + [const((1, DK))] * 4 + pages + pages,
        out_specs=rowspec,
        scratch_shapes=[pltpu.VMEM(((n_pages + 1) * PAGE, D_GRP), BF16)] * 2,
    )
    return pl.pallas_call(
        functools.partial(_sample_attn_kernel, lam_init=lam_init, n_pages=n_pages, dec_seq=dec_seq),
        grid_spec=grid_spec,
        out_shape=jax.ShapeDtypeStruct(q.shape, F32),
        compiler_params=pltpu.CompilerParams(dimension_semantics=("parallel",),
                                             vmem_limit_bytes=VMEM_LIMIT),
        name="sample_attn",
    )(pt, q, knbuf, vnbuf, sga, bias, sg, *lams, *([cache_k] * n_pages), *([cache_v] * n_pages))


def _mix_out_kernel(x_ref, att_ref, ugc_ref, vn_ref, ws_ref, bsb_ref, wout_ref, fg_ref,
                    o_ref, cat_sc, *, final):
    tm = x_ref.shape[0]
    nc = tm // CHUNK
    r = lax.broadcasted_iota(jnp.int32, (CHUNK, CHUNK), 0)
    c = lax.broadcasted_iota(jnp.int32, (CHUNK, CHUNK), 1)
    cat_sc[:, 0:D_GRP] = att_ref[...].astype(BF16)
    for h in range(N_HEADS):
        sl = slice(h * HEAD_W, (h + 1) * HEAD_W)
        ws = jnp.where(c <= r, ws_ref[h], 0.0).astype(BF16)
        vn = jnp.concatenate([vn_ref[ci * CHUNK:(ci + 1) * CHUNK, sl].astype(BF16)
                              for ci in range(nc)], axis=1)
        mixed = jnp.dot(ws, vn, preferred_element_type=F32)
        for ci in range(nc):
            rows = slice(ci * CHUNK, (ci + 1) * CHUNK)
            mc = mixed[:, ci * HEAD_W:(ci + 1) * HEAD_W] + bsb_ref[:, sl]
            cat_sc[rows, D_GRP + h * HEAD_W:D_GRP + (h + 1) * HEAD_W] = (
                ugc_ref[rows, sl].astype(F32) * mc).astype(BF16)
    y = x_ref[...] + jnp.dot(cat_sc[...], wout_ref[...], preferred_element_type=F32)
    if final:
        ms = jnp.mean(y * y, axis=-1, keepdims=True)
        y = y * lax.rsqrt(ms + EPS) * fg_ref[...]
    o_ref[...] = y


def _mix_out(x, att, ugc, vn, ws, bsb, wout, fg, final):
    m = x.shape[0]
    tm = ROW_TILE
    row = lambda width: pl.BlockSpec((tm, width), lambda i: (i, 0))
    return pl.pallas_call(
        functools.partial(_mix_out_kernel, final=final),
        grid=(m // tm,),
        in_specs=[row(D_MODEL), row(D_GRP), row(D_GRP), row(D_GRP),
                  pl.BlockSpec((N_HEADS, CHUNK, CHUNK), lambda i: (0, 0, 0)),
                  pl.BlockSpec((CHUNK, D_GRP), lambda i: (0, 0)),
                  pl.BlockSpec((2 * D_GRP, D_MODEL), lambda i: (0, 0)),
                  pl.BlockSpec((1, D_MODEL), lambda i: (0, 0))],
        out_specs=row(D_MODEL),
        out_shape=jax.ShapeDtypeStruct((m, D_MODEL), F32),
        scratch_shapes=[pltpu.VMEM((tm, 2 * D_GRP), BF16)],
        compiler_params=pltpu.CompilerParams(dimension_semantics=("parallel",),
                                             vmem_limit_bytes=VMEM_LIMIT),
        name="mix_out",
    )(x, att, ugc, vn, ws, bsb, wout, fg)


def kernel(x_prompt, x_sample, cache_k, cache_v, page_table, norm_g, w_in, lam_q1, lam_k1,
           lam_q2, lam_k2, subln_g, ln_v_g, ln_v_b, w_s, b_s, w_out, final_g):
    batch, seq, _ = x_prompt.shape
    dec_batch, dec_seq, _ = x_sample.shape
    depth, n_pool = cache_k.shape[0], cache_k.shape[1]
    n_pages = page_table.shape[1]
    assert seq % ATT_BLK == 0 and (batch * seq) % ROW_TILE == 0 and (dec_batch * dec_seq) % ROW_TILE == 0
    assert CHUNK % dec_seq == 0 and dec_seq % 8 == 0 and ATT_BLK % PAGE == 0

    hp = x_prompt.reshape(batch * seq, D_MODEL)
    hs = x_sample.reshape(dec_batch * dec_seq, D_MODEL)
    ck = cache_k.reshape(depth, n_pool, PAGE * N_HEADS, HEAD_W)
    cv = cache_v.reshape(depth, n_pool, PAGE * N_HEADS, HEAD_W)
    pt = page_table.reshape(-1)
    w_in_b = w_in.astype(BF16)
    w_out_b = w_out.astype(BF16)
    fg = final_g.reshape(1, D_MODEL)
    eye = jnp.eye(CHUNK // dec_seq, dtype=F32)

    prompt_bufs, sample_bufs = None, None
    for l in range(depth):
        lam_init = _lambda_init(l)
        g = norm_g[l].reshape(1, D_MODEL)
        lng = ln_v_g[l].reshape(1, D_GRP)
        lnb = ln_v_b[l].reshape(1, D_GRP)
        sg = subln_g[l].reshape(1, HEAD_W)
        lams = [a[l].reshape(1, DK) for a in (lam_q1, lam_k1, lam_q2, lam_k2)]
        final = l == depth - 1

        q, kbuf, vbuf, sga, ugc, vn = _inproj(hp, g, w_in_b[l], lng, lnb, BF16, l, depth,
                                              prompt_bufs, False)
        prompt_bufs = (kbuf, vbuf)
        att = _prompt_attn(q, kbuf, vbuf, l, sga, sg, lams, lam_init, batch, seq)
        bsb = jnp.repeat(b_s[l].T, HEAD_W, axis=1)
        hp = _mix_out(hp, att, ugc, vn, w_s[l], bsb, w_out_b[l], fg, final)

        q, kbuf, vbuf, sga, ugc, vn, vnbuf = _inproj(hs, g, w_in_b[l], lng, lnb, F32, l, depth,
                                                     sample_bufs, True)
        sample_bufs = (kbuf, vbuf, vnbuf)
        att = _sample_attn(pt, q, kbuf, vbuf, sga, ck, cv, l, sg, lams, lam_init,
                           dec_batch, dec_seq, n_pages)
        ws8 = w_s[l][:, :dec_seq, :dec_seq]
        ws_bd = (eye[None, :, None, :, None] * ws8[:, None, :, None, :]).reshape(N_HEADS, CHUNK, CHUNK)
        bsb = jnp.tile(jnp.repeat(b_s[l][:, :dec_seq].T, HEAD_W, axis=1), (CHUNK // dec_seq, 1))
        hs = _mix_out(hs, att, ugc, vn, ws_bd, bsb, w_out_b[l], fg, final)

    shape_p = (depth, batch, seq, N_HEADS, HEAD_W)
    shape_s = (depth, dec_batch, dec_seq, N_HEADS, HEAD_W)
    return (hp.reshape(batch, seq, D_MODEL),
            hs.reshape(dec_batch, dec_seq, D_MODEL),
            prompt_bufs[0].reshape(shape_p),
            prompt_bufs[1].reshape(shape_p),
            sample_bufs[0].reshape(shape_s),
            sample_bufs[1].reshape(shape_s),
            sample_bufs[2].reshape(shape_s))
```

```python
import functools
import math

import numpy as np
import jax
import jax.numpy as jnp
from jax import lax
from jax.experimental import pallas as pl
from jax.experimental.pallas import tpu as pltpu

F32 = jnp.float32
BF16 = jnp.bfloat16

D_MODEL = 1024
N_HEADS = 4
DK = 64
HEAD_W = 128
D_GRP = N_HEADS * HEAD_W
N_GRP = 7
CHUNK = 128
PAGE = 128
EPS = 1e-6
NEG = -1e30
ONES_ROWS = 16
POS_LANES = 128
SLOPE_PARTS = 3
LOG2E = np.float32(math.log2(math.e))

ROW_TILE = 512
ATT_BLK = 256
DEC_PER_STEP = 2
NEW_ROWS = 128
VMEM_LIMIT = 56 * 1024 * 1024


def _lambda_init(layer):
    return 0.8 - 0.6 * math.exp(-0.3 * layer)


def _alibi_slope(h):
    return 2.0 ** (-8.0 * (h + 1) / N_HEADS)


def _slope_parts(h):
    target = np.float32(_alibi_slope(h)) * LOG2E
    rest, parts = target, []
    for _ in range(SLOPE_PARTS):
        part = np.asarray(rest, dtype=BF16).astype(np.float32)
        parts.append(float(part))
        rest = np.float32(rest - part)
    assert rest == 0.0
    return parts


def _lam_value(lq1, lk1, lq2, lk2, lam_init):
    a = jnp.sum(lq1[...] * lk1[...], axis=-1, keepdims=True)
    b = jnp.sum(lq2[...] * lk2[...], axis=-1, keepdims=True)
    return jnp.exp(a) - jnp.exp(b) + lam_init


def _silu(x):
    return x * jax.nn.sigmoid(x)


def _head_rows(h, n_rows, first_row=0):
    return pl.ds(first_row * N_HEADS + h, n_rows, stride=N_HEADS)


def _inproj_kernel(*refs, n_alias, with_vn_out):
    x_ref, g_ref, w_ref, lng_ref, lnb_ref = refs[:5]
    outs = refs[5 + n_alias:]
    q_ref, k_ref, v_ref, sga_ref, ugc_ref, vn_ref = outs[:6]
    tm = x_ref.shape[0]

    x = x_ref[...]
    ms = jnp.mean(x * x, axis=-1, keepdims=True)
    h = (x * lax.rsqrt(ms + EPS) * g_ref[...]).astype(BF16)

    def proj(i):
        return jnp.dot(h, w_ref[:, i * D_GRP:(i + 1) * D_GRP], preferred_element_type=F32)

    def store_heads(ref, c, val):
        if len(ref.shape) == 3:
            for d in range(ref.shape[0]):
                ref[d, _head_rows(c, tm), :] = val
        else:
            ref[_head_rows(c, tm), :] = val

    q_ref[...] = (proj(0) * float(np.float32(DK ** -0.5) * LOG2E)).astype(q_ref.dtype)
    k = proj(1)
    v = proj(2)
    for c in range(N_HEADS):
        sl = slice(c * HEAD_W, (c + 1) * HEAD_W)
        store_heads(k_ref, c, k[:, sl])
        store_heads(v_ref, c, v[:, sl])
    sga_ref[...] = _silu(proj(3)).astype(sga_ref.dtype)
    ugc_ref[...] = (proj(4) * _silu(proj(6))).astype(ugc_ref.dtype)
    vc = proj(5)
    for c in range(N_HEADS):
        sl = slice(c * HEAD_W, (c + 1) * HEAD_W)
        blk = vc[:, sl]
        mu = jnp.mean(blk, axis=-1, keepdims=True)
        xc = blk - mu
        var = jnp.mean(xc * xc, axis=-1, keepdims=True)
        vn = xc * lax.rsqrt(var + EPS) * lng_ref[:, sl] + lnb_ref[:, sl]
        vn_ref[:, sl] = vn.astype(vn_ref.dtype)
        if with_vn_out:
            store_heads(outs[6], c, vn)


def _inproj(x, g, w, lng, lnb, act_dtype, layer, depth, prev, with_vn_out):
    m = x.shape[0]
    tm = ROW_TILE
    row = lambda width: pl.BlockSpec((tm, width), lambda i: (i, 0))
    const = lambda shape: pl.BlockSpec(shape, lambda i: (0, 0))
    act = lambda: jax.ShapeDtypeStruct((m, D_GRP), act_dtype)
    stacked = jax.ShapeDtypeStruct((depth, m * N_HEADS, HEAD_W), F32)
    if prev is None:
        stacked_spec = pl.BlockSpec((depth, tm * N_HEADS, HEAD_W), lambda i: (0, i, 0))
    else:
        stacked_spec = pl.BlockSpec((None, tm * N_HEADS, HEAD_W), lambda i: (layer, i, 0))
    n_stacked = 3 if with_vn_out else 2
    out_shape = [act(), stacked, stacked, act(), act(), act()] + [stacked] * (n_stacked - 2)
    out_specs = [row(D_GRP), stacked_spec, stacked_spec, row(D_GRP), row(D_GRP), row(D_GRP)]
    out_specs += [stacked_spec] * (n_stacked - 2)
    stacked_out_idx = [1, 2] + ([6] if with_vn_out else [])
    prev = list(prev) if prev is not None else []
    aliases = {5 + n: stacked_out_idx[n] for n in range(len(prev))}
    return pl.pallas_call(
        functools.partial(_inproj_kernel, n_alias=len(prev), with_vn_out=with_vn_out),
        grid=(m // tm,),
        in_specs=[row(D_MODEL), const((1, D_MODEL)), const((D_MODEL, N_GRP * D_GRP)),
                  const((1, D_GRP)), const((1, D_GRP))]
                 + [pl.BlockSpec(memory_space=pl.ANY)] * len(prev),
        out_specs=out_specs,
        out_shape=out_shape,
        input_output_aliases=aliases,
        compiler_params=pltpu.CompilerParams(dimension_semantics=("parallel",),
                                             vmem_limit_bytes=VMEM_LIMIT),
        name="inproj",
    )(x, g, w, lng, lnb, *prev)


def _prompt_attn_kernel(q_ref, k_ref, v_ref, sga_ref, sg_ref, lq1, lk1, lq2, lk2,
                        o_ref, kb_sc, vt_sc, *stat_sc, lam_init, tb, nblk):
    m_sc = stat_sc[:N_HEADS]
    acc_sc = stat_sc[N_HEADS:]
    i = pl.program_id(1)

    @pl.when(i == 0)
    def _():
        pos = lax.broadcasted_iota(jnp.int32, (tb, POS_LANES), 0)
        lane = lax.broadcasted_iota(jnp.int32, (tb, POS_LANES), 1)
        for jb in range(nblk):
            kpos = pos + jb * tb
            hi = (kpos // PAGE) * PAGE
            aug = jnp.where(lane >= 2 * SLOPE_PARTS, 0, jnp.where(lane % 2 == 0, hi, kpos - hi))
            aug = aug.astype(F32).astype(BF16)
            for h in range(N_HEADS):
                rows = _head_rows(h, tb, jb * tb)
                kb_sc[h, jb * tb:(jb + 1) * tb, 0:HEAD_W] = k_ref[rows, :].astype(BF16)
                kb_sc[h, jb * tb:(jb + 1) * tb, HEAD_W:HEAD_W + POS_LANES] = aug
                vt_sc[h, jb, 0:HEAD_W, :] = v_ref[rows, :].T.astype(BF16)
                vt_sc[h, jb, HEAD_W:HEAD_W + ONES_ROWS, :] = jnp.ones((ONES_ROWS, tb), BF16)

    lane = lax.broadcasted_iota(jnp.int32, (tb, HEAD_W), 1)
    qz = []
    for h in range(N_HEADS):
        q = q_ref[:, h * HEAD_W:(h + 1) * HEAD_W]
        zero = jnp.zeros_like(q)
        slope = jnp.zeros((tb, POS_LANES), F32)
        for n, part in enumerate(_slope_parts(h)):
            slope = jnp.where(lane // 2 == n, part, slope)
        slope = slope.astype(BF16)
        q2 = jnp.concatenate([jnp.where(lane < DK, q, zero), jnp.where(lane >= DK, q, zero)], axis=0)
        qz.append(jnp.concatenate([q2, jnp.concatenate([slope, slope], axis=0)], axis=1))
        m_sc[h][...] = jnp.full(m_sc[h].shape, NEG, F32)
        acc_sc[h][...] = jnp.zeros(acc_sc[h].shape, F32)

    def steps(first_blk, nkb, masked):
        heads = range(N_HEADS)
        rows = pl.ds(pl.multiple_of(first_blk * tb, tb), nkb * tb)
        s = [lax.dot_general(kb_sc[h, rows, :], qz[h], (((1,), (1,)), ((), ())),
                             preferred_element_type=F32) for h in heads]
        if masked:
            key_i = lax.broadcasted_iota(jnp.int32, (nkb * tb, 2 * tb), 0)
            qry_i = lax.broadcasted_iota(jnp.int32, (nkb * tb, 2 * tb), 1)
            qry_i = jnp.where(qry_i >= tb, qry_i - tb, qry_i) + (nkb - 1) * tb
            s = [jnp.where(key_i <= qry_i, sh, NEG) for sh in s]
        m_old = [m_sc[h][...] for h in heads]
        m_new = [jnp.maximum(m_old[h], jnp.max(s[h], axis=0, keepdims=True)) for h in heads]
        p = [jnp.exp2(s[h] - m_new[h]).astype(BF16) for h in heads]
        for h in heads:
            pv = jnp.dot(vt_sc[h, first_blk], p[h][0:tb, :], preferred_element_type=F32)
            for kb in range(1, nkb):
                pv = pv + jnp.dot(vt_sc[h, first_blk + kb], p[h][kb * tb:(kb + 1) * tb, :],
                                  preferred_element_type=F32)
            acc_sc[h][...] = acc_sc[h][...] * jnp.exp2(m_old[h] - m_new[h]) + pv
            m_sc[h][...] = m_new[h]

    def body(jj, carry):
        steps(2 * jj, 2, False)
        return carry

    lax.fori_loop(0, i // 2, body, 0)

    @pl.when(i % 2 == 1)
    def _():
        steps(i - 1, 2, True)

    @pl.when(i % 2 == 0)
    def _():
        steps(i, 1, True)

    lam = _lam_value(lq1, lk1, lq2, lk2, lam_init)
    for h in range(N_HEADS):
        sl = slice(h * HEAD_W, (h + 1) * HEAD_W)
        acc = acc_sc[h]
        inv = 1.0 / acc[HEAD_W:HEAD_W + 1, :]
        o1 = acc[0:HEAD_W, 0:tb] * inv[:, 0:tb]
        o2 = acc[0:HEAD_W, tb:2 * tb] * inv[:, tb:2 * tb]
        o = (o1 - lam * o2).T
        ms = jnp.mean(o * o, axis=-1, keepdims=True)
        att = o * lax.rsqrt(ms + EPS) * sg_ref[...] * (1.0 - lam_init)
        o_ref[:, sl] = (att * sga_ref[:, sl].astype(F32)).astype(o_ref.dtype)


def _prompt_attn(q, kbuf, vbuf, layer, sga, sg, lams, lam_init, batch, seq):
    tb = ATT_BLK
    nblk = seq // tb
    qspec = pl.BlockSpec((tb, D_GRP), lambda b, i: (b * nblk + i, 0))
    kvspec = pl.BlockSpec((None, seq * N_HEADS, HEAD_W), lambda b, i: (layer, b, 0))
    const = lambda shape: pl.BlockSpec(shape, lambda b, i: (0, 0))
    stats = ([pltpu.VMEM((1, 2 * tb), F32)] * N_HEADS
             + [pltpu.VMEM((HEAD_W + ONES_ROWS, 2 * tb), F32)] * N_HEADS)
    return pl.pallas_call(
        functools.partial(_prompt_attn_kernel, lam_init=lam_init, tb=tb, nblk=nblk),
        grid=(batch, nblk),
        in_specs=[qspec, kvspec, kvspec, qspec, const((1, HEAD_W))] + [const((1, DK))] * 4,
        out_specs=qspec,
        out_shape=jax.ShapeDtypeStruct(q.shape, BF16),
        scratch_shapes=[pltpu.VMEM((N_HEADS, seq, HEAD_W + POS_LANES), BF16),
                        pltpu.VMEM((N_HEADS, nblk, HEAD_W + ONES_ROWS, tb), BF16)] + stats,
        compiler_params=pltpu.CompilerParams(
            dimension_semantics=("parallel", "arbitrary"),
            vmem_limit_bytes=VMEM_LIMIT),
        name="prompt_attn",
    )(q, kbuf, vbuf, sga, sg, *lams)


def _sample_bias(n_pages, dec_seq):
    past = n_pages * PAGE
    n_pos = past + NEW_ROWS // N_HEADS
    t = np.arange(dec_seq)[:, None]
    kpos = np.arange(n_pos)[None, :]
    visible = (kpos < past) | (kpos - past <= t)
    bias = np.full((2, N_HEADS, dec_seq, n_pos, N_HEADS), NEG, np.float32)
    for h in range(N_HEADS):
        alibi = -np.float32(_alibi_slope(h)) * (past + t - kpos).astype(np.float32) * LOG2E
        bias[:, h, :, :, h] = np.where(visible, alibi, NEG)
    return bias.reshape(2 * N_HEADS * dec_seq, n_pos * N_HEADS)


def _sample_attn_kernel(pt_ref, q_ref, kn_ref, vn_ref, sga_ref, bias_ref, sg_ref, lq1, lk1, lq2, lk2,
                        ck_hbm, cv_hbm, o_ref, kbuf, vbuf, ksem, vsem, kall_sc, vall_sc,
                        *, layer, lam_init, n_pages, dec_seq):
    t = pl.program_id(0)
    n_rows = 2 * N_HEADS * dec_seq
    half = N_HEADS * dec_seq
    page_rows = PAGE * N_HEADS
    past_rows = n_pages * page_rows
    new_rows = dec_seq * N_HEADS
    slots = range(DEC_PER_STEP)

    def page_copies(seq_idx, slot):
        copies = []
        for j in range(n_pages):
            page = pt_ref[seq_idx * n_pages + j]
            copies.append(pltpu.make_async_copy(ck_hbm.at[layer, page], kbuf.at[slot, j], ksem.at[slot]))
            copies.append(pltpu.make_async_copy(cv_hbm.at[layer, page], vbuf.at[slot, j], vsem.at[slot]))
        return copies

    @pl.when(t == 0)
    def _():
        for slot in slots:
            for cp in page_copies(slot, slot):
                cp.start()

    pad = jnp.zeros((NEW_ROWS - new_rows, HEAD_W), BF16)
    for slot in slots:
        seq_idx = DEC_PER_STEP * t + slot
        for cp in page_copies(seq_idx, slot):
            cp.wait()
        for j in range(n_pages):
            rows = slice(j * page_rows, (j + 1) * page_rows)
            kall_sc[slot, rows, :] = kbuf[slot, j].astype(BF16)
            vall_sc[slot, rows, :] = vbuf[slot, j].astype(BF16)
        rows = slice(slot * new_rows, (slot + 1) * new_rows)
        kall_sc[slot, past_rows:past_rows + NEW_ROWS, :] = jnp.concatenate(
            [kn_ref[rows, :].astype(BF16), pad], axis=0)
        vall_sc[slot, past_rows:past_rows + NEW_ROWS, :] = jnp.concatenate(
            [vn_ref[rows, :].astype(BF16), pad], axis=0)

        @pl.when(t + 1 < pl.num_programs(0))
        def _():
            for cp in page_copies(seq_idx + DEC_PER_STEP, slot):
                cp.start()

    lane = lax.broadcasted_iota(jnp.int32, (half, HEAD_W), 1)
    lam = _lam_value(lq1, lk1, lq2, lk2, lam_init)
    nt = (((1,), (1,)), ((), ()))
    qrows = []
    for slot in slots:
        rows = slice(slot * dec_seq, (slot + 1) * dec_seq)
        qh = jnp.concatenate([q_ref[rows, h * HEAD_W:(h + 1) * HEAD_W] for h in range(N_HEADS)], axis=0)
        qrows.append(jnp.concatenate([jnp.where(lane < DK, qh, 0.0), jnp.where(lane >= DK, qh, 0.0)],
                                     axis=0).astype(BF16))
    s = [lax.dot_general(qrows[slot], kall_sc[slot], nt, preferred_element_type=F32) + bias_ref[...]
         for slot in slots]
    p = [jnp.exp2(sv - jnp.max(sv, axis=-1, keepdims=True)) for sv in s]
    inv = [1.0 / jnp.sum(pv, axis=-1, keepdims=True) for pv in p]
    pd = [(pv[0:half, :] * iv[0:half, :] - pv[half:n_rows, :] * (lam * iv[half:n_rows, :])).astype(BF16)
          for pv, iv in zip(p, inv)]
    res = [jnp.dot(pd[slot], vall_sc[slot], preferred_element_type=F32) for slot in slots]
    for slot in slots:
        rows = slice(slot * dec_seq, (slot + 1) * dec_seq)
        for h in range(N_HEADS):
            sl = slice(h * HEAD_W, (h + 1) * HEAD_W)
            o = res[slot][h * dec_seq:(h + 1) * dec_seq, :]
            ms = jnp.mean(o * o, axis=-1, keepdims=True)
            att = o * lax.rsqrt(ms + EPS) * sg_ref[...] * (1.0 - lam_init)
            o_ref[rows, sl] = (att * sga_ref[rows, sl]).astype(o_ref.dtype)


def _sample_attn(pt, q, knbuf, vnbuf, sga, cache_k, cache_v, layer, sg, lams, lam_init,
                 dec_batch, dec_seq, n_pages):
    bias = jnp.asarray(_sample_bias(n_pages, dec_seq))
    n = DEC_PER_STEP
    rowspec = pl.BlockSpec((n * dec_seq, D_GRP), lambda t, pt: (t, 0))
    newspec = pl.BlockSpec((None, n * dec_seq * N_HEADS, HEAD_W), lambda t, pt: (layer, t, 0))
    const = lambda shape: pl.BlockSpec(shape, lambda t, pt: (0, 0))
    hbm = pl.BlockSpec(memory_space=pl.ANY)
    page_buf = pltpu.VMEM((n, n_pages, PAGE * N_HEADS, HEAD_W), F32)
    operand = pltpu.VMEM((n, n_pages * PAGE * N_HEADS + NEW_ROWS, HEAD_W), BF16)
    grid_spec = pltpu.PrefetchScalarGridSpec(
        num_scalar_prefetch=1,
        grid=(dec_batch // n,),
        in_specs=[rowspec, newspec, newspec, rowspec, const(bias.shape), const((1, HEAD_W))]
                 + [const((1, DK))] * 4 + [hbm, hbm],
        out_specs=rowspec,
        scratch_shapes=[page_buf, page_buf, pltpu.SemaphoreType.DMA((n,)), pltpu.SemaphoreType.DMA((n,)),
                        operand, operand],
    )
    return pl.pallas_call(
        functools.partial(_sample_attn_kernel, layer=layer, lam_init=lam_init, n_pages=n_pages,
                          dec_seq=dec_seq),
        grid_spec=grid_spec,
        out_shape=jax.ShapeDtypeStruct(q.shape, F32),
        compiler_params=pltpu.CompilerParams(dimension_semantics=("arbitrary",),
                                             vmem_limit_bytes=VMEM_LIMIT),
        name="sample_attn",
    )(pt, q, knbuf, vnbuf, sga, bias, sg, *lams, cache_k, cache_v)


def _mix_out_kernel(x_ref, att_ref, ugc_ref, vn_ref, ws_ref, bsb_ref, wout_ref, fg_ref,
                    o_ref, cat_sc, *, final):
    tm = x_ref.shape[0]
    nc = tm // CHUNK
    r = lax.broadcasted_iota(jnp.int32, (CHUNK, CHUNK), 0)
    c = lax.broadcasted_iota(jnp.int32, (CHUNK, CHUNK), 1)
    cat_sc[:, 0:D_GRP] = att_ref[...].astype(BF16)
    for h in range(N_HEADS):
        sl = slice(h * HEAD_W, (h + 1) * HEAD_W)
        ws = jnp.where(c <= r, ws_ref[h], 0.0).astype(BF16)
        vn = jnp.concatenate([vn_ref[ci * CHUNK:(ci + 1) * CHUNK, sl].astype(BF16)
                              for ci in range(nc)], axis=1)
        mixed = jnp.dot(ws, vn, preferred_element_type=F32)
        for ci in range(nc):
            rows = slice(ci * CHUNK, (ci + 1) * CHUNK)
            mc = mixed[:, ci * HEAD_W:(ci + 1) * HEAD_W] + bsb_ref[:, sl]
            cat_sc[rows, D_GRP + h * HEAD_W:D_GRP + (h + 1) * HEAD_W] = (
                ugc_ref[rows, sl].astype(F32) * mc).astype(BF16)
    y = x_ref[...] + jnp.dot(cat_sc[...], wout_ref[...], preferred_element_type=F32)
    if final:
        ms = jnp.mean(y * y, axis=-1, keepdims=True)
        y = y * lax.rsqrt(ms + EPS) * fg_ref[...]
    o_ref[...] = y


def _mix_out(x, att, ugc, vn, ws, bsb, wout, fg, final):
    m = x.shape[0]
    tm = ROW_TILE
    row = lambda width: pl.BlockSpec((tm, width), lambda i: (i, 0))
    return pl.pallas_call(
        functools.partial(_mix_out_kernel, final=final),
        grid=(m // tm,),
        in_specs=[row(D_MODEL), row(D_GRP), row(D_GRP), row(D_GRP),
                  pl.BlockSpec((N_HEADS, CHUNK, CHUNK), lambda i: (0, 0, 0)),
                  pl.BlockSpec((CHUNK, D_GRP), lambda i: (0, 0)),
                  pl.BlockSpec((2 * D_GRP, D_MODEL), lambda i: (0, 0)),
                  pl.BlockSpec((1, D_MODEL), lambda i: (0, 0))],
        out_specs=row(D_MODEL),
        out_shape=jax.ShapeDtypeStruct((m, D_MODEL), F32),
        scratch_shapes=[pltpu.VMEM((tm, 2 * D_GRP), BF16)],
        compiler_params=pltpu.CompilerParams(dimension_semantics=("parallel",),
                                             vmem_limit_bytes=VMEM_LIMIT),
        name="mix_out",
    )(x, att, ugc, vn, ws, bsb, wout, fg)


def kernel(x_prompt, x_sample, cache_k, cache_v, page_table, norm_g, w_in, lam_q1, lam_k1,
           lam_q2, lam_k2, subln_g, ln_v_g, ln_v_b, w_s, b_s, w_out, final_g):
    batch, seq, _ = x_prompt.shape
    dec_batch, dec_seq, _ = x_sample.shape
    depth, n_pool = cache_k.shape[0], cache_k.shape[1]
    n_pages = page_table.shape[1]
    assert seq % ATT_BLK == 0 and (batch * seq) % ROW_TILE == 0 and (dec_batch * dec_seq) % ROW_TILE == 0
    assert CHUNK % dec_seq == 0 and dec_seq % 8 == 0 and ATT_BLK % PAGE == 0
    assert dec_seq * N_HEADS <= NEW_ROWS and dec_batch % DEC_PER_STEP == 0

    hp = x_prompt.reshape(batch * seq, D_MODEL)
    hs = x_sample.reshape(dec_batch * dec_seq, D_MODEL)
    ck = cache_k.reshape(depth, n_pool, PAGE * N_HEADS, HEAD_W)
    cv = cache_v.reshape(depth, n_pool, PAGE * N_HEADS, HEAD_W)
    pt = page_table.reshape(-1)
    w_in_b = w_in.astype(BF16)
    w_out_b = w_out.astype(BF16)
    fg = final_g.reshape(1, D_MODEL)
    eye = jnp.eye(CHUNK // dec_seq, dtype=F32)

    prompt_bufs, sample_bufs = None, None
    for l in range(depth):
        lam_init = _lambda_init(l)
        g = norm_g[l].reshape(1, D_MODEL)
        lng = ln_v_g[l].reshape(1, D_GRP)
        lnb = ln_v_b[l].reshape(1, D_GRP)
        sg = subln_g[l].reshape(1, HEAD_W)
        lams = [a[l].reshape(1, DK) for a in (lam_q1, lam_k1, lam_q2, lam_k2)]
        final = l == depth - 1

        q, kbuf, vbuf, sga, ugc, vn = _inproj(hp, g, w_in_b[l], lng, lnb, BF16, l, depth,
                                              prompt_bufs, False)
        prompt_bufs = (kbuf, vbuf)
        att = _prompt_attn(q, kbuf, vbuf, l, sga, sg, lams, lam_init, batch, seq)
        bsb = jnp.repeat(b_s[l].T, HEAD_W, axis=1)
        hp = _mix_out(hp, att, ugc, vn, w_s[l], bsb, w_out_b[l], fg, final)

        q, kbuf, vbuf, sga, ugc, vn, vnbuf = _inproj(hs, g, w_in_b[l], lng, lnb, F32, l, depth,
                                                     sample_bufs, True)
        sample_bufs = (kbuf, vbuf, vnbuf)
        att = _sample_attn(pt, q, kbuf, vbuf, sga, ck, cv, l, sg, lams, lam_init,
                           dec_batch, dec_seq, n_pages)
        ws8 = w_s[l][:, :dec_seq, :dec_seq]
        ws_bd = (eye[None, :, None, :, None] * ws8[:, None, :, None, :]).reshape(N_HEADS, CHUNK, CHUNK)
        bsb = jnp.tile(jnp.repeat(b_s[l][:, :dec_seq].T, HEAD_W, axis=1), (CHUNK // dec_seq, 1))
        hs = _mix_out(hs, att, ugc, vn, ws_bd, bsb, w_out_b[l], fg, final)

    shape_p = (depth, batch, seq, N_HEADS, HEAD_W)
    shape_s = (depth, dec_batch, dec_seq, N_HEADS, HEAD_W)
    return (hp.reshape(batch, seq, D_MODEL),
            hs.reshape(dec_batch, dec_seq, D_MODEL),
            prompt_bufs[0].reshape(shape_p),
            prompt_bufs[1].reshape(shape_p),
            sample_bufs[0].reshape(shape_s),
            sample_bufs[1].reshape(shape_s),
            sample_bufs[2].reshape(shape_s))
```

```python
import functools
import math

import numpy as np
import jax
import jax.numpy as jnp
from jax import lax
from jax.experimental import pallas as pl
from jax.experimental.pallas import tpu as pltpu

F32 = jnp.float32
BF16 = jnp.bfloat16

D_MODEL = 1024
N_HEADS = 4
DK = 64
HEAD_W = 128
D_GRP = N_HEADS * HEAD_W
N_GRP = 7
CHUNK = 128
PAGE = 128
EPS = 1e-6
NEG = -1e30
ONES_ROWS = 16
POS_LANES = 128
SLOPE_PARTS = 3
LOG2E = np.float32(math.log2(math.e))

ROW_TILE = 512
ATT_BLK = 256
DEC_PER_STEP = 2
NEW_ROWS = 128
VMEM_LIMIT = 56 * 1024 * 1024


def _lambda_init(layer):
    return 0.8 - 0.6 * math.exp(-0.3 * layer)


def _alibi_slope(h):
    return 2.0 ** (-8.0 * (h + 1) / N_HEADS)


def _slope_parts(h):
    target = np.float32(_alibi_slope(h)) * LOG2E
    rest, parts = target, []
    for _ in range(SLOPE_PARTS):
        part = np.asarray(rest, dtype=BF16).astype(np.float32)
        parts.append(float(part))
        rest = np.float32(rest - part)
    assert rest == 0.0
    return parts


def _lam_value(lq1, lk1, lq2, lk2, lam_init):
    a = jnp.sum(lq1[...] * lk1[...], axis=-1, keepdims=True)
    b = jnp.sum(lq2[...] * lk2[...], axis=-1, keepdims=True)
    return jnp.exp(a) - jnp.exp(b) + lam_init


def _silu(x):
    return x * jax.nn.sigmoid(x)


def _head_rows(h, n_rows, first_row=0):
    return pl.ds(first_row * N_HEADS + h, n_rows, stride=N_HEADS)


def _inproj_stages(x_ref, g_ref, w_ref, lng_ref, lnb_ref, outs, with_vn_out):
    q_ref, k_ref, v_ref, sga_ref, ugc_ref, vn_ref = outs[:6]
    tm = x_ref.shape[0]

    x = x_ref[...]
    ms = jnp.mean(x * x, axis=-1, keepdims=True)
    h = (x * lax.rsqrt(ms + EPS) * g_ref[...]).astype(BF16)

    def proj(i):
        return jnp.dot(h, w_ref[:, i * D_GRP:(i + 1) * D_GRP], preferred_element_type=F32)

    def store_heads(ref, c, val):
        if len(ref.shape) == 3:
            for d in range(ref.shape[0]):
                ref[d, _head_rows(c, tm), :] = val
        else:
            ref[_head_rows(c, tm), :] = val

    def store_all_heads(ref, val):
        for c in range(N_HEADS):
            store_heads(ref, c, val[:, c * HEAD_W:(c + 1) * HEAD_W])

    def q_stage():
        q_ref[...] = (proj(0) * float(np.float32(DK ** -0.5) * LOG2E)).astype(q_ref.dtype)

    def k_stage():
        store_all_heads(k_ref, proj(1))

    def v_stage():
        store_all_heads(v_ref, proj(2))

    def gate_stage():
        sga_ref[...] = _silu(proj(3)).astype(sga_ref.dtype)

    def ugc_stage():
        ugc_ref[...] = (proj(4) * _silu(proj(6))).astype(ugc_ref.dtype)

    def vn_stage():
        vc = proj(5)
        for c in range(N_HEADS):
            sl = slice(c * HEAD_W, (c + 1) * HEAD_W)
            blk = vc[:, sl]
            mu = jnp.mean(blk, axis=-1, keepdims=True)
            xc = blk - mu
            var = jnp.mean(xc * xc, axis=-1, keepdims=True)
            vn = xc * lax.rsqrt(var + EPS) * lng_ref[:, sl] + lnb_ref[:, sl]
            vn_ref[:, sl] = vn.astype(vn_ref.dtype)
            if with_vn_out:
                store_heads(outs[6], c, vn)

    return [q_stage, k_stage, v_stage, gate_stage, ugc_stage, vn_stage]


def _inproj_kernel(*refs, n_alias, with_vn_out):
    outs = refs[5 + n_alias:]
    for stage in _inproj_stages(*refs[:5], outs, with_vn_out):
        stage()


def _inproj_out_specs(m, tm, act_dtype, layer, depth, first, with_vn_out, index):
    row = pl.BlockSpec((tm, D_GRP), lambda *i: (index(*i), 0))
    act = jax.ShapeDtypeStruct((m, D_GRP), act_dtype)
    stacked = jax.ShapeDtypeStruct((depth, m * N_HEADS, HEAD_W), F32)
    if first:
        stacked_spec = pl.BlockSpec((depth, tm * N_HEADS, HEAD_W), lambda *i: (0, index(*i), 0))
    else:
        stacked_spec = pl.BlockSpec((None, tm * N_HEADS, HEAD_W), lambda *i: (layer, index(*i), 0))
    extra = 1 if with_vn_out else 0
    out_shape = [act, stacked, stacked, act, act, act] + [stacked] * extra
    out_specs = [row, stacked_spec, stacked_spec, row, row, row] + [stacked_spec] * extra
    return out_shape, out_specs, [1, 2] + [6] * extra


def _inproj(x, g, w, lng, lnb, act_dtype, layer, depth, prev, with_vn_out):
    m = x.shape[0]
    tm = ROW_TILE
    const = lambda shape: pl.BlockSpec(shape, lambda i: (0, 0))
    out_shape, out_specs, stacked_idx = _inproj_out_specs(
        m, tm, act_dtype, layer, depth, prev is None, with_vn_out, lambda i: i)
    prev = list(prev) if prev is not None else []
    aliases = {5 + n: stacked_idx[n] for n in range(len(prev))}
    return pl.pallas_call(
        functools.partial(_inproj_kernel, n_alias=len(prev), with_vn_out=with_vn_out),
        grid=(m // tm,),
        in_specs=[pl.BlockSpec((tm, D_MODEL), lambda i: (i, 0)), const((1, D_MODEL)),
                  const((D_MODEL, N_GRP * D_GRP)), const((1, D_GRP)), const((1, D_GRP))]
                 + [pl.BlockSpec(memory_space=pl.ANY)] * len(prev),
        out_specs=out_specs,
        out_shape=out_shape,
        input_output_aliases=aliases,
        compiler_params=pltpu.CompilerParams(dimension_semantics=("parallel",),
                                             vmem_limit_bytes=VMEM_LIMIT),
        name="inproj",
    )(x, g, w, lng, lnb, *prev)


def _prompt_attn_kernel(q_ref, k_ref, v_ref, sga_ref, sg_ref, lq1, lk1, lq2, lk2,
                        o_ref, kb_sc, vt_sc, *stat_sc, lam_init, tb, nblk):
    m_sc = stat_sc[:N_HEADS]
    acc_sc = stat_sc[N_HEADS:]
    i = pl.program_id(1)

    @pl.when(i == 0)
    def _():
        pos = lax.broadcasted_iota(jnp.int32, (tb, POS_LANES), 0)
        lane = lax.broadcasted_iota(jnp.int32, (tb, POS_LANES), 1)
        for jb in range(nblk):
            kpos = pos + jb * tb
            hi = (kpos // PAGE) * PAGE
            aug = jnp.where(lane >= 2 * SLOPE_PARTS, 0, jnp.where(lane % 2 == 0, hi, kpos - hi))
            aug = aug.astype(F32).astype(BF16)
            for h in range(N_HEADS):
                rows = _head_rows(h, tb, jb * tb)
                kb_sc[h, jb * tb:(jb + 1) * tb, 0:HEAD_W] = k_ref[rows, :].astype(BF16)
                kb_sc[h, jb * tb:(jb + 1) * tb, HEAD_W:HEAD_W + POS_LANES] = aug
                vt_sc[h, jb, 0:HEAD_W, :] = v_ref[rows, :].T.astype(BF16)
                vt_sc[h, jb, HEAD_W:HEAD_W + ONES_ROWS, :] = jnp.ones((ONES_ROWS, tb), BF16)

    lane = lax.broadcasted_iota(jnp.int32, (tb, HEAD_W), 1)
    qz = []
    for h in range(N_HEADS):
        q = q_ref[:, h * HEAD_W:(h + 1) * HEAD_W]
        zero = jnp.zeros_like(q)
        slope = jnp.zeros((tb, POS_LANES), F32)
        for n, part in enumerate(_slope_parts(h)):
            slope = jnp.where(lane // 2 == n, part, slope)
        slope = slope.astype(BF16)
        q2 = jnp.concatenate([jnp.where(lane < DK, q, zero), jnp.where(lane >= DK, q, zero)], axis=0)
        qz.append(jnp.concatenate([q2, jnp.concatenate([slope, slope], axis=0)], axis=1))
        m_sc[h][...] = jnp.full(m_sc[h].shape, NEG, F32)
        acc_sc[h][...] = jnp.zeros(acc_sc[h].shape, F32)

    def steps(first_blk, nkb, masked):
        heads = range(N_HEADS)
        rows = pl.ds(pl.multiple_of(first_blk * tb, tb), nkb * tb)
        s = [lax.dot_general(kb_sc[h, rows, :], qz[h], (((1,), (1,)), ((), ())),
                             preferred_element_type=F32) for h in heads]
        if masked:
            key_i = lax.broadcasted_iota(jnp.int32, (nkb * tb, 2 * tb), 0)
            qry_i = lax.broadcasted_iota(jnp.int32, (nkb * tb, 2 * tb), 1)
            qry_i = jnp.where(qry_i >= tb, qry_i - tb, qry_i) + (nkb - 1) * tb
            s = [jnp.where(key_i <= qry_i, sh, NEG) for sh in s]
        m_old = [m_sc[h][...] for h in heads]
        m_new = [jnp.maximum(m_old[h], jnp.max(s[h], axis=0, keepdims=True)) for h in heads]
        p = [jnp.exp2(s[h] - m_new[h]).astype(BF16) for h in heads]
        for h in heads:
            pv = jnp.dot(vt_sc[h, first_blk], p[h][0:tb, :], preferred_element_type=F32)
            for kb in range(1, nkb):
                pv = pv + jnp.dot(vt_sc[h, first_blk + kb], p[h][kb * tb:(kb + 1) * tb, :],
                                  preferred_element_type=F32)
            acc_sc[h][...] = acc_sc[h][...] * jnp.exp2(m_old[h] - m_new[h]) + pv
            m_sc[h][...] = m_new[h]

    def body(jj, carry):
        steps(2 * jj, 2, False)
        return carry

    lax.fori_loop(0, i // 2, body, 0)

    @pl.when(i % 2 == 1)
    def _():
        steps(i - 1, 2, True)

    @pl.when(i % 2 == 0)
    def _():
        steps(i, 1, True)

    lam = _lam_value(lq1, lk1, lq2, lk2, lam_init)
    for h in range(N_HEADS):
        sl = slice(h * HEAD_W, (h + 1) * HEAD_W)
        acc = acc_sc[h]
        inv = 1.0 / acc[HEAD_W:HEAD_W + 1, :]
        o1 = acc[0:HEAD_W, 0:tb] * inv[:, 0:tb]
        o2 = acc[0:HEAD_W, tb:2 * tb] * inv[:, tb:2 * tb]
        o = (o1 - lam * o2).T
        ms = jnp.mean(o * o, axis=-1, keepdims=True)
        att = o * lax.rsqrt(ms + EPS) * sg_ref[...] * (1.0 - lam_init)
        o_ref[:, sl] = (att * sga_ref[:, sl].astype(F32)).astype(o_ref.dtype)


def _prompt_attn(q, kbuf, vbuf, layer, sga, sg, lams, lam_init, batch, seq):
    tb = ATT_BLK
    nblk = seq // tb
    qspec = pl.BlockSpec((tb, D_GRP), lambda b, i: (b * nblk + i, 0))
    kvspec = pl.BlockSpec((None, seq * N_HEADS, HEAD_W), lambda b, i: (layer, b, 0))
    const = lambda shape: pl.BlockSpec(shape, lambda b, i: (0, 0))
    stats = ([pltpu.VMEM((1, 2 * tb), F32)] * N_HEADS
             + [pltpu.VMEM((HEAD_W + ONES_ROWS, 2 * tb), F32)] * N_HEADS)
    return pl.pallas_call(
        functools.partial(_prompt_attn_kernel, lam_init=lam_init, tb=tb, nblk=nblk),
        grid=(batch, nblk),
        in_specs=[qspec, kvspec, kvspec, qspec, const((1, HEAD_W))] + [const((1, DK))] * 4,
        out_specs=qspec,
        out_shape=jax.ShapeDtypeStruct(q.shape, BF16),
        scratch_shapes=[pltpu.VMEM((N_HEADS, seq, HEAD_W + POS_LANES), BF16),
                        pltpu.VMEM((N_HEADS, nblk, HEAD_W + ONES_ROWS, tb), BF16)] + stats,
        compiler_params=pltpu.CompilerParams(
            dimension_semantics=("parallel", "arbitrary"),
            vmem_limit_bytes=VMEM_LIMIT),
        name="prompt_attn",
    )(q, kbuf, vbuf, sga, sg, *lams)


def _sample_bias(n_pages, dec_seq):
    past = n_pages * PAGE
    n_pos = past + NEW_ROWS // N_HEADS
    t = np.arange(dec_seq)[:, None]
    kpos = np.arange(n_pos)[None, :]
    visible = (kpos < past) | (kpos - past <= t)
    bias = np.full((2, N_HEADS, dec_seq, n_pos, N_HEADS), NEG, np.float32)
    for h in range(N_HEADS):
        alibi = -np.float32(_alibi_slope(h)) * (past + t - kpos).astype(np.float32) * LOG2E
        bias[:, h, :, :, h] = np.where(visible, alibi, NEG)
    return bias.reshape(2 * N_HEADS * dec_seq, n_pos * N_HEADS)


def _paged_attention(pt_ref, q_ref, kn_ref, vn_ref, sga_ref, bias_ref, sg_ref, lams, ck_hbm, cv_hbm,
                     o_ref, kbuf, vbuf, ksem, vsem,
                     *, layer, lam_init, n_pages, dec_seq, fillers):
    t = pl.program_id(0)
    n_rows = 2 * N_HEADS * dec_seq
    half = N_HEADS * dec_seq
    page_rows = PAGE * N_HEADS
    past_rows = n_pages * page_rows
    new_rows = dec_seq * N_HEADS
    slots = range(DEC_PER_STEP)

    def page_copies(seq_idx, slot):
        copies = []
        for j in range(n_pages):
            page = pt_ref[seq_idx * n_pages + j]
            copies.append(pltpu.make_async_copy(ck_hbm.at[layer, page], kbuf.at[slot, j], ksem.at[slot]))
            copies.append(pltpu.make_async_copy(cv_hbm.at[layer, page], vbuf.at[slot, j], vsem.at[slot]))
        return copies

    @pl.when(t == 0)
    def _():
        for slot in slots:
            for cp in page_copies(slot, slot):
                cp.start()

    lane = lax.broadcasted_iota(jnp.int32, (half, HEAD_W), 1)
    lam = _lam_value(*lams, lam_init)
    pad = jnp.zeros((NEW_ROWS - new_rows, HEAD_W), F32)
    nt = (((1,), (1,)), ((), ()))

    def attend(slot, seq_idx):
        rows = slice(slot * dec_seq, (slot + 1) * dec_seq)
        new = slice(slot * new_rows, (slot + 1) * new_rows)
        k_new = jnp.concatenate([kn_ref[new, :], pad], axis=0)
        v_new = jnp.concatenate([vn_ref[new, :], pad], axis=0)
        qh = jnp.concatenate([q_ref[rows, h * HEAD_W:(h + 1) * HEAD_W] for h in range(N_HEADS)], axis=0)
        qrows = jnp.concatenate([jnp.where(lane < DK, qh, 0.0), jnp.where(lane >= DK, qh, 0.0)], axis=0)
        s = [lax.dot_general(qrows, kbuf[slot, j], nt, preferred_element_type=F32) for j in range(n_pages)]
        s.append(lax.dot_general(qrows, k_new, nt, preferred_element_type=F32))
        s = jnp.concatenate(s, axis=1) + bias_ref[...]
        yield
        p = jnp.exp2(s - jnp.max(s, axis=-1, keepdims=True))
        inv = 1.0 / jnp.sum(p, axis=-1, keepdims=True)
        yield
        pd = p[0:half, :] * inv[0:half, :] - p[half:n_rows, :] * (lam * inv[half:n_rows, :])
        res = jnp.dot(pd[:, past_rows:], v_new, preferred_element_type=F32)
        for j in range(n_pages):
            res = res + jnp.dot(pd[:, j * page_rows:(j + 1) * page_rows], vbuf[slot, j],
                                preferred_element_type=F32)

        @pl.when(t + 1 < pl.num_programs(0))
        def _():
            for cp in page_copies(seq_idx + DEC_PER_STEP, slot):
                cp.start()

        yield
        for h in range(N_HEADS):
            sl = slice(h * HEAD_W, (h + 1) * HEAD_W)
            o = res[h * dec_seq:(h + 1) * dec_seq, :]
            ms = jnp.mean(o * o, axis=-1, keepdims=True)
            att = o * lax.rsqrt(ms + EPS) * sg_ref[...] * (1.0 - lam_init)
            o_ref[rows, sl] = (att * sga_ref[rows, sl]).astype(o_ref.dtype)

    for slot in slots:
        seq_idx = DEC_PER_STEP * t + slot
        for cp in page_copies(seq_idx, slot):
            cp.wait()
        pending = list(fillers[slot])
        for _ in attend(slot, seq_idx):
            if pending:
                pending.pop(0)()
        for filler in pending:
            filler()


def _inproj_decode_kernel(pt_ref, x_ref, g_ref, w_ref, lng_ref, lnb_ref,
                          qs_ref, kn_ref, vn_ref, sgas_ref, bias_ref, sg_ref, lq1, lk1, lq2, lk2,
                          ck_hbm, cv_hbm, *rest, n_alias, layer, lam_init, n_pages, dec_seq):
    rest = rest[n_alias:]
    outs, att_ref, scratch = rest[:6], rest[6], rest[7:]
    stages = _inproj_stages(x_ref, g_ref, w_ref, lng_ref, lnb_ref, outs, False)
    fillers = [stages[slot::DEC_PER_STEP] for slot in range(DEC_PER_STEP)]
    _paged_attention(pt_ref, qs_ref, kn_ref, vn_ref, sgas_ref, bias_ref, sg_ref, (lq1, lk1, lq2, lk2),
                     ck_hbm, cv_hbm, att_ref, *scratch, layer=layer, lam_init=lam_init,
                     n_pages=n_pages, dec_seq=dec_seq, fillers=fillers)


def _inproj_decode(x, g, w, lng, lnb, layer, depth, prev,
                   pt, qs, knbuf, vnbuf, sgas, cache_k, cache_v, sg, lams, lam_init,
                   dec_batch, dec_seq, n_pages):
    m = x.shape[0]
    n = DEC_PER_STEP
    n_steps = dec_batch // n
    tm = m // n_steps
    bias = jnp.asarray(_sample_bias(n_pages, dec_seq))
    const = lambda shape, **kw: pl.BlockSpec(shape, lambda t, pt: (0, 0), **kw)
    once = dict(pipeline_mode=pl.Buffered(1))
    seqspec = pl.BlockSpec((n * dec_seq, D_GRP), lambda t, pt: (t, 0))
    newspec = pl.BlockSpec((None, n * dec_seq * N_HEADS, HEAD_W), lambda t, pt: (layer, t, 0))
    hbm = pl.BlockSpec(memory_space=pl.ANY)
    out_shape, out_specs, stacked_idx = _inproj_out_specs(
        m, tm, BF16, layer, depth, prev is None, False, lambda t, pt: t)
    prev = list(prev) if prev is not None else []
    n_inputs = 18
    aliases = {n_inputs + i: stacked_idx[i] for i in range(len(prev))}
    page_buf = pltpu.VMEM((n, n_pages, PAGE * N_HEADS, HEAD_W), F32)
    grid_spec = pltpu.PrefetchScalarGridSpec(
        num_scalar_prefetch=1,
        grid=(n_steps,),
        in_specs=[pl.BlockSpec((tm, D_MODEL), lambda t, pt: (t, 0)), const((1, D_MODEL)),
                  const((D_MODEL, N_GRP * D_GRP), **once), const((1, D_GRP)), const((1, D_GRP)),
                  seqspec, newspec, newspec, seqspec, const(bias.shape, **once), const((1, HEAD_W))]
                 + [const((1, DK))] * 4 + [hbm, hbm] + [hbm] * len(prev),
        out_specs=out_specs + [seqspec],
        scratch_shapes=[page_buf, page_buf, pltpu.SemaphoreType.DMA((n,)), pltpu.SemaphoreType.DMA((n,))],
    )
    return pl.pallas_call(
        functools.partial(_inproj_decode_kernel, n_alias=len(prev), layer=layer, lam_init=lam_init,
                          n_pages=n_pages, dec_seq=dec_seq),
        grid_spec=grid_spec,
        out_shape=out_shape + [jax.ShapeDtypeStruct(qs.shape, F32)],
        input_output_aliases=aliases,
        compiler_params=pltpu.CompilerParams(dimension_semantics=("arbitrary",),
                                             vmem_limit_bytes=VMEM_LIMIT),
        name="inproj_decode",
    )(pt, x, g, w, lng, lnb, qs, knbuf, vnbuf, sgas, bias, sg, *lams, cache_k, cache_v, *prev)


def _mix_out_kernel(x_ref, att_ref, ugc_ref, vn_ref, ws_ref, bsb_ref, wout_ref, fg_ref,
                    o_ref, cat_sc, *, final):
    tm = x_ref.shape[0]
    nc = tm // CHUNK
    r = lax.broadcasted_iota(jnp.int32, (CHUNK, CHUNK), 0)
    c = lax.broadcasted_iota(jnp.int32, (CHUNK, CHUNK), 1)
    cat_sc[:, 0:D_GRP] = att_ref[...].astype(BF16)
    for h in range(N_HEADS):
        sl = slice(h * HEAD_W, (h + 1) * HEAD_W)
        ws = jnp.where(c <= r, ws_ref[h], 0.0).astype(BF16)
        vn = jnp.concatenate([vn_ref[ci * CHUNK:(ci + 1) * CHUNK, sl].astype(BF16)
                              for ci in range(nc)], axis=1)
        mixed = jnp.dot(ws, vn, preferred_element_type=F32)
        for ci in range(nc):
            rows = slice(ci * CHUNK, (ci + 1) * CHUNK)
            mc = mixed[:, ci * HEAD_W:(ci + 1) * HEAD_W] + bsb_ref[:, sl]
            cat_sc[rows, D_GRP + h * HEAD_W:D_GRP + (h + 1) * HEAD_W] = (
                ugc_ref[rows, sl].astype(F32) * mc).astype(BF16)
    y = x_ref[...] + jnp.dot(cat_sc[...], wout_ref[...], preferred_element_type=F32)
    if final:
        ms = jnp.mean(y * y, axis=-1, keepdims=True)
        y = y * lax.rsqrt(ms + EPS) * fg_ref[...]
    o_ref[...] = y


def _mix_out(x, att, ugc, vn, ws, bsb, wout, fg, final):
    m = x.shape[0]
    tm = ROW_TILE
    row = lambda width: pl.BlockSpec((tm, width), lambda i: (i, 0))
    return pl.pallas_call(
        functools.partial(_mix_out_kernel, final=final),
        grid=(m // tm,),
        in_specs=[row(D_MODEL), row(D_GRP), row(D_GRP), row(D_GRP),
                  pl.BlockSpec((N_HEADS, CHUNK, CHUNK), lambda i: (0, 0, 0)),
                  pl.BlockSpec((CHUNK, D_GRP), lambda i: (0, 0)),
                  pl.BlockSpec((2 * D_GRP, D_MODEL), lambda i: (0, 0)),
                  pl.BlockSpec((1, D_MODEL), lambda i: (0, 0))],
        out_specs=row(D_MODEL),
        out_shape=jax.ShapeDtypeStruct((m, D_MODEL), F32),
        scratch_shapes=[pltpu.VMEM((tm, 2 * D_GRP), BF16)],
        compiler_params=pltpu.CompilerParams(dimension_semantics=("parallel",),
                                             vmem_limit_bytes=VMEM_LIMIT),
        name="mix_out",
    )(x, att, ugc, vn, ws, bsb, wout, fg)


def kernel(x_prompt, x_sample, cache_k, cache_v, page_table, norm_g, w_in, lam_q1, lam_k1,
           lam_q2, lam_k2, subln_g, ln_v_g, ln_v_b, w_s, b_s, w_out, final_g):
    batch, seq, _ = x_prompt.shape
    dec_batch, dec_seq, _ = x_sample.shape
    depth, n_pool = cache_k.shape[0], cache_k.shape[1]
    n_pages = page_table.shape[1]
    assert seq % ATT_BLK == 0 and (batch * seq) % ROW_TILE == 0 and (dec_batch * dec_seq) % ROW_TILE == 0
    assert CHUNK % dec_seq == 0 and dec_seq % 8 == 0 and ATT_BLK % PAGE == 0
    assert dec_seq * N_HEADS <= NEW_ROWS and dec_batch % DEC_PER_STEP == 0
    assert (batch * seq) % (dec_batch // DEC_PER_STEP) == 0

    hp = x_prompt.reshape(batch * seq, D_MODEL)
    hs = x_sample.reshape(dec_batch * dec_seq, D_MODEL)
    ck = cache_k.reshape(depth, n_pool, PAGE * N_HEADS, HEAD_W)
    cv = cache_v.reshape(depth, n_pool, PAGE * N_HEADS, HEAD_W)
    pt = page_table.reshape(-1)
    w_in_b = w_in.astype(BF16)
    w_out_b = w_out.astype(BF16)
    fg = final_g.reshape(1, D_MODEL)
    eye = jnp.eye(CHUNK // dec_seq, dtype=F32)

    prompt_bufs, sample_bufs = None, None
    for l in range(depth):
        lam_init = _lambda_init(l)
        g = norm_g[l].reshape(1, D_MODEL)
        lng = ln_v_g[l].reshape(1, D_GRP)
        lnb = ln_v_b[l].reshape(1, D_GRP)
        sg = subln_g[l].reshape(1, HEAD_W)
        lams = [a[l].reshape(1, DK) for a in (lam_q1, lam_k1, lam_q2, lam_k2)]
        final = l == depth - 1

        qs, ksbuf, vsbuf, sgas, ugcs, vns, vnsbuf = _inproj(hs, g, w_in_b[l], lng, lnb, F32, l, depth,
                                                            sample_bufs, True)
        sample_bufs = (ksbuf, vsbuf, vnsbuf)
        q, kbuf, vbuf, sga, ugc, vn, atts = _inproj_decode(
            hp, g, w_in_b[l], lng, lnb, l, depth, prompt_bufs,
            pt, qs, ksbuf, vsbuf, sgas, ck, cv, sg, lams, lam_init, dec_batch, dec_seq, n_pages)
        prompt_bufs = (kbuf, vbuf)

        att = _prompt_attn(q, kbuf, vbuf, l, sga, sg, lams, lam_init, batch, seq)
        bsb = jnp.repeat(b_s[l].T, HEAD_W, axis=1)
        hp = _mix_out(hp, att, ugc, vn, w_s[l], bsb, w_out_b[l], fg, final)

        ws8 = w_s[l][:, :dec_seq, :dec_seq]
        ws_bd = (eye[None, :, None, :, None] * ws8[:, None, :, None, :]).reshape(N_HEADS, CHUNK, CHUNK)
        bsb = jnp.tile(jnp.repeat(b_s[l][:, :dec_seq].T, HEAD_W, axis=1), (CHUNK // dec_seq, 1))
        hs = _mix_out(hs, atts, ugcs, vns, ws_bd, bsb, w_out_b[l], fg, final)

    shape_p = (depth, batch, seq, N_HEADS, HEAD_W)
    shape_s = (depth, dec_batch, dec_seq, N_HEADS, HEAD_W)
    return (hp.reshape(batch, seq, D_MODEL),
            hs.reshape(dec_batch, dec_seq, D_MODEL),
            prompt_bufs[0].reshape(shape_p),
            prompt_bufs[1].reshape(shape_p),
            sample_bufs[0].reshape(shape_s),
            sample_bufs[1].reshape(shape_s),
            sample_bufs[2].reshape(shape_s))
```

```python
import functools
import math

import numpy as np
import jax
import jax.numpy as jnp
from jax import lax
from jax.experimental import pallas as pl
from jax.experimental.pallas import tpu as pltpu

F32 = jnp.float32
BF16 = jnp.bfloat16

D_MODEL = 1024
N_HEADS = 4
DK = 64
HEAD_W = 128
D_GRP = N_HEADS * HEAD_W
N_GRP = 7
CHUNK = 128
PAGE = 128
EPS = 1e-6
NEG = -1e30
ONES_ROWS = 16
POS_LANES = 128
SLOPE_PARTS = 3
LOG2E = np.float32(math.log2(math.e))

ROW_TILE = 512
ATT_BLK = 256
DEC_PER_STEP = 2
NEW_ROWS = 128
VMEM_LIMIT = 56 * 1024 * 1024


def _lambda_init(layer):
    return 0.8 - 0.6 * math.exp(-0.3 * layer)


def _alibi_slope(h):
    return 2.0 ** (-8.0 * (h + 1) / N_HEADS)


def _slope_parts(h):
    target = np.float32(_alibi_slope(h)) * LOG2E
    rest, parts = target, []
    for _ in range(SLOPE_PARTS):
        part = np.asarray(rest, dtype=BF16).astype(np.float32)
        parts.append(float(part))
        rest = np.float32(rest - part)
    assert rest == 0.0
    return parts


def _lam_value(lq1, lk1, lq2, lk2, lam_init):
    a = jnp.sum(lq1[...] * lk1[...], axis=-1, keepdims=True)
    b = jnp.sum(lq2[...] * lk2[...], axis=-1, keepdims=True)
    return jnp.exp(a) - jnp.exp(b) + lam_init


def _silu(x):
    return x * jax.nn.sigmoid(x)


def _head_rows(h, n_rows, first_row=0):
    return pl.ds(first_row * N_HEADS + h, n_rows, stride=N_HEADS)


def _inproj_stages(x_ref, g_ref, w_ref, lng_ref, lnb_ref, outs, with_vn_out):
    q_ref, k_ref, v_ref, sga_ref, ugc_ref, vn_ref = outs[:6]
    tm = x_ref.shape[0]

    normed = []

    def proj(i):
        if not normed:
            x = x_ref[...]
            ms = jnp.mean(x * x, axis=-1, keepdims=True)
            normed.append((x * lax.rsqrt(ms + EPS) * g_ref[...]).astype(BF16))
        return jnp.dot(normed[0], w_ref[:, i * D_GRP:(i + 1) * D_GRP], preferred_element_type=F32)

    def store_heads(ref, c, val):
        if len(ref.shape) == 3:
            for d in range(ref.shape[0]):
                ref[d, _head_rows(c, tm), :] = val
        else:
            ref[_head_rows(c, tm), :] = val

    def store_all_heads(ref, val):
        for c in range(N_HEADS):
            store_heads(ref, c, val[:, c * HEAD_W:(c + 1) * HEAD_W])

    def q_stage():
        q_ref[...] = (proj(0) * float(np.float32(DK ** -0.5) * LOG2E)).astype(q_ref.dtype)

    def k_stage():
        store_all_heads(k_ref, proj(1))

    def v_stage():
        store_all_heads(v_ref, proj(2))

    def gate_stage():
        sga_ref[...] = _silu(proj(3)).astype(sga_ref.dtype)

    def ugc_stage():
        ugc_ref[...] = (proj(4) * _silu(proj(6))).astype(ugc_ref.dtype)

    def vn_stage():
        vc = proj(5)
        for c in range(N_HEADS):
            sl = slice(c * HEAD_W, (c + 1) * HEAD_W)
            blk = vc[:, sl]
            mu = jnp.mean(blk, axis=-1, keepdims=True)
            xc = blk - mu
            var = jnp.mean(xc * xc, axis=-1, keepdims=True)
            vn = xc * lax.rsqrt(var + EPS) * lng_ref[:, sl] + lnb_ref[:, sl]
            vn_ref[:, sl] = vn.astype(vn_ref.dtype)
            if with_vn_out:
                store_heads(outs[6], c, vn)

    return [q_stage, k_stage, v_stage, gate_stage, ugc_stage, vn_stage]


def _inproj_kernel(*refs, n_alias, with_vn_out):
    outs = refs[5 + n_alias:]
    for stage in _inproj_stages(*refs[:5], outs, with_vn_out):
        stage()


def _inproj_out_specs(m, tm, act_dtype, layer, depth, first, with_vn_out, index):
    row = pl.BlockSpec((tm, D_GRP), lambda *i: (index(*i), 0))
    act = jax.ShapeDtypeStruct((m, D_GRP), act_dtype)
    stacked = jax.ShapeDtypeStruct((depth, m * N_HEADS, HEAD_W), F32)
    if first:
        stacked_spec = pl.BlockSpec((depth, tm * N_HEADS, HEAD_W), lambda *i: (0, index(*i), 0))
    else:
        stacked_spec = pl.BlockSpec((None, tm * N_HEADS, HEAD_W), lambda *i: (layer, index(*i), 0))
    extra = 1 if with_vn_out else 0
    out_shape = [act, stacked, stacked, act, act, act] + [stacked] * extra
    out_specs = [row, stacked_spec, stacked_spec, row, row, row] + [stacked_spec] * extra
    return out_shape, out_specs, [1, 2] + [6] * extra


def _inproj(x, g, w, lng, lnb, act_dtype, layer, depth, prev, with_vn_out):
    m = x.shape[0]
    tm = ROW_TILE
    const = lambda shape: pl.BlockSpec(shape, lambda i: (0, 0))
    out_shape, out_specs, stacked_idx = _inproj_out_specs(
        m, tm, act_dtype, layer, depth, prev is None, with_vn_out, lambda i: i)
    prev = list(prev) if prev is not None else []
    aliases = {5 + n: stacked_idx[n] for n in range(len(prev))}
    return pl.pallas_call(
        functools.partial(_inproj_kernel, n_alias=len(prev), with_vn_out=with_vn_out),
        grid=(m // tm,),
        in_specs=[pl.BlockSpec((tm, D_MODEL), lambda i: (i, 0)), const((1, D_MODEL)),
                  const((D_MODEL, N_GRP * D_GRP)), const((1, D_GRP)), const((1, D_GRP))]
                 + [pl.BlockSpec(memory_space=pl.ANY)] * len(prev),
        out_specs=out_specs,
        out_shape=out_shape,
        input_output_aliases=aliases,
        compiler_params=pltpu.CompilerParams(dimension_semantics=("parallel",),
                                             vmem_limit_bytes=VMEM_LIMIT),
        name="inproj",
    )(x, g, w, lng, lnb, *prev)


def _prompt_attn_kernel(q_ref, k_ref, v_ref, sga_ref, sg_ref, lq1, lk1, lq2, lk2,
                        o_ref, kb_sc, vt_sc, *stat_sc, lam_init, tb, nblk):
    m_sc = stat_sc[:N_HEADS]
    acc_sc = stat_sc[N_HEADS:]
    i = pl.program_id(1)

    @pl.when(i == 0)
    def _():
        pos = lax.broadcasted_iota(jnp.int32, (tb, POS_LANES), 0)
        lane = lax.broadcasted_iota(jnp.int32, (tb, POS_LANES), 1)
        for jb in range(nblk):
            kpos = pos + jb * tb
            hi = (kpos // PAGE) * PAGE
            aug = jnp.where(lane >= 2 * SLOPE_PARTS, 0, jnp.where(lane % 2 == 0, hi, kpos - hi))
            aug = aug.astype(F32).astype(BF16)
            for h in range(N_HEADS):
                rows = _head_rows(h, tb, jb * tb)
                kb_sc[h, jb * tb:(jb + 1) * tb, 0:HEAD_W] = k_ref[rows, :].astype(BF16)
                kb_sc[h, jb * tb:(jb + 1) * tb, HEAD_W:HEAD_W + POS_LANES] = aug
                vt_sc[h, jb, 0:HEAD_W, :] = v_ref[rows, :].T.astype(BF16)
                vt_sc[h, jb, HEAD_W:HEAD_W + ONES_ROWS, :] = jnp.ones((ONES_ROWS, tb), BF16)

    lane = lax.broadcasted_iota(jnp.int32, (tb, HEAD_W), 1)
    qz = []
    for h in range(N_HEADS):
        q = q_ref[:, h * HEAD_W:(h + 1) * HEAD_W]
        zero = jnp.zeros_like(q)
        slope = jnp.zeros((tb, POS_LANES), F32)
        for n, part in enumerate(_slope_parts(h)):
            slope = jnp.where(lane // 2 == n, part, slope)
        slope = slope.astype(BF16)
        q2 = jnp.concatenate([jnp.where(lane < DK, q, zero), jnp.where(lane >= DK, q, zero)], axis=0)
        qz.append(jnp.concatenate([q2, jnp.concatenate([slope, slope], axis=0)], axis=1))
        m_sc[h][...] = jnp.full(m_sc[h].shape, NEG, F32)
        acc_sc[h][...] = jnp.zeros(acc_sc[h].shape, F32)

    def steps(first_blk, nkb, masked):
        heads = range(N_HEADS)
        rows = pl.ds(pl.multiple_of(first_blk * tb, tb), nkb * tb)
        s = [lax.dot_general(kb_sc[h, rows, :], qz[h], (((1,), (1,)), ((), ())),
                             preferred_element_type=F32) for h in heads]
        if masked:
            key_i = lax.broadcasted_iota(jnp.int32, (nkb * tb, 2 * tb), 0)
            qry_i = lax.broadcasted_iota(jnp.int32, (nkb * tb, 2 * tb), 1)
            qry_i = jnp.where(qry_i >= tb, qry_i - tb, qry_i) + (nkb - 1) * tb
            s = [jnp.where(key_i <= qry_i, sh, NEG) for sh in s]
        m_old = [m_sc[h][...] for h in heads]
        m_new = [jnp.maximum(m_old[h], jnp.max(s[h], axis=0, keepdims=True)) for h in heads]
        p = [jnp.exp2(s[h] - m_new[h]).astype(BF16) for h in heads]
        for h in heads:
            pv = jnp.dot(vt_sc[h, first_blk], p[h][0:tb, :], preferred_element_type=F32)
            for kb in range(1, nkb):
                pv = pv + jnp.dot(vt_sc[h, first_blk + kb], p[h][kb * tb:(kb + 1) * tb, :],
                                  preferred_element_type=F32)
            acc_sc[h][...] = acc_sc[h][...] * jnp.exp2(m_old[h] - m_new[h]) + pv
            m_sc[h][...] = m_new[h]

    def body(jj, carry):
        steps(2 * jj, 2, False)
        return carry

    lax.fori_loop(0, i // 2, body, 0)

    @pl.when(i % 2 == 1)
    def _():
        steps(i - 1, 2, True)

    @pl.when(i % 2 == 0)
    def _():
        steps(i, 1, True)

    lam = _lam_value(lq1, lk1, lq2, lk2, lam_init)
    for h in range(N_HEADS):
        sl = slice(h * HEAD_W, (h + 1) * HEAD_W)
        acc = acc_sc[h]
        inv = 1.0 / acc[HEAD_W:HEAD_W + 1, :]
        o1 = acc[0:HEAD_W, 0:tb] * inv[:, 0:tb]
        o2 = acc[0:HEAD_W, tb:2 * tb] * inv[:, tb:2 * tb]
        o = (o1 - lam * o2).T
        ms = jnp.mean(o * o, axis=-1, keepdims=True)
        att = o * lax.rsqrt(ms + EPS) * sg_ref[...] * (1.0 - lam_init)
        o_ref[:, sl] = (att * sga_ref[:, sl].astype(F32)).astype(o_ref.dtype)


def _prompt_attn(q, kbuf, vbuf, layer, sga, sg, lams, lam_init, batch, seq):
    tb = ATT_BLK
    nblk = seq // tb
    qspec = pl.BlockSpec((tb, D_GRP), lambda b, i: (b * nblk + i, 0))
    kvspec = pl.BlockSpec((None, seq * N_HEADS, HEAD_W), lambda b, i: (layer, b, 0))
    const = lambda shape: pl.BlockSpec(shape, lambda b, i: (0, 0))
    stats = ([pltpu.VMEM((1, 2 * tb), F32)] * N_HEADS
             + [pltpu.VMEM((HEAD_W + ONES_ROWS, 2 * tb), F32)] * N_HEADS)
    return pl.pallas_call(
        functools.partial(_prompt_attn_kernel, lam_init=lam_init, tb=tb, nblk=nblk),
        grid=(batch, nblk),
        in_specs=[qspec, kvspec, kvspec, qspec, const((1, HEAD_W))] + [const((1, DK))] * 4,
        out_specs=qspec,
        out_shape=jax.ShapeDtypeStruct(q.shape, BF16),
        scratch_shapes=[pltpu.VMEM((N_HEADS, seq, HEAD_W + POS_LANES), BF16),
                        pltpu.VMEM((N_HEADS, nblk, HEAD_W + ONES_ROWS, tb), BF16)] + stats,
        compiler_params=pltpu.CompilerParams(
            dimension_semantics=("parallel", "arbitrary"),
            vmem_limit_bytes=VMEM_LIMIT),
        name="prompt_attn",
    )(q, kbuf, vbuf, sga, sg, *lams)


def _sample_bias(n_pages, dec_seq):
    past = n_pages * PAGE
    n_pos = past + NEW_ROWS // N_HEADS
    t = np.arange(dec_seq)[:, None]
    kpos = np.arange(n_pos)[None, :]
    visible = (kpos < past) | (kpos - past <= t)
    bias = np.full((2, N_HEADS, dec_seq, n_pos, N_HEADS), NEG, np.float32)
    for h in range(N_HEADS):
        alibi = -np.float32(_alibi_slope(h)) * (past + t - kpos).astype(np.float32) * LOG2E
        bias[:, h, :, :, h] = np.where(visible, alibi, NEG)
    return bias.reshape(2 * N_HEADS * dec_seq, n_pos * N_HEADS)


def _paged_attention(pt_ref, q_ref, kn_ref, vn_ref, sga_ref, bias_ref, sg_ref, lams, ck_hbm, cv_hbm,
                     o_ref, kbuf, vbuf, ksem, vsem,
                     *, layer, lam_init, n_pages, dec_seq, fillers):
    t = pl.program_id(0)
    n_rows = 2 * N_HEADS * dec_seq
    half = N_HEADS * dec_seq
    page_rows = PAGE * N_HEADS
    past_rows = n_pages * page_rows
    new_rows = dec_seq * N_HEADS
    slots = range(DEC_PER_STEP)

    def page_copies(seq_idx, slot):
        copies = []
        for j in range(n_pages):
            page = pt_ref[seq_idx * n_pages + j]
            copies.append(pltpu.make_async_copy(ck_hbm.at[layer, page], kbuf.at[slot, j], ksem.at[slot]))
            copies.append(pltpu.make_async_copy(cv_hbm.at[layer, page], vbuf.at[slot, j], vsem.at[slot]))
        return copies

    @pl.when(t == 0)
    def _():
        for slot in slots:
            for cp in page_copies(slot, slot):
                cp.start()

    lane = lax.broadcasted_iota(jnp.int32, (half, HEAD_W), 1)
    lam = _lam_value(*lams, lam_init)
    pad = jnp.zeros((NEW_ROWS - new_rows, HEAD_W), F32)
    nt = (((1,), (1,)), ((), ()))

    def attend(slot, seq_idx):
        rows = slice(slot * dec_seq, (slot + 1) * dec_seq)
        new = slice(slot * new_rows, (slot + 1) * new_rows)
        k_new = jnp.concatenate([kn_ref[new, :], pad], axis=0)
        v_new = jnp.concatenate([vn_ref[new, :], pad], axis=0)
        qh = jnp.concatenate([q_ref[rows, h * HEAD_W:(h + 1) * HEAD_W] for h in range(N_HEADS)], axis=0)
        qrows = jnp.concatenate([jnp.where(lane < DK, qh, 0.0), jnp.where(lane >= DK, qh, 0.0)], axis=0)
        s = [lax.dot_general(qrows, kbuf[slot, j], nt, preferred_element_type=F32) for j in range(n_pages)]
        s.append(lax.dot_general(qrows, k_new, nt, preferred_element_type=F32))
        s = jnp.concatenate(s, axis=1) + bias_ref[...]
        yield
        p = jnp.exp2(s - jnp.max(s, axis=-1, keepdims=True))
        inv = 1.0 / jnp.sum(p, axis=-1, keepdims=True)
        yield
        pd = p[0:half, :] * inv[0:half, :] - p[half:n_rows, :] * (lam * inv[half:n_rows, :])
        res = jnp.dot(pd[:, past_rows:], v_new, preferred_element_type=F32)
        for j in range(n_pages):
            res = res + jnp.dot(pd[:, j * page_rows:(j + 1) * page_rows], vbuf[slot, j],
                                preferred_element_type=F32)

        @pl.when(t + 1 < pl.num_programs(0))
        def _():
            for cp in page_copies(seq_idx + DEC_PER_STEP, slot):
                cp.start()

        yield
        for h in range(N_HEADS):
            sl = slice(h * HEAD_W, (h + 1) * HEAD_W)
            o = res[h * dec_seq:(h + 1) * dec_seq, :]
            ms = jnp.mean(o * o, axis=-1, keepdims=True)
            att = o * lax.rsqrt(ms + EPS) * sg_ref[...] * (1.0 - lam_init)
            o_ref[rows, sl] = (att * sga_ref[rows, sl]).astype(o_ref.dtype)

    for slot in slots:
        seq_idx = DEC_PER_STEP * t + slot
        for cp in page_copies(seq_idx, slot):
            cp.wait()
        pending = list(fillers[slot])
        for _ in attend(slot, seq_idx):
            while pending:
                pending.pop(0)()


def _inproj_decode_kernel(pt_ref, x_ref, g_ref, w_ref, lng_ref, lnb_ref,
                          qs_ref, kn_ref, vn_ref, sgas_ref, bias_ref, sg_ref, lq1, lk1, lq2, lk2,
                          ck_hbm, cv_hbm, *rest, n_alias, layer, lam_init, n_pages, dec_seq):
    rest = rest[n_alias:]
    outs, att_ref, scratch = rest[:6], rest[6], rest[7:]
    stages = _inproj_stages(x_ref, g_ref, w_ref, lng_ref, lnb_ref, outs, False)
    per_slot = -(-len(stages) // DEC_PER_STEP)
    stages = stages[:per_slot] + stages[per_slot:][::-1]
    fillers = [stages[slot * per_slot:(slot + 1) * per_slot] for slot in range(DEC_PER_STEP)]
    _paged_attention(pt_ref, qs_ref, kn_ref, vn_ref, sgas_ref, bias_ref, sg_ref, (lq1, lk1, lq2, lk2),
                     ck_hbm, cv_hbm, att_ref, *scratch, layer=layer, lam_init=lam_init,
                     n_pages=n_pages, dec_seq=dec_seq, fillers=fillers)


def _inproj_decode(x, g, w, lng, lnb, layer, depth, prev,
                   pt, qs, knbuf, vnbuf, sgas, cache_k, cache_v, sg, lams, lam_init,
                   dec_batch, dec_seq, n_pages):
    m = x.shape[0]
    n = DEC_PER_STEP
    n_steps = dec_batch // n
    tm = m // n_steps
    bias = jnp.asarray(_sample_bias(n_pages, dec_seq))
    const = lambda shape, **kw: pl.BlockSpec(shape, lambda t, pt: (0, 0), **kw)
    once = dict(pipeline_mode=pl.Buffered(1))
    seqspec = pl.BlockSpec((n * dec_seq, D_GRP), lambda t, pt: (t, 0))
    newspec = pl.BlockSpec((None, n * dec_seq * N_HEADS, HEAD_W), lambda t, pt: (layer, t, 0))
    hbm = pl.BlockSpec(memory_space=pl.ANY)
    out_shape, out_specs, stacked_idx = _inproj_out_specs(
        m, tm, BF16, layer, depth, prev is None, False, lambda t, pt: t)
    prev = list(prev) if prev is not None else []
    n_inputs = 18
    aliases = {n_inputs + i: stacked_idx[i] for i in range(len(prev))}
    page_buf = pltpu.VMEM((n, n_pages, PAGE * N_HEADS, HEAD_W), F32)
    grid_spec = pltpu.PrefetchScalarGridSpec(
        num_scalar_prefetch=1,
        grid=(n_steps,),
        in_specs=[pl.BlockSpec((tm, D_MODEL), lambda t, pt: (t, 0)), const((1, D_MODEL)),
                  const((D_MODEL, N_GRP * D_GRP), **once), const((1, D_GRP)), const((1, D_GRP)),
                  seqspec, newspec, newspec, seqspec, const(bias.shape, **once), const((1, HEAD_W))]
                 + [const((1, DK))] * 4 + [hbm, hbm] + [hbm] * len(prev),
        out_specs=out_specs + [seqspec],
        scratch_shapes=[page_buf, page_buf, pltpu.SemaphoreType.DMA((n,)), pltpu.SemaphoreType.DMA((n,))],
    )
    return pl.pallas_call(
        functools.partial(_inproj_decode_kernel, n_alias=len(prev), layer=layer, lam_init=lam_init,
                          n_pages=n_pages, dec_seq=dec_seq),
        grid_spec=grid_spec,
        out_shape=out_shape + [jax.ShapeDtypeStruct(qs.shape, F32)],
        input_output_aliases=aliases,
        compiler_params=pltpu.CompilerParams(dimension_semantics=("arbitrary",),
                                             vmem_limit_bytes=VMEM_LIMIT),
        name="inproj_decode",
    )(pt, x, g, w, lng, lnb, qs, knbuf, vnbuf, sgas, bias, sg, *lams, cache_k, cache_v, *prev)


def _mix_out_kernel(x_ref, att_ref, ugc_ref, vn_ref, ws_ref, bsb_ref, wout_ref, fg_ref,
                    o_ref, cat_sc, *, final):
    tm = x_ref.shape[0]
    nc = tm // CHUNK
    r = lax.broadcasted_iota(jnp.int32, (CHUNK, CHUNK), 0)
    c = lax.broadcasted_iota(jnp.int32, (CHUNK, CHUNK), 1)
    cat_sc[:, 0:D_GRP] = att_ref[...].astype(BF16)
    for h in range(N_HEADS):
        sl = slice(h * HEAD_W, (h + 1) * HEAD_W)
        ws = jnp.where(c <= r, ws_ref[h], 0.0).astype(BF16)
        vn = jnp.concatenate([vn_ref[ci * CHUNK:(ci + 1) * CHUNK, sl].astype(BF16)
                              for ci in range(nc)], axis=1)
        mixed = jnp.dot(ws, vn, preferred_element_type=F32)
        for ci in range(nc):
            rows = slice(ci * CHUNK, (ci + 1) * CHUNK)
            mc = mixed[:, ci * HEAD_W:(ci + 1) * HEAD_W] + bsb_ref[:, sl]
            cat_sc[rows, D_GRP + h * HEAD_W:D_GRP + (h + 1) * HEAD_W] = (
                ugc_ref[rows, sl].astype(F32) * mc).astype(BF16)
    y = x_ref[...] + jnp.dot(cat_sc[...], wout_ref[...], preferred_element_type=F32)
    if final:
        ms = jnp.mean(y * y, axis=-1, keepdims=True)
        y = y * lax.rsqrt(ms + EPS) * fg_ref[...]
    o_ref[...] = y


def _mix_out(x, att, ugc, vn, ws, bsb, wout, fg, final):
    m = x.shape[0]
    tm = ROW_TILE
    row = lambda width: pl.BlockSpec((tm, width), lambda i: (i, 0))
    return pl.pallas_call(
        functools.partial(_mix_out_kernel, final=final),
        grid=(m // tm,),
        in_specs=[row(D_MODEL), row(D_GRP), row(D_GRP), row(D_GRP),
                  pl.BlockSpec((N_HEADS, CHUNK, CHUNK), lambda i: (0, 0, 0)),
                  pl.BlockSpec((CHUNK, D_GRP), lambda i: (0, 0)),
                  pl.BlockSpec((2 * D_GRP, D_MODEL), lambda i: (0, 0)),
                  pl.BlockSpec((1, D_MODEL), lambda i: (0, 0))],
        out_specs=row(D_MODEL),
        out_shape=jax.ShapeDtypeStruct((m, D_MODEL), F32),
        scratch_shapes=[pltpu.VMEM((tm, 2 * D_GRP), BF16)],
        compiler_params=pltpu.CompilerParams(dimension_semantics=("parallel",),
                                             vmem_limit_bytes=VMEM_LIMIT),
        name="mix_out",
    )(x, att, ugc, vn, ws, bsb, wout, fg)


def kernel(x_prompt, x_sample, cache_k, cache_v, page_table, norm_g, w_in, lam_q1, lam_k1,
           lam_q2, lam_k2, subln_g, ln_v_g, ln_v_b, w_s, b_s, w_out, final_g):
    batch, seq, _ = x_prompt.shape
    dec_batch, dec_seq, _ = x_sample.shape
    depth, n_pool = cache_k.shape[0], cache_k.shape[1]
    n_pages = page_table.shape[1]
    assert seq % ATT_BLK == 0 and (batch * seq) % ROW_TILE == 0 and (dec_batch * dec_seq) % ROW_TILE == 0
    assert CHUNK % dec_seq == 0 and dec_seq % 8 == 0 and ATT_BLK % PAGE == 0
    assert dec_seq * N_HEADS <= NEW_ROWS and dec_batch % DEC_PER_STEP == 0
    assert (batch * seq) % (dec_batch // DEC_PER_STEP) == 0

    hp = x_prompt.reshape(batch * seq, D_MODEL)
    hs = x_sample.reshape(dec_batch * dec_seq, D_MODEL)
    ck = cache_k.reshape(depth, n_pool, PAGE * N_HEADS, HEAD_W)
    cv = cache_v.reshape(depth, n_pool, PAGE * N_HEADS, HEAD_W)
    pt = page_table.reshape(-1)
    w_in_b = w_in.astype(BF16)
    w_out_b = w_out.astype(BF16)
    fg = final_g.reshape(1, D_MODEL)
    eye = jnp.eye(CHUNK // dec_seq, dtype=F32)

    prompt_bufs, sample_bufs = None, None
    for l in range(depth):
        lam_init = _lambda_init(l)
        g = norm_g[l].reshape(1, D_MODEL)
        lng = ln_v_g[l].reshape(1, D_GRP)
        lnb = ln_v_b[l].reshape(1, D_GRP)
        sg = subln_g[l].reshape(1, HEAD_W)
        lams = [a[l].reshape(1, DK) for a in (lam_q1, lam_k1, lam_q2, lam_k2)]
        final = l == depth - 1

        qs, ksbuf, vsbuf, sgas, ugcs, vns, vnsbuf = _inproj(hs, g, w_in_b[l], lng, lnb, F32, l, depth,
                                                            sample_bufs, True)
        sample_bufs = (ksbuf, vsbuf, vnsbuf)
        q, kbuf, vbuf, sga, ugc, vn, atts = _inproj_decode(
            hp, g, w_in_b[l], lng, lnb, l, depth, prompt_bufs,
            pt, qs, ksbuf, vsbuf, sgas, ck, cv, sg, lams, lam_init, dec_batch, dec_seq, n_pages)
        prompt_bufs = (kbuf, vbuf)

        att = _prompt_attn(q, kbuf, vbuf, l, sga, sg, lams, lam_init, batch, seq)
        bsb = jnp.repeat(b_s[l].T, HEAD_W, axis=1)
        hp = _mix_out(hp, att, ugc, vn, w_s[l], bsb, w_out_b[l], fg, final)

        ws8 = w_s[l][:, :dec_seq, :dec_seq]
        ws_bd = (eye[None, :, None, :, None] * ws8[:, None, :, None, :]).reshape(N_HEADS, CHUNK, CHUNK)
        bsb = jnp.tile(jnp.repeat(b_s[l][:, :dec_seq].T, HEAD_W, axis=1), (CHUNK // dec_seq, 1))
        hs = _mix_out(hs, atts, ugcs, vns, ws_bd, bsb, w_out_b[l], fg, final)

    shape_p = (depth, batch, seq, N_HEADS, HEAD_W)
    shape_s = (depth, dec_batch, dec_seq, N_HEADS, HEAD_W)
    return (hp.reshape(batch, seq, D_MODEL),
            hs.reshape(dec_batch, dec_seq, D_MODEL),
            prompt_bufs[0].reshape(shape_p),
            prompt_bufs[1].reshape(shape_p),
            sample_bufs[0].reshape(shape_s),
            sample_bufs[1].reshape(shape_s),
            sample_bufs[2].reshape(shape_s))
```

```python
import functools
import math

import numpy as np
import jax
import jax.numpy as jnp
from jax import lax
from jax.experimental import pallas as pl
from jax.experimental.pallas import tpu as pltpu

F32 = jnp.float32
BF16 = jnp.bfloat16

D_MODEL = 1024
N_HEADS = 4
DK = 64
HEAD_W = 128
D_GRP = N_HEADS * HEAD_W
N_GRP = 7
CHUNK = 128
PAGE = 128
EPS = 1e-6
NEG = -1e30
ONES_ROWS = 16
POS_LANES = 128
SLOPE_PARTS = 3
LOG2E = np.float32(math.log2(math.e))

ROW_TILE = 512
ATT_BLK = 256
NEW_ROWS = 128
VMEM_LIMIT = 56 * 1024 * 1024


def _lambda_init(layer):
    return 0.8 - 0.6 * math.exp(-0.3 * layer)


def _alibi_slope(h):
    return 2.0 ** (-8.0 * (h + 1) / N_HEADS)


def _slope_parts(h):
    target = np.float32(_alibi_slope(h)) * LOG2E
    rest, parts = target, []
    for _ in range(SLOPE_PARTS):
        part = np.asarray(rest, dtype=BF16).astype(np.float32)
        parts.append(float(part))
        rest = np.float32(rest - part)
    assert rest == 0.0
    return parts


def _lam_value(lq1, lk1, lq2, lk2, lam_init):
    a = jnp.sum(lq1[...] * lk1[...], axis=-1, keepdims=True)
    b = jnp.sum(lq2[...] * lk2[...], axis=-1, keepdims=True)
    return jnp.exp(a) - jnp.exp(b) + lam_init


def _silu(x):
    return x * jax.nn.sigmoid(x)


def _head_rows(h, n_rows, first_row=0):
    return pl.ds(first_row * N_HEADS + h, n_rows, stride=N_HEADS)


def _inproj_stages(x_ref, g_ref, w_ref, lng_ref, lnb_ref, outs, with_vn_out):
    q_ref, k_ref, v_ref, sga_ref, ugc_ref, vn_ref = outs[:6]
    tm = x_ref.shape[0]

    x = x_ref[...]
    ms = jnp.mean(x * x, axis=-1, keepdims=True)
    h = (x * lax.rsqrt(ms + EPS) * g_ref[...]).astype(BF16)

    def proj(i):
        return jnp.dot(h, w_ref[:, i * D_GRP:(i + 1) * D_GRP], preferred_element_type=F32)

    def store_heads(ref, c, val):
        if len(ref.shape) == 3:
            for d in range(ref.shape[0]):
                ref[d, _head_rows(c, tm), :] = val
        else:
            ref[_head_rows(c, tm), :] = val

    def store_all_heads(ref, val):
        for c in range(N_HEADS):
            store_heads(ref, c, val[:, c * HEAD_W:(c + 1) * HEAD_W])

    def q_stage():
        q_ref[...] = (proj(0) * float(np.float32(DK ** -0.5) * LOG2E)).astype(q_ref.dtype)

    def k_stage():
        store_all_heads(k_ref, proj(1))

    def v_stage():
        store_all_heads(v_ref, proj(2))

    def gate_stage():
        sga_ref[...] = _silu(proj(3)).astype(sga_ref.dtype)

    def ugc_stage():
        ugc_ref[...] = (proj(4) * _silu(proj(6))).astype(ugc_ref.dtype)

    def vn_stage():
        vc = proj(5)
        for c in range(N_HEADS):
            sl = slice(c * HEAD_W, (c + 1) * HEAD_W)
            blk = vc[:, sl]
            mu = jnp.mean(blk, axis=-1, keepdims=True)
            xc = blk - mu
            var = jnp.mean(xc * xc, axis=-1, keepdims=True)
            vn = xc * lax.rsqrt(var + EPS) * lng_ref[:, sl] + lnb_ref[:, sl]
            vn_ref[:, sl] = vn.astype(vn_ref.dtype)
            if with_vn_out:
                store_heads(outs[6], c, vn)

    return [q_stage, k_stage, v_stage, gate_stage, ugc_stage, vn_stage]


def _inproj_kernel(*refs, n_alias, with_vn_out):
    outs = refs[5 + n_alias:]
    for stage in _inproj_stages(*refs[:5], outs, with_vn_out):
        stage()


def _inproj_out_specs(m, tm, act_dtype, layer, depth, first, with_vn_out, index):
    row = pl.BlockSpec((tm, D_GRP), lambda *i: (index(*i), 0))
    act = jax.ShapeDtypeStruct((m, D_GRP), act_dtype)
    stacked = jax.ShapeDtypeStruct((depth, m * N_HEADS, HEAD_W), F32)
    if first:
        stacked_spec = pl.BlockSpec((depth, tm * N_HEADS, HEAD_W), lambda *i: (0, index(*i), 0))
    else:
        stacked_spec = pl.BlockSpec((None, tm * N_HEADS, HEAD_W), lambda *i: (layer, index(*i), 0))
    extra = 1 if with_vn_out else 0
    out_shape = [act, stacked, stacked, act, act, act] + [stacked] * extra
    out_specs = [row, stacked_spec, stacked_spec, row, row, row] + [stacked_spec] * extra
    return out_shape, out_specs, [1, 2] + [6] * extra


def _inproj(x, g, w, lng, lnb, act_dtype, layer, depth, prev, with_vn_out):
    m = x.shape[0]
    tm = ROW_TILE
    const = lambda shape: pl.BlockSpec(shape, lambda i: (0, 0))
    out_shape, out_specs, stacked_idx = _inproj_out_specs(
        m, tm, act_dtype, layer, depth, prev is None, with_vn_out, lambda i: i)
    prev = list(prev) if prev is not None else []
    aliases = {5 + n: stacked_idx[n] for n in range(len(prev))}
    return pl.pallas_call(
        functools.partial(_inproj_kernel, n_alias=len(prev), with_vn_out=with_vn_out),
        grid=(m // tm,),
        in_specs=[pl.BlockSpec((tm, D_MODEL), lambda i: (i, 0)), const((1, D_MODEL)),
                  const((D_MODEL, N_GRP * D_GRP)), const((1, D_GRP)), const((1, D_GRP))]
                 + [pl.BlockSpec(memory_space=pl.ANY)] * len(prev),
        out_specs=out_specs,
        out_shape=out_shape,
        input_output_aliases=aliases,
        compiler_params=pltpu.CompilerParams(dimension_semantics=("parallel",),
                                             vmem_limit_bytes=VMEM_LIMIT),
        name="inproj",
    )(x, g, w, lng, lnb, *prev)


def _prompt_attn_kernel(pt_ref, q_ref, k_ref, v_ref, sga_ref, sg_ref, lq1, lk1, lq2, lk2, *rest,
                        lam_init, tb, nblk, first_seq, layer, n_pages, dec_seq):
    paged_in, rest = rest[:N_PAGED_IN], rest[N_PAGED_IN:]
    o_ref, att_s_ref, kb_sc, vt_sc = rest[:4]
    m_sc = rest[4:4 + N_HEADS]
    acc_sc = rest[4 + N_HEADS:4 + 2 * N_HEADS]
    page_scratch = rest[4 + 2 * N_HEADS:]
    i = pl.program_id(1)

    _paged_attention(pt_ref, pl.program_id(0) * nblk + i, pl.num_programs(0) * nblk, first_seq,
                     *paged_in, att_s_ref, *page_scratch, layer=layer, lam_init=lam_init,
                     n_pages=n_pages, dec_seq=dec_seq, fillers=[[]])

    @pl.when(i == 0)
    def _():
        pos = lax.broadcasted_iota(jnp.int32, (tb, POS_LANES), 0)
        lane = lax.broadcasted_iota(jnp.int32, (tb, POS_LANES), 1)
        for jb in range(nblk):
            kpos = pos + jb * tb
            hi = (kpos // PAGE) * PAGE
            aug = jnp.where(lane >= 2 * SLOPE_PARTS, 0, jnp.where(lane % 2 == 0, hi, kpos - hi))
            aug = aug.astype(F32).astype(BF16)
            for h in range(N_HEADS):
                rows = _head_rows(h, tb, jb * tb)
                kb_sc[h, jb * tb:(jb + 1) * tb, 0:HEAD_W] = k_ref[rows, :].astype(BF16)
                kb_sc[h, jb * tb:(jb + 1) * tb, HEAD_W:HEAD_W + POS_LANES] = aug
                vt_sc[h, jb, 0:HEAD_W, :] = v_ref[rows, :].T.astype(BF16)
                vt_sc[h, jb, HEAD_W:HEAD_W + ONES_ROWS, :] = jnp.ones((ONES_ROWS, tb), BF16)

    lane = lax.broadcasted_iota(jnp.int32, (tb, HEAD_W), 1)
    qz = []
    for h in range(N_HEADS):
        q = q_ref[:, h * HEAD_W:(h + 1) * HEAD_W]
        zero = jnp.zeros_like(q)
        slope = jnp.zeros((tb, POS_LANES), F32)
        for n, part in enumerate(_slope_parts(h)):
            slope = jnp.where(lane // 2 == n, part, slope)
        slope = slope.astype(BF16)
        q2 = jnp.concatenate([jnp.where(lane < DK, q, zero), jnp.where(lane >= DK, q, zero)], axis=0)
        qz.append(jnp.concatenate([q2, jnp.concatenate([slope, slope], axis=0)], axis=1))
        m_sc[h][...] = jnp.full(m_sc[h].shape, NEG, F32)
        acc_sc[h][...] = jnp.zeros(acc_sc[h].shape, F32)

    def steps(first_blk, nkb, masked):
        heads = range(N_HEADS)
        rows = pl.ds(pl.multiple_of(first_blk * tb, tb), nkb * tb)
        s = [lax.dot_general(kb_sc[h, rows, :], qz[h], (((1,), (1,)), ((), ())),
                             preferred_element_type=F32) for h in heads]
        if masked:
            key_i = lax.broadcasted_iota(jnp.int32, (nkb * tb, 2 * tb), 0)
            qry_i = lax.broadcasted_iota(jnp.int32, (nkb * tb, 2 * tb), 1)
            qry_i = jnp.where(qry_i >= tb, qry_i - tb, qry_i) + (nkb - 1) * tb
            s = [jnp.where(key_i <= qry_i, sh, NEG) for sh in s]
        m_old = [m_sc[h][...] for h in heads]
        m_new = [jnp.maximum(m_old[h], jnp.max(s[h], axis=0, keepdims=True)) for h in heads]
        p = [jnp.exp2(s[h] - m_new[h]).astype(BF16) for h in heads]
        for h in heads:
            pv = jnp.dot(vt_sc[h, first_blk], p[h][0:tb, :], preferred_element_type=F32)
            for kb in range(1, nkb):
                pv = pv + jnp.dot(vt_sc[h, first_blk + kb], p[h][kb * tb:(kb + 1) * tb, :],
                                  preferred_element_type=F32)
            acc_sc[h][...] = acc_sc[h][...] * jnp.exp2(m_old[h] - m_new[h]) + pv
            m_sc[h][...] = m_new[h]

    def body(jj, carry):
        steps(2 * jj, 2, False)
        return carry

    lax.fori_loop(0, i // 2, body, 0)

    @pl.when(i % 2 == 1)
    def _():
        steps(i - 1, 2, True)

    @pl.when(i % 2 == 0)
    def _():
        steps(i, 1, True)

    lam = _lam_value(lq1, lk1, lq2, lk2, lam_init)
    for h in range(N_HEADS):
        sl = slice(h * HEAD_W, (h + 1) * HEAD_W)
        acc = acc_sc[h]
        inv = 1.0 / acc[HEAD_W:HEAD_W + 1, :]
        o1 = acc[0:HEAD_W, 0:tb] * inv[:, 0:tb]
        o2 = acc[0:HEAD_W, tb:2 * tb] * inv[:, tb:2 * tb]
        o = (o1 - lam * o2).T
        ms = jnp.mean(o * o, axis=-1, keepdims=True)
        att = o * lax.rsqrt(ms + EPS) * sg_ref[...] * (1.0 - lam_init)
        o_ref[:, sl] = (att * sga_ref[:, sl].astype(F32)).astype(o_ref.dtype)


def _prompt_attn(q, kbuf, vbuf, layer, sga, sg, lams, lam_init, batch, seq,
                 pt, paged, first_seq, dec_seq, n_pages):
    tb = ATT_BLK
    nblk = seq // tb
    qspec = pl.BlockSpec((tb, D_GRP), lambda b, i, pt: (b * nblk + i, 0))
    kvspec = pl.BlockSpec((None, seq * N_HEADS, HEAD_W), lambda b, i, pt: (layer, b, 0))
    const = lambda shape: pl.BlockSpec(shape, lambda b, i, pt: (0, 0))
    stats = ([pltpu.VMEM((1, 2 * tb), F32)] * N_HEADS
             + [pltpu.VMEM((HEAD_W + ONES_ROWS, 2 * tb), F32)] * N_HEADS)
    p_operands, p_specs, p_out_spec, p_out_shape, p_scratch = _paged_operands(
        *paged, layer, dec_seq, n_pages, first_seq, 1, batch * nblk, lambda b, i, pt: b * nblk + i)
    grid_spec = pltpu.PrefetchScalarGridSpec(
        num_scalar_prefetch=1,
        grid=(batch, nblk),
        in_specs=[qspec, kvspec, kvspec, qspec, const((1, HEAD_W))] + [const((1, DK))] * 4 + p_specs,
        out_specs=[qspec, p_out_spec],
        scratch_shapes=[pltpu.VMEM((N_HEADS, seq, HEAD_W + POS_LANES), BF16),
                        pltpu.VMEM((N_HEADS, nblk, HEAD_W + ONES_ROWS, tb), BF16)] + stats + p_scratch,
    )
    return pl.pallas_call(
        functools.partial(_prompt_attn_kernel, lam_init=lam_init, tb=tb, nblk=nblk, first_seq=first_seq,
                          layer=layer, n_pages=n_pages, dec_seq=dec_seq),
        grid_spec=grid_spec,
        out_shape=[jax.ShapeDtypeStruct(q.shape, BF16), p_out_shape],
        compiler_params=pltpu.CompilerParams(
            dimension_semantics=("arbitrary", "arbitrary"),
            vmem_limit_bytes=VMEM_LIMIT),
        name="prompt_attn",
    )(pt, q, kbuf, vbuf, sga, sg, *lams, *p_operands)


def _sample_bias(n_pages, dec_seq):
    past = n_pages * PAGE
    n_pos = past + NEW_ROWS // N_HEADS
    t = np.arange(dec_seq)[:, None]
    kpos = np.arange(n_pos)[None, :]
    visible = (kpos < past) | (kpos - past <= t)
    bias = np.full((2, N_HEADS, dec_seq, n_pos, N_HEADS), NEG, np.float32)
    for h in range(N_HEADS):
        alibi = -np.float32(_alibi_slope(h)) * (past + t - kpos).astype(np.float32) * LOG2E
        bias[:, h, :, :, h] = np.where(visible, alibi, NEG)
    return bias.reshape(2 * N_HEADS * dec_seq, n_pos * N_HEADS)


def _paged_attention(pt_ref, step, n_steps, first_seq, q_ref, kn_ref, vn_ref, sga_ref, bias_ref, sg_ref,
                     lq1, lk1, lq2, lk2, ck_hbm, cv_hbm, o_ref, kbuf, vbuf, ksem, vsem,
                     *, layer, lam_init, n_pages, dec_seq, fillers):
    n_seq = len(fillers)
    n_rows = 2 * N_HEADS * dec_seq
    half = N_HEADS * dec_seq
    page_rows = PAGE * N_HEADS
    past_rows = n_pages * page_rows
    new_rows = dec_seq * N_HEADS
    slots = range(n_seq)

    def page_copies(seq_idx, slot):
        copies = []
        for j in range(n_pages):
            page = pt_ref[seq_idx * n_pages + j]
            copies.append(pltpu.make_async_copy(ck_hbm.at[layer, page], kbuf.at[slot, j], ksem.at[slot]))
            copies.append(pltpu.make_async_copy(cv_hbm.at[layer, page], vbuf.at[slot, j], vsem.at[slot]))
        return copies

    @pl.when(step == 0)
    def _():
        for slot in slots:
            for cp in page_copies(first_seq + slot, slot):
                cp.start()

    lane = lax.broadcasted_iota(jnp.int32, (half, HEAD_W), 1)
    lam = _lam_value(lq1, lk1, lq2, lk2, lam_init)
    pad = jnp.zeros((NEW_ROWS - new_rows, HEAD_W), F32)
    nt = (((1,), (1,)), ((), ()))

    def attend(slot, seq_idx):
        rows = slice(slot * dec_seq, (slot + 1) * dec_seq)
        new = slice(slot * new_rows, (slot + 1) * new_rows)
        k_new = jnp.concatenate([kn_ref[new, :], pad], axis=0)
        v_new = jnp.concatenate([vn_ref[new, :], pad], axis=0)
        qh = jnp.concatenate([q_ref[rows, h * HEAD_W:(h + 1) * HEAD_W] for h in range(N_HEADS)], axis=0)
        qrows = jnp.concatenate([jnp.where(lane < DK, qh, 0.0), jnp.where(lane >= DK, qh, 0.0)], axis=0)
        s = [lax.dot_general(qrows, kbuf[slot, j], nt, preferred_element_type=F32) for j in range(n_pages)]
        s.append(lax.dot_general(qrows, k_new, nt, preferred_element_type=F32))
        s = jnp.concatenate(s, axis=1) + bias_ref[...]
        yield
        p = jnp.exp2(s - jnp.max(s, axis=-1, keepdims=True))
        inv = 1.0 / jnp.sum(p, axis=-1, keepdims=True)
        yield
        pv = jnp.dot(p[:, past_rows:], v_new, preferred_element_type=F32)
        for j in range(n_pages):
            pv = pv + jnp.dot(p[:, j * page_rows:(j + 1) * page_rows], vbuf[slot, j],
                              preferred_element_type=F32)
        res = pv[0:half, :] * inv[0:half, :] - pv[half:n_rows, :] * (lam * inv[half:n_rows, :])

        @pl.when(step + 1 < n_steps)
        def _():
            for cp in page_copies(seq_idx + n_seq, slot):
                cp.start()

        yield
        for h in range(N_HEADS):
            sl = slice(h * HEAD_W, (h + 1) * HEAD_W)
            o = res[h * dec_seq:(h + 1) * dec_seq, :]
            ms = jnp.mean(o * o, axis=-1, keepdims=True)
            att = o * lax.rsqrt(ms + EPS) * sg_ref[...] * (1.0 - lam_init)
            o_ref[rows, sl] = (att * sga_ref[rows, sl]).astype(o_ref.dtype)

    for slot in slots:
        seq_idx = first_seq + n_seq * step + slot
        for cp in page_copies(seq_idx, slot):
            cp.wait()
        pending = list(fillers[slot])
        for _ in attend(slot, seq_idx):
            if pending:
                pending.pop(0)()
        for filler in pending:
            filler()


N_PAGED_IN = 12


def _paged_operands(qs, knbuf, vnbuf, sgas, cache_k, cache_v, sg, lams, layer, dec_seq, n_pages,
                    first_seq, n_seq, n_steps, step_of):
    bias = jnp.asarray(_sample_bias(n_pages, dec_seq))
    first_blk = first_seq // n_seq
    const = lambda shape, **kw: pl.BlockSpec(shape, lambda *i: (0, 0), **kw)
    seqspec = pl.BlockSpec((n_seq * dec_seq, D_GRP), lambda *i: (first_blk + step_of(*i), 0))
    newspec = pl.BlockSpec((None, n_seq * dec_seq * N_HEADS, HEAD_W),
                           lambda *i: (layer, first_blk + step_of(*i), 0))
    hbm = pl.BlockSpec(memory_space=pl.ANY)
    operands = [qs, knbuf, vnbuf, sgas, bias, sg, *lams, cache_k, cache_v]
    in_specs = ([seqspec, newspec, newspec, seqspec, const(bias.shape, pipeline_mode=pl.Buffered(1)),
                 const((1, HEAD_W))] + [const((1, DK))] * 4 + [hbm, hbm])
    assert len(operands) == len(in_specs) == N_PAGED_IN
    out_spec = pl.BlockSpec((n_seq * dec_seq, D_GRP), lambda *i: (step_of(*i), 0))
    out_shape = jax.ShapeDtypeStruct((n_steps * n_seq * dec_seq, D_GRP), F32)
    page_buf = pltpu.VMEM((n_seq, n_pages, PAGE * N_HEADS, HEAD_W), F32)
    scratch = [page_buf, page_buf, pltpu.SemaphoreType.DMA((n_seq,)), pltpu.SemaphoreType.DMA((n_seq,))]
    return operands, in_specs, out_spec, out_shape, scratch


def _inproj_decode_kernel(pt_ref, x_ref, g_ref, w_ref, lng_ref, lnb_ref, *rest,
                          n_alias, n_seq, first_seq, layer, lam_init, n_pages, dec_seq):
    paged_in, rest = rest[:N_PAGED_IN], rest[N_PAGED_IN + n_alias:]
    outs, att_ref, scratch = rest[:6], rest[6], rest[7:]
    stages = _inproj_stages(x_ref, g_ref, w_ref, lng_ref, lnb_ref, outs, False)
    fillers = [stages[slot::n_seq] for slot in range(n_seq)]
    _paged_attention(pt_ref, pl.program_id(0), pl.num_programs(0), first_seq, *paged_in, att_ref, *scratch,
                     layer=layer, lam_init=lam_init, n_pages=n_pages, dec_seq=dec_seq, fillers=fillers)


def _inproj_decode(x, g, w, lng, lnb, layer, depth, prev, pt, paged, first_seq, n_seq, dec_seq, n_pages,
                   lam_init):
    m = x.shape[0]
    tm = ROW_TILE // 2
    n_steps = m // tm
    const = lambda shape, **kw: pl.BlockSpec(shape, lambda t, pt: (0, 0), **kw)
    once = dict(pipeline_mode=pl.Buffered(1))
    p_operands, p_specs, p_out_spec, p_out_shape, p_scratch = _paged_operands(
        *paged, layer, dec_seq, n_pages, first_seq, n_seq, n_steps, lambda t, pt: t)
    out_shape, out_specs, stacked_idx = _inproj_out_specs(
        m, tm, BF16, layer, depth, prev is None, False, lambda t, pt: t)
    prev = list(prev) if prev is not None else []
    n_inputs = 6 + N_PAGED_IN
    aliases = {n_inputs + i: stacked_idx[i] for i in range(len(prev))}
    grid_spec = pltpu.PrefetchScalarGridSpec(
        num_scalar_prefetch=1,
        grid=(n_steps,),
        in_specs=[pl.BlockSpec((tm, D_MODEL), lambda t, pt: (t, 0)), const((1, D_MODEL)),
                  const((D_MODEL, N_GRP * D_GRP), **once), const((1, D_GRP)), const((1, D_GRP))]
                 + p_specs + [pl.BlockSpec(memory_space=pl.ANY)] * len(prev),
        out_specs=out_specs + [p_out_spec],
        scratch_shapes=p_scratch,
    )
    return pl.pallas_call(
        functools.partial(_inproj_decode_kernel, n_alias=len(prev), n_seq=n_seq, first_seq=first_seq,
                          layer=layer, lam_init=lam_init, n_pages=n_pages, dec_seq=dec_seq),
        grid_spec=grid_spec,
        out_shape=out_shape + [p_out_shape],
        input_output_aliases=aliases,
        compiler_params=pltpu.CompilerParams(dimension_semantics=("arbitrary",),
                                             vmem_limit_bytes=VMEM_LIMIT),
        name="inproj_decode",
    )(pt, x, g, w, lng, lnb, *p_operands, *prev)


def _mix_out_kernel(x_ref, att_ref, ugc_ref, vn_ref, ws_ref, bsb_ref, wout_ref, fg_ref,
                    o_ref, cat_sc, *, final):
    tm = x_ref.shape[0]
    nc = tm // CHUNK
    r = lax.broadcasted_iota(jnp.int32, (CHUNK, CHUNK), 0)
    c = lax.broadcasted_iota(jnp.int32, (CHUNK, CHUNK), 1)
    cat_sc[:, 0:D_GRP] = att_ref[...].astype(BF16)
    for h in range(N_HEADS):
        sl = slice(h * HEAD_W, (h + 1) * HEAD_W)
        ws = jnp.where(c <= r, ws_ref[h], 0.0).astype(BF16)
        vn = jnp.concatenate([vn_ref[ci * CHUNK:(ci + 1) * CHUNK, sl].astype(BF16)
                              for ci in range(nc)], axis=1)
        mixed = jnp.dot(ws, vn, preferred_element_type=F32)
        for ci in range(nc):
            rows = slice(ci * CHUNK, (ci + 1) * CHUNK)
            mc = mixed[:, ci * HEAD_W:(ci + 1) * HEAD_W] + bsb_ref[:, sl]
            cat_sc[rows, D_GRP + h * HEAD_W:D_GRP + (h + 1) * HEAD_W] = (
                ugc_ref[rows, sl].astype(F32) * mc).astype(BF16)
    y = x_ref[...] + jnp.dot(cat_sc[...], wout_ref[...], preferred_element_type=F32)
    if final:
        ms = jnp.mean(y * y, axis=-1, keepdims=True)
        y = y * lax.rsqrt(ms + EPS) * fg_ref[...]
    o_ref[...] = y


def _mix_out(x, att, ugc, vn, ws, bsb, wout, fg, final):
    m = x.shape[0]
    tm = ROW_TILE
    row = lambda width: pl.BlockSpec((tm, width), lambda i: (i, 0))
    return pl.pallas_call(
        functools.partial(_mix_out_kernel, final=final),
        grid=(m // tm,),
        in_specs=[row(D_MODEL), row(D_GRP), row(D_GRP), row(D_GRP),
                  pl.BlockSpec((N_HEADS, CHUNK, CHUNK), lambda i: (0, 0, 0)),
                  pl.BlockSpec((CHUNK, D_GRP), lambda i: (0, 0)),
                  pl.BlockSpec((2 * D_GRP, D_MODEL), lambda i: (0, 0)),
                  pl.BlockSpec((1, D_MODEL), lambda i: (0, 0))],
        out_specs=row(D_MODEL),
        out_shape=jax.ShapeDtypeStruct((m, D_MODEL), F32),
        scratch_shapes=[pltpu.VMEM((tm, 2 * D_GRP), BF16)],
        compiler_params=pltpu.CompilerParams(dimension_semantics=("parallel",),
                                             vmem_limit_bytes=VMEM_LIMIT),
        name="mix_out",
    )(x, att, ugc, vn, ws, bsb, wout, fg)


def kernel(x_prompt, x_sample, cache_k, cache_v, page_table, norm_g, w_in, lam_q1, lam_k1,
           lam_q2, lam_k2, subln_g, ln_v_g, ln_v_b, w_s, b_s, w_out, final_g):
    batch, seq, _ = x_prompt.shape
    dec_batch, dec_seq, _ = x_sample.shape
    depth, n_pool = cache_k.shape[0], cache_k.shape[1]
    n_pages = page_table.shape[1]
    assert seq % ATT_BLK == 0 and (batch * seq) % ROW_TILE == 0 and (dec_batch * dec_seq) % ROW_TILE == 0
    assert CHUNK % dec_seq == 0 and dec_seq % 8 == 0 and ATT_BLK % PAGE == 0
    assert dec_seq * N_HEADS <= NEW_ROWS
    n_seq_attn = batch * (seq // ATT_BLK)
    proj_steps = (batch * seq) // (ROW_TILE // 2)
    assert dec_batch > n_seq_attn and (dec_batch - n_seq_attn) % proj_steps == 0
    n_seq_proj = (dec_batch - n_seq_attn) // proj_steps

    hp = x_prompt.reshape(batch * seq, D_MODEL)
    hs = x_sample.reshape(dec_batch * dec_seq, D_MODEL)
    ck = cache_k.reshape(depth, n_pool, PAGE * N_HEADS, HEAD_W)
    cv = cache_v.reshape(depth, n_pool, PAGE * N_HEADS, HEAD_W)
    pt = page_table.reshape(-1)
    w_in_b = w_in.astype(BF16)
    w_out_b = w_out.astype(BF16)
    fg = final_g.reshape(1, D_MODEL)
    eye = jnp.eye(CHUNK // dec_seq, dtype=F32)

    prompt_bufs, sample_bufs = None, None
    for l in range(depth):
        lam_init = _lambda_init(l)
        g = norm_g[l].reshape(1, D_MODEL)
        lng = ln_v_g[l].reshape(1, D_GRP)
        lnb = ln_v_b[l].reshape(1, D_GRP)
        sg = subln_g[l].reshape(1, HEAD_W)
        lams = [a[l].reshape(1, DK) for a in (lam_q1, lam_k1, lam_q2, lam_k2)]
        final = l == depth - 1

        qs, ksbuf, vsbuf, sgas, ugcs, vns, vnsbuf = _inproj(hs, g, w_in_b[l], lng, lnb, F32, l, depth,
                                                            sample_bufs, True)
        sample_bufs = (ksbuf, vsbuf, vnsbuf)
        paged = (qs, ksbuf, vsbuf, sgas, ck, cv, sg, lams)
        q, kbuf, vbuf, sga, ugc, vn, atts_proj = _inproj_decode(
            hp, g, w_in_b[l], lng, lnb, l, depth, prompt_bufs,
            pt, paged, 0, n_seq_proj, dec_seq, n_pages, lam_init)
        prompt_bufs = (kbuf, vbuf)

        att, atts_attn = _prompt_attn(q, kbuf, vbuf, l, sga, sg, lams, lam_init, batch, seq,
                                      pt, paged, dec_batch - n_seq_attn, dec_seq, n_pages)
        atts = jnp.concatenate([atts_proj, atts_attn], axis=0)
        bsb = jnp.repeat(b_s[l].T, HEAD_W, axis=1)
        hp = _mix_out(hp, att, ugc, vn, w_s[l], bsb, w_out_b[l], fg, final)

        ws8 = w_s[l][:, :dec_seq, :dec_seq]
        ws_bd = (eye[None, :, None, :, None] * ws8[:, None, :, None, :]).reshape(N_HEADS, CHUNK, CHUNK)
        bsb = jnp.tile(jnp.repeat(b_s[l][:, :dec_seq].T, HEAD_W, axis=1), (CHUNK // dec_seq, 1))
        hs = _mix_out(hs, atts, ugcs, vns, ws_bd, bsb, w_out_b[l], fg, final)

    shape_p = (depth, batch, seq, N_HEADS, HEAD_W)
    shape_s = (depth, dec_batch, dec_seq, N_HEADS, HEAD_W)
    return (hp.reshape(batch, seq, D_MODEL),
            hs.reshape(dec_batch, dec_seq, D_MODEL),
            prompt_bufs[0].reshape(shape_p),
            prompt_bufs[1].reshape(shape_p),
            sample_bufs[0].reshape(shape_s),
            sample_bufs[1].reshape(shape_s),
            sample_bufs[2].reshape(shape_s))
```

```python
import functools
import math

import numpy as np
import jax
import jax.numpy as jnp
from jax import lax
from jax.experimental import pallas as pl
from jax.experimental.pallas import tpu as pltpu

F32 = jnp.float32
BF16 = jnp.bfloat16

D_MODEL = 1024
N_HEADS = 4
DK = 64
HEAD_W = 128
D_GRP = N_HEADS * HEAD_W
N_GRP = 7
CHUNK = 128
PAGE = 128
EPS = 1e-6
NEG = -1e30
ONES_ROWS = 16
POS_LANES = 128
SLOPE_PARTS = 3
LOG2E = np.float32(math.log2(math.e))

ROW_TILE = 512
ATT_BLK = 256
NEW_ROWS = 128
VMEM_LIMIT = 56 * 1024 * 1024


def _lambda_init(layer):
    return 0.8 - 0.6 * math.exp(-0.3 * layer)


def _alibi_slope(h):
    return 2.0 ** (-8.0 * (h + 1) / N_HEADS)


def _slope_parts(h):
    target = np.float32(_alibi_slope(h)) * LOG2E
    rest, parts = target, []
    for _ in range(SLOPE_PARTS):
        part = np.asarray(rest, dtype=BF16).astype(np.float32)
        parts.append(float(part))
        rest = np.float32(rest - part)
    assert rest == 0.0
    return parts


def _lam_value(lq1, lk1, lq2, lk2, lam_init):
    a = jnp.sum(lq1[...] * lk1[...], axis=-1, keepdims=True)
    b = jnp.sum(lq2[...] * lk2[...], axis=-1, keepdims=True)
    return jnp.exp(a) - jnp.exp(b) + lam_init


def _silu(x):
    return x * jax.nn.sigmoid(x)


def _head_rows(h, n_rows, first_row=0):
    return pl.ds(first_row * N_HEADS + h, n_rows, stride=N_HEADS)


def _inproj_stages(x_ref, g_ref, w_ref, lng_ref, lnb_ref, outs, with_vn_out):
    q_ref, k_ref, v_ref, sga_ref, ugc_ref, vn_ref = outs[:6]
    tm = x_ref.shape[0]

    normed = []

    def proj(i):
        if not normed:
            x = x_ref[...]
            ms = jnp.mean(x * x, axis=-1, keepdims=True)
            normed.append((x * lax.rsqrt(ms + EPS) * g_ref[...]).astype(BF16))
        return jnp.dot(normed[0], w_ref[:, i * D_GRP:(i + 1) * D_GRP], preferred_element_type=F32)

    def store_heads(ref, c, val):
        if len(ref.shape) == 3:
            for d in range(ref.shape[0]):
                ref[d, _head_rows(c, tm), :] = val
        else:
            ref[_head_rows(c, tm), :] = val

    def store_all_heads(ref, val):
        for c in range(N_HEADS):
            store_heads(ref, c, val[:, c * HEAD_W:(c + 1) * HEAD_W])

    def q_stage():
        q_ref[...] = (proj(0) * float(np.float32(DK ** -0.5) * LOG2E)).astype(q_ref.dtype)

    def k_stage():
        store_all_heads(k_ref, proj(1))

    def v_stage():
        store_all_heads(v_ref, proj(2))

    def gate_stage():
        sga_ref[...] = _silu(proj(3)).astype(sga_ref.dtype)

    def ugc_stage():
        ugc_ref[...] = (proj(4) * _silu(proj(6))).astype(ugc_ref.dtype)

    def vn_stage():
        vc = proj(5)
        for c in range(N_HEADS):
            sl = slice(c * HEAD_W, (c + 1) * HEAD_W)
            blk = vc[:, sl]
            mu = jnp.mean(blk, axis=-1, keepdims=True)
            xc = blk - mu
            var = jnp.mean(xc * xc, axis=-1, keepdims=True)
            vn = xc * lax.rsqrt(var + EPS) * lng_ref[:, sl] + lnb_ref[:, sl]
            vn_ref[:, sl] = vn.astype(vn_ref.dtype)
            if with_vn_out:
                store_heads(outs[6], c, vn)

    return [q_stage, k_stage, v_stage, gate_stage, ugc_stage, vn_stage]


def _inproj_kernel(*refs, n_alias, with_vn_out):
    outs = refs[5 + n_alias:]
    for stage in _inproj_stages(*refs[:5], outs, with_vn_out):
        stage()


def _inproj_out_specs(m, tm, act_dtype, layer, depth, first, with_vn_out, index):
    row = pl.BlockSpec((tm, D_GRP), lambda *i: (index(*i), 0))
    act = jax.ShapeDtypeStruct((m, D_GRP), act_dtype)
    stacked = jax.ShapeDtypeStruct((depth, m * N_HEADS, HEAD_W), F32)
    if first:
        stacked_spec = pl.BlockSpec((depth, tm * N_HEADS, HEAD_W), lambda *i: (0, index(*i), 0))
    else:
        stacked_spec = pl.BlockSpec((None, tm * N_HEADS, HEAD_W), lambda *i: (layer, index(*i), 0))
    extra = 1 if with_vn_out else 0
    out_shape = [act, stacked, stacked, act, act, act] + [stacked] * extra
    out_specs = [row, stacked_spec, stacked_spec, row, row, row] + [stacked_spec] * extra
    return out_shape, out_specs, [1, 2] + [6] * extra


def _inproj(x, g, w, lng, lnb, act_dtype, layer, depth, prev, with_vn_out):
    m = x.shape[0]
    tm = ROW_TILE
    const = lambda shape: pl.BlockSpec(shape, lambda i: (0, 0))
    out_shape, out_specs, stacked_idx = _inproj_out_specs(
        m, tm, act_dtype, layer, depth, prev is None, with_vn_out, lambda i: i)
    prev = list(prev) if prev is not None else []
    aliases = {5 + n: stacked_idx[n] for n in range(len(prev))}
    return pl.pallas_call(
        functools.partial(_inproj_kernel, n_alias=len(prev), with_vn_out=with_vn_out),
        grid=(m // tm,),
        in_specs=[pl.BlockSpec((tm, D_MODEL), lambda i: (i, 0)), const((1, D_MODEL)),
                  const((D_MODEL, N_GRP * D_GRP)), const((1, D_GRP)), const((1, D_GRP))]
                 + [pl.BlockSpec(memory_space=pl.ANY)] * len(prev),
        out_specs=out_specs,
        out_shape=out_shape,
        input_output_aliases=aliases,
        compiler_params=pltpu.CompilerParams(dimension_semantics=("parallel",),
                                             vmem_limit_bytes=VMEM_LIMIT),
        name="inproj",
    )(x, g, w, lng, lnb, *prev)


def _prompt_attn_kernel(q_ref, k_ref, v_ref, sga_ref, sg_ref, lq1, lk1, lq2, lk2,
                        o_ref, kb_sc, vt_sc, *stat_sc, lam_init, tb, nblk):
    m_sc = stat_sc[:N_HEADS]
    acc_sc = stat_sc[N_HEADS:]
    i = pl.program_id(1)

    @pl.when(i == 0)
    def _():
        pos = lax.broadcasted_iota(jnp.int32, (tb, POS_LANES), 0)
        lane = lax.broadcasted_iota(jnp.int32, (tb, POS_LANES), 1)
        for jb in range(nblk):
            kpos = pos + jb * tb
            hi = (kpos // PAGE) * PAGE
            aug = jnp.where(lane >= 2 * SLOPE_PARTS, 0, jnp.where(lane % 2 == 0, hi, kpos - hi))
            aug = aug.astype(F32).astype(BF16)
            for h in range(N_HEADS):
                rows = _head_rows(h, tb, jb * tb)
                kb_sc[h, jb * tb:(jb + 1) * tb, 0:HEAD_W] = k_ref[rows, :].astype(BF16)
                kb_sc[h, jb * tb:(jb + 1) * tb, HEAD_W:HEAD_W + POS_LANES] = aug
                vt_sc[h, jb, 0:HEAD_W, :] = v_ref[rows, :].T.astype(BF16)
                vt_sc[h, jb, HEAD_W:HEAD_W + ONES_ROWS, :] = jnp.ones((ONES_ROWS, tb), BF16)

    lane = lax.broadcasted_iota(jnp.int32, (tb, HEAD_W), 1)
    qz = []
    for h in range(N_HEADS):
        q = q_ref[:, h * HEAD_W:(h + 1) * HEAD_W]
        zero = jnp.zeros_like(q)
        slope = jnp.zeros((tb, POS_LANES), F32)
        for n, part in enumerate(_slope_parts(h)):
            slope = jnp.where(lane // 2 == n, part, slope)
        slope = slope.astype(BF16)
        q2 = jnp.concatenate([jnp.where(lane < DK, q, zero), jnp.where(lane >= DK, q, zero)], axis=0)
        qz.append(jnp.concatenate([q2, jnp.concatenate([slope, slope], axis=0)], axis=1))
        m_sc[h][...] = jnp.full(m_sc[h].shape, NEG, F32)
        acc_sc[h][...] = jnp.zeros(acc_sc[h].shape, F32)

    def steps(first_blk, nkb, masked):
        heads = range(N_HEADS)
        rows = pl.ds(pl.multiple_of(first_blk * tb, tb), nkb * tb)
        s = [lax.dot_general(kb_sc[h, rows, :], qz[h], (((1,), (1,)), ((), ())),
                             preferred_element_type=F32) for h in heads]
        if masked:
            key_i = lax.broadcasted_iota(jnp.int32, (nkb * tb, 2 * tb), 0)
            qry_i = lax.broadcasted_iota(jnp.int32, (nkb * tb, 2 * tb), 1)
            qry_i = jnp.where(qry_i >= tb, qry_i - tb, qry_i) + (nkb - 1) * tb
            s = [jnp.where(key_i <= qry_i, sh, NEG) for sh in s]
        m_old = [m_sc[h][...] for h in heads]
        m_new = [jnp.maximum(m_old[h], jnp.max(s[h], axis=0, keepdims=True)) for h in heads]
        p = [jnp.exp2(s[h] - m_new[h]).astype(BF16) for h in heads]
        for h in heads:
            pv = jnp.dot(vt_sc[h, first_blk], p[h][0:tb, :], preferred_element_type=F32)
            for kb in range(1, nkb):
                pv = pv + jnp.dot(vt_sc[h, first_blk + kb], p[h][kb * tb:(kb + 1) * tb, :],
                                  preferred_element_type=F32)
            acc_sc[h][...] = acc_sc[h][...] * jnp.exp2(m_old[h] - m_new[h]) + pv
            m_sc[h][...] = m_new[h]

    def body(jj, carry):
        steps(2 * jj, 2, False)
        return carry

    lax.fori_loop(0, i // 2, body, 0)

    @pl.when(i % 2 == 1)
    def _():
        steps(i - 1, 2, True)

    @pl.when(i % 2 == 0)
    def _():
        steps(i, 1, True)

    lam = _lam_value(lq1, lk1, lq2, lk2, lam_init)
    for h in range(N_HEADS):
        sl = slice(h * HEAD_W, (h + 1) * HEAD_W)
        acc = acc_sc[h]
        inv = 1.0 / acc[HEAD_W:HEAD_W + 1, :]
        o1 = acc[0:HEAD_W, 0:tb] * inv[:, 0:tb]
        o2 = acc[0:HEAD_W, tb:2 * tb] * inv[:, tb:2 * tb]
        o = (o1 - lam * o2).T
        ms = jnp.mean(o * o, axis=-1, keepdims=True)
        att = o * lax.rsqrt(ms + EPS) * sg_ref[...] * (1.0 - lam_init)
        o_ref[:, sl] = (att * sga_ref[:, sl].astype(F32)).astype(o_ref.dtype)


def _prompt_attn(q, kbuf, vbuf, layer, sga, sg, lams, lam_init, batch, seq):
    tb = ATT_BLK
    nblk = seq // tb
    qspec = pl.BlockSpec((tb, D_GRP), lambda b, i: (b * nblk + i, 0))
    kvspec = pl.BlockSpec((None, seq * N_HEADS, HEAD_W), lambda b, i: (layer, b, 0))
    const = lambda shape: pl.BlockSpec(shape, lambda b, i: (0, 0))
    stats = ([pltpu.VMEM((1, 2 * tb), F32)] * N_HEADS
             + [pltpu.VMEM((HEAD_W + ONES_ROWS, 2 * tb), F32)] * N_HEADS)
    return pl.pallas_call(
        functools.partial(_prompt_attn_kernel, lam_init=lam_init, tb=tb, nblk=nblk),
        grid=(batch, nblk),
        in_specs=[qspec, kvspec, kvspec, qspec, const((1, HEAD_W))] + [const((1, DK))] * 4,
        out_specs=qspec,
        out_shape=jax.ShapeDtypeStruct(q.shape, BF16),
        scratch_shapes=[pltpu.VMEM((N_HEADS, seq, HEAD_W + POS_LANES), BF16),
                        pltpu.VMEM((N_HEADS, nblk, HEAD_W + ONES_ROWS, tb), BF16)] + stats,
        compiler_params=pltpu.CompilerParams(
            dimension_semantics=("parallel", "arbitrary"),
            vmem_limit_bytes=VMEM_LIMIT),
        name="prompt_attn",
    )(q, kbuf, vbuf, sga, sg, *lams)


def _sample_bias(n_pages, dec_seq):
    past = n_pages * PAGE
    n_pos = past + NEW_ROWS // N_HEADS
    t = np.arange(dec_seq)[:, None]
    kpos = np.arange(n_pos)[None, :]
    visible = (kpos < past) | (kpos - past <= t)
    bias = np.full((2, N_HEADS, dec_seq, n_pos, N_HEADS), NEG, np.float32)
    for h in range(N_HEADS):
        alibi = -np.float32(_alibi_slope(h)) * (past + t - kpos).astype(np.float32) * LOG2E
        bias[:, h, :, :, h] = np.where(visible, alibi, NEG)
    return bias.reshape(2 * N_HEADS * dec_seq, n_pos * N_HEADS)


def _paged_attention(pt_ref, step, n_steps, first_seq, q_ref, kn_ref, vn_ref, sga_ref, bias_ref, sg_ref,
                     lq1, lk1, lq2, lk2, ck_hbm, cv_hbm, o_ref, kbuf, vbuf, ksem, vsem,
                     *, layer, lam_init, n_pages, dec_seq, fillers):
    n_seq = len(fillers)
    n_rows = 2 * N_HEADS * dec_seq
    half = N_HEADS * dec_seq
    page_rows = PAGE * N_HEADS
    past_rows = n_pages * page_rows
    new_rows = dec_seq * N_HEADS
    slots = range(n_seq)

    def page_copies(seq_idx, slot):
        copies = []
        for j in range(n_pages):
            page = pt_ref[seq_idx * n_pages + j]
            copies.append(pltpu.make_async_copy(ck_hbm.at[layer, page], kbuf.at[slot, j], ksem.at[slot]))
            copies.append(pltpu.make_async_copy(cv_hbm.at[layer, page], vbuf.at[slot, j], vsem.at[slot]))
        return copies

    @pl.when(step == 0)
    def _():
        for slot in slots:
            for cp in page_copies(first_seq + slot, slot):
                cp.start()

    lane = lax.broadcasted_iota(jnp.int32, (half, HEAD_W), 1)
    lam = _lam_value(lq1, lk1, lq2, lk2, lam_init)
    pad = jnp.zeros((NEW_ROWS - new_rows, HEAD_W), F32)
    nt = (((1,), (1,)), ((), ()))

    def attend(slot, seq_idx):
        rows = slice(slot * dec_seq, (slot + 1) * dec_seq)
        new = slice(slot * new_rows, (slot + 1) * new_rows)
        k_new = jnp.concatenate([kn_ref[new, :], pad], axis=0)
        v_new = jnp.concatenate([vn_ref[new, :], pad], axis=0)
        qh = jnp.concatenate([q_ref[rows, h * HEAD_W:(h + 1) * HEAD_W] for h in range(N_HEADS)], axis=0)
        qrows = jnp.concatenate([jnp.where(lane < DK, qh, 0.0), jnp.where(lane >= DK, qh, 0.0)], axis=0)
        s = [lax.dot_general(qrows, kbuf[slot, j], nt, preferred_element_type=F32) for j in range(n_pages)]
        s.append(lax.dot_general(qrows, k_new, nt, preferred_element_type=F32))
        s = jnp.concatenate(s, axis=1) + bias_ref[...]
        yield
        p = jnp.exp2(s - jnp.max(s, axis=-1, keepdims=True))
        inv = 1.0 / jnp.sum(p, axis=-1, keepdims=True)
        yield
        pv = jnp.dot(p[:, past_rows:], v_new, preferred_element_type=F32)
        for j in range(n_pages):
            pv = pv + jnp.dot(p[:, j * page_rows:(j + 1) * page_rows], vbuf[slot, j],
                              preferred_element_type=F32)
        res = pv[0:half, :] * inv[0:half, :] - pv[half:n_rows, :] * (lam * inv[half:n_rows, :])

        @pl.when(step + 1 < n_steps)
        def _():
            for cp in page_copies(seq_idx + n_seq, slot):
                cp.start()

        yield
        for h in range(N_HEADS):
            sl = slice(h * HEAD_W, (h + 1) * HEAD_W)
            o = res[h * dec_seq:(h + 1) * dec_seq, :]
            ms = jnp.mean(o * o, axis=-1, keepdims=True)
            att = o * lax.rsqrt(ms + EPS) * sg_ref[...] * (1.0 - lam_init)
            o_ref[rows, sl] = (att * sga_ref[rows, sl]).astype(o_ref.dtype)

    for slot in slots:
        seq_idx = first_seq + n_seq * step + slot
        for cp in page_copies(seq_idx, slot):
            cp.wait()
        pending = list(fillers[slot])
        for _ in attend(slot, seq_idx):
            if pending:
                pending.pop(0)()
        for filler in pending:
            filler()


N_PAGED_IN = 12


def _paged_operands(qs, knbuf, vnbuf, sgas, cache_k, cache_v, sg, lams, layer, dec_seq, n_pages,
                    first_seq, n_seq, n_steps, step_of):
    bias = jnp.asarray(_sample_bias(n_pages, dec_seq))
    first_blk = first_seq // n_seq
    const = lambda shape, **kw: pl.BlockSpec(shape, lambda *i: (0, 0), **kw)
    seqspec = pl.BlockSpec((n_seq * dec_seq, D_GRP), lambda *i: (first_blk + step_of(*i), 0))
    newspec = pl.BlockSpec((None, n_seq * dec_seq * N_HEADS, HEAD_W),
                           lambda *i: (layer, first_blk + step_of(*i), 0))
    hbm = pl.BlockSpec(memory_space=pl.ANY)
    operands = [qs, knbuf, vnbuf, sgas, bias, sg, *lams, cache_k, cache_v]
    in_specs = ([seqspec, newspec, newspec, seqspec, const(bias.shape, pipeline_mode=pl.Buffered(1)),
                 const((1, HEAD_W))] + [const((1, DK))] * 4 + [hbm, hbm])
    assert len(operands) == len(in_specs) == N_PAGED_IN
    out_spec = pl.BlockSpec((n_seq * dec_seq, D_GRP), lambda *i: (step_of(*i), 0))
    out_shape = jax.ShapeDtypeStruct((n_steps * n_seq * dec_seq, D_GRP), F32)
    page_buf = pltpu.VMEM((n_seq, n_pages, PAGE * N_HEADS, HEAD_W), F32)
    scratch = [page_buf, page_buf, pltpu.SemaphoreType.DMA((n_seq,)), pltpu.SemaphoreType.DMA((n_seq,))]
    return operands, in_specs, out_spec, out_shape, scratch


def _inproj_decode_kernel(pt_ref, x_ref, g_ref, w_ref, lng_ref, lnb_ref, *rest,
                          n_alias, n_seq, first_seq, layer, lam_init, n_pages, dec_seq):
    paged_in, rest = rest[:N_PAGED_IN], rest[N_PAGED_IN + n_alias:]
    outs, att_ref, scratch = rest[:6], rest[6], rest[7:]
    stages = _inproj_stages(x_ref, g_ref, w_ref, lng_ref, lnb_ref, outs, False)
    per_slot = -(-len(stages) // n_seq)
    stages = stages[:per_slot] + stages[per_slot:][::-1]
    fillers = [stages[slot * per_slot:(slot + 1) * per_slot] for slot in range(n_seq)]
    _paged_attention(pt_ref, pl.program_id(0), pl.num_programs(0), first_seq, *paged_in, att_ref, *scratch,
                     layer=layer, lam_init=lam_init, n_pages=n_pages, dec_seq=dec_seq, fillers=fillers)


def _inproj_decode(x, g, w, lng, lnb, layer, depth, prev, pt, paged, first_seq, n_seq, dec_seq, n_pages,
                   lam_init):
    m = x.shape[0]
    tm = ROW_TILE // 2
    n_steps = m // tm
    const = lambda shape, **kw: pl.BlockSpec(shape, lambda t, pt: (0, 0), **kw)
    once = dict(pipeline_mode=pl.Buffered(1))
    p_operands, p_specs, p_out_spec, p_out_shape, p_scratch = _paged_operands(
        *paged, layer, dec_seq, n_pages, first_seq, n_seq, n_steps, lambda t, pt: t)
    out_shape, out_specs, stacked_idx = _inproj_out_specs(
        m, tm, BF16, layer, depth, prev is None, False, lambda t, pt: t)
    prev = list(prev) if prev is not None else []
    n_inputs = 6 + N_PAGED_IN
    aliases = {n_inputs + i: stacked_idx[i] for i in range(len(prev))}
    grid_spec = pltpu.PrefetchScalarGridSpec(
        num_scalar_prefetch=1,
        grid=(n_steps,),
        in_specs=[pl.BlockSpec((tm, D_MODEL), lambda t, pt: (t, 0)), const((1, D_MODEL)),
                  const((D_MODEL, N_GRP * D_GRP), **once), const((1, D_GRP)), const((1, D_GRP))]
                 + p_specs + [pl.BlockSpec(memory_space=pl.ANY)] * len(prev),
        out_specs=out_specs + [p_out_spec],
        scratch_shapes=p_scratch,
    )
    return pl.pallas_call(
        functools.partial(_inproj_decode_kernel, n_alias=len(prev), n_seq=n_seq, first_seq=first_seq,
                          layer=layer, lam_init=lam_init, n_pages=n_pages, dec_seq=dec_seq),
        grid_spec=grid_spec,
        out_shape=out_shape + [p_out_shape],
        input_output_aliases=aliases,
        compiler_params=pltpu.CompilerParams(dimension_semantics=("arbitrary",),
                                             vmem_limit_bytes=VMEM_LIMIT),
        name="inproj_decode",
    )(pt, x, g, w, lng, lnb, *p_operands, *prev)


def _mix_out_kernel(x_ref, att_ref, ugc_ref, vn_ref, ws_ref, bsb_ref, wout_ref, fg_ref,
                    o_ref, cat_sc, *, final):
    tm = x_ref.shape[0]
    nc = tm // CHUNK
    r = lax.broadcasted_iota(jnp.int32, (CHUNK, CHUNK), 0)
    c = lax.broadcasted_iota(jnp.int32, (CHUNK, CHUNK), 1)
    cat_sc[:, 0:D_GRP] = att_ref[...].astype(BF16)
    for h in range(N_HEADS):
        sl = slice(h * HEAD_W, (h + 1) * HEAD_W)
        ws = jnp.where(c <= r, ws_ref[h], 0.0).astype(BF16)
        vn = jnp.concatenate([vn_ref[ci * CHUNK:(ci + 1) * CHUNK, sl].astype(BF16)
                              for ci in range(nc)], axis=1)
        mixed = jnp.dot(ws, vn, preferred_element_type=F32)
        for ci in range(nc):
            rows = slice(ci * CHUNK, (ci + 1) * CHUNK)
            mc = mixed[:, ci * HEAD_W:(ci + 1) * HEAD_W] + bsb_ref[:, sl]
            cat_sc[rows, D_GRP + h * HEAD_W:D_GRP + (h + 1) * HEAD_W] = (
                ugc_ref[rows, sl].astype(F32) * mc).astype(BF16)
    y = x_ref[...] + jnp.dot(cat_sc[...], wout_ref[...], preferred_element_type=F32)
    if final:
        ms = jnp.mean(y * y, axis=-1, keepdims=True)
        y = y * lax.rsqrt(ms + EPS) * fg_ref[...]
    o_ref[...] = y


def _mix_out(x, att, ugc, vn, ws, bsb, wout, fg, final):
    m = x.shape[0]
    tm = ROW_TILE
    row = lambda width: pl.BlockSpec((tm, width), lambda i: (i, 0))
    return pl.pallas_call(
        functools.partial(_mix_out_kernel, final=final),
        grid=(m // tm,),
        in_specs=[row(D_MODEL), row(D_GRP), row(D_GRP), row(D_GRP),
                  pl.BlockSpec((N_HEADS, CHUNK, CHUNK), lambda i: (0, 0, 0)),
                  pl.BlockSpec((CHUNK, D_GRP), lambda i: (0, 0)),
                  pl.BlockSpec((2 * D_GRP, D_MODEL), lambda i: (0, 0)),
                  pl.BlockSpec((1, D_MODEL), lambda i: (0, 0))],
        out_specs=row(D_MODEL),
        out_shape=jax.ShapeDtypeStruct((m, D_MODEL), F32),
        scratch_shapes=[pltpu.VMEM((tm, 2 * D_GRP), BF16)],
        compiler_params=pltpu.CompilerParams(dimension_semantics=("parallel",),
                                             vmem_limit_bytes=VMEM_LIMIT),
        name="mix_out",
    )(x, att, ugc, vn, ws, bsb, wout, fg)


def kernel(x_prompt, x_sample, cache_k, cache_v, page_table, norm_g, w_in, lam_q1, lam_k1,
           lam_q2, lam_k2, subln_g, ln_v_g, ln_v_b, w_s, b_s, w_out, final_g):
    batch, seq, _ = x_prompt.shape
    dec_batch, dec_seq, _ = x_sample.shape
    depth, n_pool = cache_k.shape[0], cache_k.shape[1]
    n_pages = page_table.shape[1]
    assert seq % ATT_BLK == 0 and (batch * seq) % ROW_TILE == 0 and (dec_batch * dec_seq) % ROW_TILE == 0
    assert CHUNK % dec_seq == 0 and dec_seq % 8 == 0 and ATT_BLK % PAGE == 0
    assert dec_seq * N_HEADS <= NEW_ROWS
    proj_steps = (batch * seq) // (ROW_TILE // 2)
    assert dec_batch % proj_steps == 0
    n_seq_proj = dec_batch // proj_steps

    hp = x_prompt.reshape(batch * seq, D_MODEL)
    hs = x_sample.reshape(dec_batch * dec_seq, D_MODEL)
    ck = cache_k.reshape(depth, n_pool, PAGE * N_HEADS, HEAD_W)
    cv = cache_v.reshape(depth, n_pool, PAGE * N_HEADS, HEAD_W)
    pt = page_table.reshape(-1)
    w_in_b = w_in.astype(BF16)
    w_out_b = w_out.astype(BF16)
    fg = final_g.reshape(1, D_MODEL)
    eye = jnp.eye(CHUNK // dec_seq, dtype=F32)

    prompt_bufs, sample_bufs = None, None
    for l in range(depth):
        lam_init = _lambda_init(l)
        g = norm_g[l].reshape(1, D_MODEL)
        lng = ln_v_g[l].reshape(1, D_GRP)
        lnb = ln_v_b[l].reshape(1, D_GRP)
        sg = subln_g[l].reshape(1, HEAD_W)
        lams = [a[l].reshape(1, DK) for a in (lam_q1, lam_k1, lam_q2, lam_k2)]
        final = l == depth - 1

        qs, ksbuf, vsbuf, sgas, ugcs, vns, vnsbuf = _inproj(hs, g, w_in_b[l], lng, lnb, F32, l, depth,
                                                            sample_bufs, True)
        sample_bufs = (ksbuf, vsbuf, vnsbuf)
        paged = (qs, ksbuf, vsbuf, sgas, ck, cv, sg, lams)
        q, kbuf, vbuf, sga, ugc, vn, atts = _inproj_decode(
            hp, g, w_in_b[l], lng, lnb, l, depth, prompt_bufs,
            pt, paged, 0, n_seq_proj, dec_seq, n_pages, lam_init)
        prompt_bufs = (kbuf, vbuf)

        att = _prompt_attn(q, kbuf, vbuf, l, sga, sg, lams, lam_init, batch, seq)
        bsb = jnp.repeat(b_s[l].T, HEAD_W, axis=1)
        hp = _mix_out(hp, att, ugc, vn, w_s[l], bsb, w_out_b[l], fg, final)

        ws8 = w_s[l][:, :dec_seq, :dec_seq]
        ws_bd = (eye[None, :, None, :, None] * ws8[:, None, :, None, :]).reshape(N_HEADS, CHUNK, CHUNK)
        bsb = jnp.tile(jnp.repeat(b_s[l][:, :dec_seq].T, HEAD_W, axis=1), (CHUNK // dec_seq, 1))
        hs = _mix_out(hs, atts, ugcs, vns, ws_bd, bsb, w_out_b[l], fg, final)

    shape_p = (depth, batch, seq, N_HEADS, HEAD_W)
    shape_s = (depth, dec_batch, dec_seq, N_HEADS, HEAD_W)
    return (hp.reshape(batch, seq, D_MODEL),
            hs.reshape(dec_batch, dec_seq, D_MODEL),
            prompt_bufs[0].reshape(shape_p),
            prompt_bufs[1].reshape(shape_p),
            sample_bufs[0].reshape(shape_s),
            sample_bufs[1].reshape(shape_s),
            sample_bufs[2].reshape(shape_s))
```

```python
import functools
import math

import numpy as np
import jax
import jax.numpy as jnp
from jax import lax
from jax.experimental import pallas as pl
from jax.experimental.pallas import tpu as pltpu

F32 = jnp.float32
BF16 = jnp.bfloat16

D_MODEL = 1024
N_HEADS = 4
DK = 64
HEAD_W = 128
D_GRP = N_HEADS * HEAD_W
N_GRP = 7
CHUNK = 128
PAGE = 128
EPS = 1e-6
NEG = -1e30
ONES_ROWS = 16
POS_LANES = 128
SLOPE_PARTS = 3
LOG2E = np.float32(math.log2(math.e))

ROW_TILE = 512
ATT_BLK = 256
NEW_ROWS = 128
VMEM_LIMIT = 56 * 1024 * 1024


def _lambda_init(layer):
    return 0.8 - 0.6 * math.exp(-0.3 * layer)


def _alibi_slope(h):
    return 2.0 ** (-8.0 * (h + 1) / N_HEADS)


def _slope_parts(h):
    target = np.float32(_alibi_slope(h)) * LOG2E
    rest, parts = target, []
    for _ in range(SLOPE_PARTS):
        part = np.asarray(rest, dtype=BF16).astype(np.float32)
        parts.append(float(part))
        rest = np.float32(rest - part)
    assert rest == 0.0
    return parts


def _lam_value(lq1, lk1, lq2, lk2, lam_init):
    a = jnp.sum(lq1[...] * lk1[...], axis=-1, keepdims=True)
    b = jnp.sum(lq2[...] * lk2[...], axis=-1, keepdims=True)
    return jnp.exp(a) - jnp.exp(b) + lam_init


def _silu(x):
    return x * jax.nn.sigmoid(x)


def _head_rows(h, n_rows, first_row=0):
    return pl.ds(first_row * N_HEADS + h, n_rows, stride=N_HEADS)


def _inproj_stages(x_ref, g_ref, w_ref, lng_ref, lnb_ref, outs, with_vn_out):
    q_ref, k_ref, v_ref, sga_ref, ugc_ref, vn_ref = outs[:6]
    tm = x_ref.shape[0]

    normed = []

    def proj(i):
        if not normed:
            x = x_ref[...]
            ms = jnp.mean(x * x, axis=-1, keepdims=True)
            normed.append((x * lax.rsqrt(ms + EPS) * g_ref[...]).astype(BF16))
        return jnp.dot(normed[0], w_ref[:, i * D_GRP:(i + 1) * D_GRP], preferred_element_type=F32)

    def store_heads(ref, c, val):
        if len(ref.shape) == 3:
            for d in range(ref.shape[0]):
                ref[d, _head_rows(c, tm), :] = val
        else:
            ref[_head_rows(c, tm), :] = val

    def store_all_heads(ref, val):
        for c in range(N_HEADS):
            store_heads(ref, c, val[:, c * HEAD_W:(c + 1) * HEAD_W])

    def q_stage():
        q_ref[...] = (proj(0) * float(np.float32(DK ** -0.5) * LOG2E)).astype(q_ref.dtype)

    def k_stage():
        store_all_heads(k_ref, proj(1))

    def v_stage():
        store_all_heads(v_ref, proj(2))

    def gate_stage():
        sga_ref[...] = _silu(proj(3)).astype(sga_ref.dtype)

    def ugc_stage():
        ugc_ref[...] = (proj(4) * _silu(proj(6))).astype(ugc_ref.dtype)

    def vn_stage():
        vc = proj(5)
        for c in range(N_HEADS):
            sl = slice(c * HEAD_W, (c + 1) * HEAD_W)
            blk = vc[:, sl]
            mu = jnp.mean(blk, axis=-1, keepdims=True)
            xc = blk - mu
            var = jnp.mean(xc * xc, axis=-1, keepdims=True)
            vn = xc * lax.rsqrt(var + EPS) * lng_ref[:, sl] + lnb_ref[:, sl]
            vn_ref[:, sl] = vn.astype(vn_ref.dtype)
            if with_vn_out:
                store_heads(outs[6], c, vn)

    return [q_stage, k_stage, v_stage, gate_stage, ugc_stage, vn_stage]


def _inproj_kernel(*refs, n_alias, with_vn_out):
    outs = refs[5 + n_alias:]
    for stage in _inproj_stages(*refs[:5], outs, with_vn_out):
        stage()


def _inproj_out_specs(m, tm, act_dtype, layer, depth, first, with_vn_out, index):
    row = pl.BlockSpec((tm, D_GRP), lambda *i: (index(*i), 0))
    act = jax.ShapeDtypeStruct((m, D_GRP), act_dtype)
    stacked = jax.ShapeDtypeStruct((depth, m * N_HEADS, HEAD_W), F32)
    if first:
        stacked_spec = pl.BlockSpec((depth, tm * N_HEADS, HEAD_W), lambda *i: (0, index(*i), 0))
    else:
        stacked_spec = pl.BlockSpec((None, tm * N_HEADS, HEAD_W), lambda *i: (layer, index(*i), 0))
    extra = 1 if with_vn_out else 0
    out_shape = [act, stacked, stacked, act, act, act] + [stacked] * extra
    out_specs = [row, stacked_spec, stacked_spec, row, row, row] + [stacked_spec] * extra
    return out_shape, out_specs, [1, 2] + [6] * extra


def _inproj(x, g, w, lng, lnb, act_dtype, layer, depth, prev, with_vn_out):
    m = x.shape[0]
    tm = ROW_TILE
    const = lambda shape: pl.BlockSpec(shape, lambda i: (0, 0))
    out_shape, out_specs, stacked_idx = _inproj_out_specs(
        m, tm, act_dtype, layer, depth, prev is None, with_vn_out, lambda i: i)
    prev = list(prev) if prev is not None else []
    aliases = {5 + n: stacked_idx[n] for n in range(len(prev))}
    return pl.pallas_call(
        functools.partial(_inproj_kernel, n_alias=len(prev), with_vn_out=with_vn_out),
        grid=(m // tm,),
        in_specs=[pl.BlockSpec((tm, D_MODEL), lambda i: (i, 0)), const((1, D_MODEL)),
                  pl.BlockSpec((None, D_MODEL, N_GRP * D_GRP), lambda i: (layer, 0, 0)),
                  const((1, D_GRP)), const((1, D_GRP))]
                 + [pl.BlockSpec(memory_space=pl.ANY)] * len(prev),
        out_specs=out_specs,
        out_shape=out_shape,
        input_output_aliases=aliases,
        compiler_params=pltpu.CompilerParams(dimension_semantics=("parallel",),
                                             vmem_limit_bytes=VMEM_LIMIT),
        name="inproj",
    )(x, g, w, lng, lnb, *prev)


def _prompt_attn_kernel(q_ref, k_ref, v_ref, sga_ref, sg_ref, lq1, lk1, lq2, lk2,
                        x_ref, ugc_ref, vn_ref, ws_ref, bsb_ref, wout_ref, fg_ref,
                        o_ref, kb_sc, vt_sc, cat_sc, *stat_sc, lam_init, tb, nblk, final):
    m_sc = stat_sc[:N_HEADS]
    acc_sc = stat_sc[N_HEADS:]
    i = pl.program_id(1)

    @pl.when(i == 0)
    def _():
        pos = lax.broadcasted_iota(jnp.int32, (tb, POS_LANES), 0)
        lane = lax.broadcasted_iota(jnp.int32, (tb, POS_LANES), 1)
        for jb in range(nblk):
            kpos = pos + jb * tb
            hi = (kpos // PAGE) * PAGE
            aug = jnp.where(lane >= 2 * SLOPE_PARTS, 0, jnp.where(lane % 2 == 0, hi, kpos - hi))
            aug = aug.astype(F32).astype(BF16)
            for h in range(N_HEADS):
                rows = _head_rows(h, tb, jb * tb)
                kb_sc[h, jb * tb:(jb + 1) * tb, 0:HEAD_W] = k_ref[rows, :].astype(BF16)
                kb_sc[h, jb * tb:(jb + 1) * tb, HEAD_W:HEAD_W + POS_LANES] = aug
                vt_sc[h, jb, 0:HEAD_W, :] = v_ref[rows, :].T.astype(BF16)
                vt_sc[h, jb, HEAD_W:HEAD_W + ONES_ROWS, :] = jnp.ones((ONES_ROWS, tb), BF16)

    lane = lax.broadcasted_iota(jnp.int32, (tb, HEAD_W), 1)
    qz = []
    for h in range(N_HEADS):
        q = q_ref[:, h * HEAD_W:(h + 1) * HEAD_W]
        zero = jnp.zeros_like(q)
        slope = jnp.zeros((tb, POS_LANES), F32)
        for n, part in enumerate(_slope_parts(h)):
            slope = jnp.where(lane // 2 == n, part, slope)
        slope = slope.astype(BF16)
        q2 = jnp.concatenate([jnp.where(lane < DK, q, zero), jnp.where(lane >= DK, q, zero)], axis=0)
        qz.append(jnp.concatenate([q2, jnp.concatenate([slope, slope], axis=0)], axis=1))
        m_sc[h][...] = jnp.full(m_sc[h].shape, NEG, F32)
        acc_sc[h][...] = jnp.zeros(acc_sc[h].shape, F32)

    def steps(first_blk, nkb, masked):
        heads = range(N_HEADS)
        rows = pl.ds(pl.multiple_of(first_blk * tb, tb), nkb * tb)
        s = [lax.dot_general(kb_sc[h, rows, :], qz[h], (((1,), (1,)), ((), ())),
                             preferred_element_type=F32) for h in heads]
        if masked:
            key_i = lax.broadcasted_iota(jnp.int32, (nkb * tb, 2 * tb), 0)
            qry_i = lax.broadcasted_iota(jnp.int32, (nkb * tb, 2 * tb), 1)
            qry_i = jnp.where(qry_i >= tb, qry_i - tb, qry_i) + (nkb - 1) * tb
            s = [jnp.where(key_i <= qry_i, sh, NEG) for sh in s]
        m_old = [m_sc[h][...] for h in heads]
        m_new = [jnp.maximum(m_old[h], jnp.max(s[h], axis=0, keepdims=True)) for h in heads]
        p = [jnp.exp2(s[h] - m_new[h]).astype(BF16) for h in heads]
        for h in heads:
            pv = jnp.dot(vt_sc[h, first_blk], p[h][0:tb, :], preferred_element_type=F32)
            for kb in range(1, nkb):
                pv = pv + jnp.dot(vt_sc[h, first_blk + kb], p[h][kb * tb:(kb + 1) * tb, :],
                                  preferred_element_type=F32)
            acc_sc[h][...] = acc_sc[h][...] * jnp.exp2(m_old[h] - m_new[h]) + pv
            m_sc[h][...] = m_new[h]

    def body(jj, carry):
        steps(2 * jj, 2, False)
        return carry

    lax.fori_loop(0, i // 2, body, 0)

    @pl.when(i % 2 == 1)
    def _():
        steps(i - 1, 2, True)

    @pl.when(i % 2 == 0)
    def _():
        steps(i, 1, True)

    lam = _lam_value(lq1, lk1, lq2, lk2, lam_init)
    for h in range(N_HEADS):
        sl = slice(h * HEAD_W, (h + 1) * HEAD_W)
        acc = acc_sc[h]
        inv = 1.0 / acc[HEAD_W:HEAD_W + 1, :]
        o1 = acc[0:HEAD_W, 0:tb] * inv[:, 0:tb]
        o2 = acc[0:HEAD_W, tb:2 * tb] * inv[:, tb:2 * tb]
        o = (o1 - lam * o2).T
        ms = jnp.mean(o * o, axis=-1, keepdims=True)
        att = o * lax.rsqrt(ms + EPS) * sg_ref[...] * (1.0 - lam_init)
        cat_sc[:, sl] = (att * sga_ref[:, sl].astype(F32)).astype(BF16)

    _mix_out_compute(x_ref, ugc_ref, vn_ref, ws_ref, bsb_ref, wout_ref, fg_ref, o_ref, cat_sc, final)


def _prompt_attn_mix(q, kbuf, vbuf, layer, sga, sg, lams, lam_init, batch, seq,
                     x, ugc, vn, ws, bsb, wout, fg, final):
    tb = ATT_BLK
    nblk = seq // tb
    rowspec = lambda width: pl.BlockSpec((tb, width), lambda b, i: (b * nblk + i, 0))
    kvspec = pl.BlockSpec((None, seq * N_HEADS, HEAD_W), lambda b, i: (layer, b, 0))
    const = lambda shape, **kw: pl.BlockSpec(shape, lambda b, i: (0,) * len(shape), **kw)
    stats = ([pltpu.VMEM((1, 2 * tb), F32)] * N_HEADS
             + [pltpu.VMEM((HEAD_W + ONES_ROWS, 2 * tb), F32)] * N_HEADS)
    return pl.pallas_call(
        functools.partial(_prompt_attn_kernel, lam_init=lam_init, tb=tb, nblk=nblk, final=final),
        grid=(batch, nblk),
        in_specs=[rowspec(D_GRP), kvspec, kvspec, rowspec(D_GRP), const((1, HEAD_W))] + [const((1, DK))] * 4
                 + [rowspec(D_MODEL), rowspec(D_GRP), rowspec(D_GRP), const((N_HEADS, CHUNK, CHUNK)),
                    const((CHUNK, D_GRP)),
                    pl.BlockSpec((None, 2 * D_GRP, D_MODEL), lambda b, i: (layer, 0, 0),
                                 pipeline_mode=pl.Buffered(1)),
                    const((1, D_MODEL))],
        out_specs=rowspec(D_MODEL),
        out_shape=jax.ShapeDtypeStruct(x.shape, F32),
        scratch_shapes=[pltpu.VMEM((N_HEADS, seq, HEAD_W + POS_LANES), BF16),
                        pltpu.VMEM((N_HEADS, nblk, HEAD_W + ONES_ROWS, tb), BF16),
                        pltpu.VMEM((tb, 2 * D_GRP), BF16)] + stats,
        compiler_params=pltpu.CompilerParams(
            dimension_semantics=("parallel", "arbitrary"),
            vmem_limit_bytes=VMEM_LIMIT),
        name="prompt_attn_mix",
    )(q, kbuf, vbuf, sga, sg, *lams, x, ugc, vn, ws, bsb, wout, fg)


def _sample_bias(n_pages, dec_seq):
    past = n_pages * PAGE
    n_pos = past + NEW_ROWS // N_HEADS
    t = np.arange(dec_seq)[:, None]
    kpos = np.arange(n_pos)[None, :]
    visible = (kpos < past) | (kpos - past <= t)
    bias = np.full((2, N_HEADS, dec_seq, n_pos, N_HEADS), NEG, np.float32)
    for h in range(N_HEADS):
        alibi = -np.float32(_alibi_slope(h)) * (past + t - kpos).astype(np.float32) * LOG2E
        bias[:, h, :, :, h] = np.where(visible, alibi, NEG)
    return bias.reshape(2 * N_HEADS * dec_seq, n_pos * N_HEADS)


def _paged_attention(pt_ref, step, n_steps, first_seq, q_ref, kn_ref, vn_ref, sga_ref, bias_ref, sg_ref,
                     lq1, lk1, lq2, lk2, ck_hbm, cv_hbm, o_ref, kbuf, vbuf, ksem, vsem,
                     *, layer, lam_init, n_pages, dec_seq, fillers):
    n_seq = len(fillers)
    n_rows = 2 * N_HEADS * dec_seq
    half = N_HEADS * dec_seq
    page_rows = PAGE * N_HEADS
    past_rows = n_pages * page_rows
    new_rows = dec_seq * N_HEADS
    slots = range(n_seq)

    def page_copies(seq_idx, slot):
        copies = []
        for j in range(n_pages):
            page = pt_ref[seq_idx * n_pages + j]
            copies.append(pltpu.make_async_copy(ck_hbm.at[layer, page], kbuf.at[slot, j], ksem.at[slot]))
            copies.append(pltpu.make_async_copy(cv_hbm.at[layer, page], vbuf.at[slot, j], vsem.at[slot]))
        return copies

    @pl.when(step == 0)
    def _():
        for slot in slots:
            for cp in page_copies(first_seq + slot, slot):
                cp.start()

    lane = lax.broadcasted_iota(jnp.int32, (half, HEAD_W), 1)
    lam = _lam_value(lq1, lk1, lq2, lk2, lam_init)
    pad = jnp.zeros((NEW_ROWS - new_rows, HEAD_W), F32)
    nt = (((1,), (1,)), ((), ()))

    def attend(slot, seq_idx):
        rows = slice(slot * dec_seq, (slot + 1) * dec_seq)
        new = slice(slot * new_rows, (slot + 1) * new_rows)
        k_new = jnp.concatenate([kn_ref[new, :], pad], axis=0)
        v_new = jnp.concatenate([vn_ref[new, :], pad], axis=0)
        qh = jnp.concatenate([q_ref[rows, h * HEAD_W:(h + 1) * HEAD_W] for h in range(N_HEADS)], axis=0)
        qrows = jnp.concatenate([jnp.where(lane < DK, qh, 0.0), jnp.where(lane >= DK, qh, 0.0)], axis=0)
        s = [lax.dot_general(qrows, kbuf[slot, j], nt, preferred_element_type=F32) for j in range(n_pages)]
        s.append(lax.dot_general(qrows, k_new, nt, preferred_element_type=F32))
        s = jnp.concatenate(s, axis=1) + bias_ref[...]
        yield
        p = jnp.exp2(s - jnp.max(s, axis=-1, keepdims=True))
        inv = 1.0 / jnp.sum(p, axis=-1, keepdims=True)
        yield
        pv = jnp.dot(p[:, past_rows:], v_new, preferred_element_type=F32)
        for j in range(n_pages):
            pv = pv + jnp.dot(p[:, j * page_rows:(j + 1) * page_rows], vbuf[slot, j],
                              preferred_element_type=F32)
        res = pv[0:half, :] * inv[0:half, :] - pv[half:n_rows, :] * (lam * inv[half:n_rows, :])

        @pl.when(step + 1 < n_steps)
        def _():
            for cp in page_copies(seq_idx + n_seq, slot):
                cp.start()

        yield
        for h in range(N_HEADS):
            sl = slice(h * HEAD_W, (h + 1) * HEAD_W)
            o = res[h * dec_seq:(h + 1) * dec_seq, :]
            ms = jnp.mean(o * o, axis=-1, keepdims=True)
            att = o * lax.rsqrt(ms + EPS) * sg_ref[...] * (1.0 - lam_init)
            o_ref[rows, sl] = (att * sga_ref[rows, sl]).astype(o_ref.dtype)

    for slot in slots:
        seq_idx = first_seq + n_seq * step + slot
        for cp in page_copies(seq_idx, slot):
            cp.wait()
        pending = list(fillers[slot])
        for _ in attend(slot, seq_idx):
            if pending:
                pending.pop(0)()
        for filler in pending:
            filler()


N_PAGED_IN = 12


def _paged_operands(qs, knbuf, vnbuf, sgas, cache_k, cache_v, sg, lams, layer, dec_seq, n_pages,
                    first_seq, n_seq, n_steps, step_of):
    bias = jnp.asarray(_sample_bias(n_pages, dec_seq))
    first_blk = first_seq // n_seq
    const = lambda shape, **kw: pl.BlockSpec(shape, lambda *i: (0, 0), **kw)
    seqspec = pl.BlockSpec((n_seq * dec_seq, D_GRP), lambda *i: (first_blk + step_of(*i), 0))
    newspec = pl.BlockSpec((None, n_seq * dec_seq * N_HEADS, HEAD_W),
                           lambda *i: (layer, first_blk + step_of(*i), 0))
    hbm = pl.BlockSpec(memory_space=pl.ANY)
    operands = [qs, knbuf, vnbuf, sgas, bias, sg, *lams, cache_k, cache_v]
    in_specs = ([seqspec, newspec, newspec, seqspec, const(bias.shape, pipeline_mode=pl.Buffered(1)),
                 const((1, HEAD_W))] + [const((1, DK))] * 4 + [hbm, hbm])
    assert len(operands) == len(in_specs) == N_PAGED_IN
    out_spec = pl.BlockSpec((n_seq * dec_seq, D_GRP), lambda *i: (step_of(*i), 0))
    out_shape = jax.ShapeDtypeStruct((n_steps * n_seq * dec_seq, D_GRP), F32)
    page_buf = pltpu.VMEM((n_seq, n_pages, PAGE * N_HEADS, HEAD_W), F32)
    scratch = [page_buf, page_buf, pltpu.SemaphoreType.DMA((n_seq,)), pltpu.SemaphoreType.DMA((n_seq,))]
    return operands, in_specs, out_spec, out_shape, scratch


def _inproj_decode_kernel(pt_ref, x_ref, g_ref, w_ref, lng_ref, lnb_ref, *rest,
                          n_alias, n_seq, first_seq, layer, lam_init, n_pages, dec_seq):
    paged_in, rest = rest[:N_PAGED_IN], rest[N_PAGED_IN + n_alias:]
    outs, att_ref, scratch = rest[:6], rest[6], rest[7:]
    stages = _inproj_stages(x_ref, g_ref, w_ref, lng_ref, lnb_ref, outs, False)
    per_slot = -(-len(stages) // n_seq)
    stages = stages[:per_slot] + stages[per_slot:][::-1]
    fillers = [stages[slot * per_slot:(slot + 1) * per_slot] for slot in range(n_seq)]
    _paged_attention(pt_ref, pl.program_id(0), pl.num_programs(0), first_seq, *paged_in, att_ref, *scratch,
                     layer=layer, lam_init=lam_init, n_pages=n_pages, dec_seq=dec_seq, fillers=fillers)


def _inproj_decode(x, g, w, lng, lnb, layer, depth, prev, pt, paged, first_seq, n_seq, dec_seq, n_pages,
                   lam_init):
    m = x.shape[0]
    tm = ROW_TILE // 2
    n_steps = m // tm
    const = lambda shape, **kw: pl.BlockSpec(shape, lambda t, pt: (0, 0), **kw)
    once = dict(pipeline_mode=pl.Buffered(1))
    p_operands, p_specs, p_out_spec, p_out_shape, p_scratch = _paged_operands(
        *paged, layer, dec_seq, n_pages, first_seq, n_seq, n_steps, lambda t, pt: t)
    out_shape, out_specs, stacked_idx = _inproj_out_specs(
        m, tm, BF16, layer, depth, prev is None, False, lambda t, pt: t)
    prev = list(prev) if prev is not None else []
    n_inputs = 6 + N_PAGED_IN
    aliases = {n_inputs + i: stacked_idx[i] for i in range(len(prev))}
    grid_spec = pltpu.PrefetchScalarGridSpec(
        num_scalar_prefetch=1,
        grid=(n_steps,),
        in_specs=[pl.BlockSpec((tm, D_MODEL), lambda t, pt: (t, 0)), const((1, D_MODEL)),
                  pl.BlockSpec((None, D_MODEL, N_GRP * D_GRP), lambda t, pt: (layer, 0, 0), **once),
                  const((1, D_GRP)), const((1, D_GRP))]
                 + p_specs + [pl.BlockSpec(memory_space=pl.ANY)] * len(prev),
        out_specs=out_specs + [p_out_spec],
        scratch_shapes=p_scratch,
    )
    return pl.pallas_call(
        functools.partial(_inproj_decode_kernel, n_alias=len(prev), n_seq=n_seq, first_seq=first_seq,
                          layer=layer, lam_init=lam_init, n_pages=n_pages, dec_seq=dec_seq),
        grid_spec=grid_spec,
        out_shape=out_shape + [p_out_shape],
        input_output_aliases=aliases,
        compiler_params=pltpu.CompilerParams(dimension_semantics=("arbitrary",),
                                             vmem_limit_bytes=VMEM_LIMIT),
        name="inproj_decode",
    )(pt, x, g, w, lng, lnb, *p_operands, *prev)


def _mix_out_compute(x_ref, ugc_ref, vn_ref, ws_ref, bsb_ref, wout_ref, fg_ref, o_ref, cat_sc, final):
    tm = x_ref.shape[0]
    nc = tm // CHUNK
    r = lax.broadcasted_iota(jnp.int32, (CHUNK, CHUNK), 0)
    c = lax.broadcasted_iota(jnp.int32, (CHUNK, CHUNK), 1)
    for h in range(N_HEADS):
        sl = slice(h * HEAD_W, (h + 1) * HEAD_W)
        ws = jnp.where(c <= r, ws_ref[h], 0.0).astype(BF16)
        vn = jnp.concatenate([vn_ref[ci * CHUNK:(ci + 1) * CHUNK, sl].astype(BF16)
                              for ci in range(nc)], axis=1)
        mixed = jnp.dot(ws, vn, preferred_element_type=F32)
        for ci in range(nc):
            rows = slice(ci * CHUNK, (ci + 1) * CHUNK)
            mc = mixed[:, ci * HEAD_W:(ci + 1) * HEAD_W] + bsb_ref[:, sl]
            cat_sc[rows, D_GRP + h * HEAD_W:D_GRP + (h + 1) * HEAD_W] = (
                ugc_ref[rows, sl].astype(F32) * mc).astype(BF16)
    y = x_ref[...] + jnp.dot(cat_sc[...], wout_ref[...], preferred_element_type=F32)
    if final:
        ms = jnp.mean(y * y, axis=-1, keepdims=True)
        y = y * lax.rsqrt(ms + EPS) * fg_ref[...]
    o_ref[...] = y


def _mix_out_kernel(x_ref, att_ref, ugc_ref, vn_ref, ws_ref, bsb_ref, wout_ref, fg_ref,
                    o_ref, cat_sc, *, final):
    cat_sc[:, 0:D_GRP] = att_ref[...].astype(BF16)
    _mix_out_compute(x_ref, ugc_ref, vn_ref, ws_ref, bsb_ref, wout_ref, fg_ref, o_ref, cat_sc, final)


def _mix_out(x, att, ugc, vn, ws, bsb, wout, layer, fg, final):
    m = x.shape[0]
    tm = ROW_TILE
    row = lambda width: pl.BlockSpec((tm, width), lambda i: (i, 0))
    return pl.pallas_call(
        functools.partial(_mix_out_kernel, final=final),
        grid=(m // tm,),
        in_specs=[row(D_MODEL), row(D_GRP), row(D_GRP), row(D_GRP),
                  pl.BlockSpec((N_HEADS, CHUNK, CHUNK), lambda i: (0, 0, 0)),
                  pl.BlockSpec((CHUNK, D_GRP), lambda i: (0, 0)),
                  pl.BlockSpec((None, 2 * D_GRP, D_MODEL), lambda i: (layer, 0, 0)),
                  pl.BlockSpec((1, D_MODEL), lambda i: (0, 0))],
        out_specs=row(D_MODEL),
        out_shape=jax.ShapeDtypeStruct((m, D_MODEL), F32),
        scratch_shapes=[pltpu.VMEM((tm, 2 * D_GRP), BF16)],
        compiler_params=pltpu.CompilerParams(dimension_semantics=("parallel",),
                                             vmem_limit_bytes=VMEM_LIMIT),
        name="mix_out",
    )(x, att, ugc, vn, ws, bsb, wout, fg)


def kernel(x_prompt, x_sample, cache_k, cache_v, page_table, norm_g, w_in, lam_q1, lam_k1,
           lam_q2, lam_k2, subln_g, ln_v_g, ln_v_b, w_s, b_s, w_out, final_g):
    batch, seq, _ = x_prompt.shape
    dec_batch, dec_seq, _ = x_sample.shape
    depth, n_pool = cache_k.shape[0], cache_k.shape[1]
    n_pages = page_table.shape[1]
    assert seq % ATT_BLK == 0 and (batch * seq) % ROW_TILE == 0 and (dec_batch * dec_seq) % ROW_TILE == 0
    assert CHUNK % dec_seq == 0 and dec_seq % 8 == 0 and ATT_BLK % PAGE == 0
    assert dec_seq * N_HEADS <= NEW_ROWS
    proj_steps = (batch * seq) // (ROW_TILE // 2)
    assert dec_batch % proj_steps == 0
    n_seq_proj = dec_batch // proj_steps

    hp = x_prompt.reshape(batch * seq, D_MODEL)
    hs = x_sample.reshape(dec_batch * dec_seq, D_MODEL)
    ck = cache_k.reshape(depth, n_pool, PAGE * N_HEADS, HEAD_W)
    cv = cache_v.reshape(depth, n_pool, PAGE * N_HEADS, HEAD_W)
    pt = page_table.reshape(-1)
    w_in_b = w_in.astype(BF16)
    w_out_b = w_out.astype(BF16)
    fg = final_g.reshape(1, D_MODEL)
    eye = jnp.eye(CHUNK // dec_seq, dtype=F32)

    prompt_bufs, sample_bufs = None, None
    for l in range(depth):
        lam_init = _lambda_init(l)
        g = norm_g[l].reshape(1, D_MODEL)
        lng = ln_v_g[l].reshape(1, D_GRP)
        lnb = ln_v_b[l].reshape(1, D_GRP)
        sg = subln_g[l].reshape(1, HEAD_W)
        lams = [a[l].reshape(1, DK) for a in (lam_q1, lam_k1, lam_q2, lam_k2)]
        final = l == depth - 1

        qs, ksbuf, vsbuf, sgas, ugcs, vns, vnsbuf = _inproj(hs, g, w_in_b, lng, lnb, F32, l, depth,
                                                            sample_bufs, True)
        sample_bufs = (ksbuf, vsbuf, vnsbuf)
        paged = (qs, ksbuf, vsbuf, sgas, ck, cv, sg, lams)
        q, kbuf, vbuf, sga, ugc, vn, atts = _inproj_decode(
            hp, g, w_in_b, lng, lnb, l, depth, prompt_bufs,
            pt, paged, 0, n_seq_proj, dec_seq, n_pages, lam_init)
        prompt_bufs = (kbuf, vbuf)

        bsb = jnp.repeat(b_s[l].T, HEAD_W, axis=1)
        hp = _prompt_attn_mix(q, kbuf, vbuf, l, sga, sg, lams, lam_init, batch, seq,
                              hp, ugc, vn, w_s[l], bsb, w_out_b, fg, final)

        ws8 = w_s[l][:, :dec_seq, :dec_seq]
        ws_bd = (eye[None, :, None, :, None] * ws8[:, None, :, None, :]).reshape(N_HEADS, CHUNK, CHUNK)
        bsb = jnp.tile(jnp.repeat(b_s[l][:, :dec_seq].T, HEAD_W, axis=1), (CHUNK // dec_seq, 1))
        hs = _mix_out(hs, atts, ugcs, vns, ws_bd, bsb, w_out_b, l, fg, final)

    shape_p = (depth, batch, seq, N_HEADS, HEAD_W)
    shape_s = (depth, dec_batch, dec_seq, N_HEADS, HEAD_W)
    return (hp.reshape(batch, seq, D_MODEL),
            hs.reshape(dec_batch, dec_seq, D_MODEL),
            prompt_bufs[0].reshape(shape_p),
            prompt_bufs[1].reshape(shape_p),
            sample_bufs[0].reshape(shape_s),
            sample_bufs[1].reshape(shape_s),
            sample_bufs[2].reshape(shape_s))
```

```python
import functools
import math

import numpy as np
import jax
import jax.numpy as jnp
from jax import lax
from jax.experimental import pallas as pl
from jax.experimental.pallas import tpu as pltpu

F32 = jnp.float32
BF16 = jnp.bfloat16

D_MODEL = 1024
N_HEADS = 4
DK = 64
HEAD_W = 128
D_GRP = N_HEADS * HEAD_W
N_GRP = 7
CHUNK = 128
PAGE = 128
EPS = 1e-6
NEG = -1e30
ONES_ROWS = 16
POS_LANES = 128
SLOPE_PARTS = 3
LOG2E = np.float32(math.log2(math.e))

ROW_TILE = 512
ATT_BLK = 256
NEW_ROWS = 128
VMEM_LIMIT = 56 * 1024 * 1024


def _lambda_init(layer):
    return 0.8 - 0.6 * math.exp(-0.3 * layer)


def _alibi_slope(h):
    return 2.0 ** (-8.0 * (h + 1) / N_HEADS)


def _slope_parts(h):
    target = np.float32(_alibi_slope(h)) * LOG2E
    rest, parts = target, []
    for _ in range(SLOPE_PARTS):
        part = np.asarray(rest, dtype=BF16).astype(np.float32)
        parts.append(float(part))
        rest = np.float32(rest - part)
    assert rest == 0.0
    return parts


def _lam_value(lq1, lk1, lq2, lk2, lam_init):
    a = jnp.sum(lq1[...] * lk1[...], axis=-1, keepdims=True)
    b = jnp.sum(lq2[...] * lk2[...], axis=-1, keepdims=True)
    return jnp.exp(a) - jnp.exp(b) + lam_init


def _silu(x):
    return x * jax.nn.sigmoid(x)


def _head_rows(h, n_rows, first_row=0):
    return pl.ds(first_row * N_HEADS + h, n_rows, stride=N_HEADS)


def _inproj_stages(x_ref, g_ref, w_ref, lng_ref, lnb_ref, outs, with_vn_out):
    q_ref, k_ref, v_ref, sga_ref, ugc_ref, vn_ref = outs[:6]
    tm = x_ref.shape[0]

    normed = []

    def proj(i):
        if not normed:
            x = x_ref[...]
            ms = jnp.mean(x * x, axis=-1, keepdims=True)
            normed.append((x * lax.rsqrt(ms + EPS) * g_ref[...]).astype(BF16))
        return jnp.dot(normed[0], w_ref[:, i * D_GRP:(i + 1) * D_GRP], preferred_element_type=F32)

    def store_heads(ref, c, val):
        if len(ref.shape) == 3:
            for d in range(ref.shape[0]):
                ref[d, _head_rows(c, tm), :] = val
        else:
            ref[_head_rows(c, tm), :] = val

    def store_all_heads(ref, val):
        for c in range(N_HEADS):
            store_heads(ref, c, val[:, c * HEAD_W:(c + 1) * HEAD_W])

    def q_stage():
        q_ref[...] = (proj(0) * float(np.float32(DK ** -0.5) * LOG2E)).astype(q_ref.dtype)

    def k_stage():
        store_all_heads(k_ref, proj(1))

    def v_stage():
        store_all_heads(v_ref, proj(2))

    def gate_stage():
        sga_ref[...] = _silu(proj(3)).astype(sga_ref.dtype)

    def ugc_stage():
        ugc_ref[...] = (proj(4) * _silu(proj(6))).astype(ugc_ref.dtype)

    def vn_stage():
        vc = proj(5)
        for c in range(N_HEADS):
            sl = slice(c * HEAD_W, (c + 1) * HEAD_W)
            blk = vc[:, sl]
            mu = jnp.mean(blk, axis=-1, keepdims=True)
            xc = blk - mu
            var = jnp.mean(xc * xc, axis=-1, keepdims=True)
            vn = xc * lax.rsqrt(var + EPS) * lng_ref[:, sl] + lnb_ref[:, sl]
            vn_ref[:, sl] = vn.astype(vn_ref.dtype)
            if with_vn_out:
                store_heads(outs[6], c, vn)

    return [q_stage, k_stage, v_stage, gate_stage, ugc_stage, vn_stage]


def _inproj_kernel(*refs, n_alias, with_vn_out):
    outs = refs[5 + n_alias:]
    for stage in _inproj_stages(*refs[:5], outs, with_vn_out):
        stage()


def _inproj_out_specs(m, tm, act_dtype, layer, depth, first, with_vn_out, index):
    row = pl.BlockSpec((tm, D_GRP), lambda *i: (index(*i), 0))
    act = jax.ShapeDtypeStruct((m, D_GRP), act_dtype)
    stacked = jax.ShapeDtypeStruct((depth, m * N_HEADS, HEAD_W), F32)
    if first:
        stacked_spec = pl.BlockSpec((depth, tm * N_HEADS, HEAD_W), lambda *i: (0, index(*i), 0))
    else:
        stacked_spec = pl.BlockSpec((None, tm * N_HEADS, HEAD_W), lambda *i: (layer, index(*i), 0))
    extra = 1 if with_vn_out else 0
    out_shape = [act, stacked, stacked, act, act, act] + [stacked] * extra
    out_specs = [row, stacked_spec, stacked_spec, row, row, row] + [stacked_spec] * extra
    return out_shape, out_specs, [1, 2] + [6] * extra


def _inproj(x, g, w, lng, lnb, act_dtype, layer, depth, prev, with_vn_out):
    m = x.shape[0]
    tm = ROW_TILE
    const = lambda shape: pl.BlockSpec(shape, lambda i: (0, 0))
    out_shape, out_specs, stacked_idx = _inproj_out_specs(
        m, tm, act_dtype, layer, depth, prev is None, with_vn_out, lambda i: i)
    prev = list(prev) if prev is not None else []
    aliases = {5 + n: stacked_idx[n] for n in range(len(prev))}
    return pl.pallas_call(
        functools.partial(_inproj_kernel, n_alias=len(prev), with_vn_out=with_vn_out),
        grid=(m // tm,),
        in_specs=[pl.BlockSpec((tm, D_MODEL), lambda i: (i, 0)), const((1, D_MODEL)),
                  pl.BlockSpec((None, D_MODEL, N_GRP * D_GRP), lambda i: (layer, 0, 0)),
                  const((1, D_GRP)), const((1, D_GRP))]
                 + [pl.BlockSpec(memory_space=pl.ANY)] * len(prev),
        out_specs=out_specs,
        out_shape=out_shape,
        input_output_aliases=aliases,
        compiler_params=pltpu.CompilerParams(dimension_semantics=("parallel",),
                                             vmem_limit_bytes=VMEM_LIMIT),
        name="inproj",
    )(x, g, w, lng, lnb, *prev)


def _prompt_attn_kernel(q_ref, k_ref, v_ref, sga_ref, sg_ref, lq1, lk1, lq2, lk2,
                        x_ref, ugc_ref, vn_ref, ws_ref, bsb_ref, wout_ref, fg_ref,
                        o_ref, kb_sc, vt_sc, cm_sc, att_sc, *stat_sc, lam_init, tb, nblk, final):
    m_sc = stat_sc[:N_HEADS]
    acc_sc = stat_sc[N_HEADS:]
    i = pl.program_id(1)

    @pl.when(i == 0)
    def _():
        pos = lax.broadcasted_iota(jnp.int32, (tb, POS_LANES), 0)
        lane = lax.broadcasted_iota(jnp.int32, (tb, POS_LANES), 1)
        for jb in range(nblk):
            kpos = pos + jb * tb
            hi = (kpos // PAGE) * PAGE
            aug = jnp.where(lane >= 2 * SLOPE_PARTS, 0, jnp.where(lane % 2 == 0, hi, kpos - hi))
            aug = aug.astype(F32).astype(BF16)
            for h in range(N_HEADS):
                rows = _head_rows(h, tb, jb * tb)
                kb_sc[h, jb * tb:(jb + 1) * tb, 0:HEAD_W] = k_ref[rows, :].astype(BF16)
                kb_sc[h, jb * tb:(jb + 1) * tb, HEAD_W:HEAD_W + POS_LANES] = aug
                vt_sc[h, jb, 0:HEAD_W, :] = v_ref[rows, :].T.astype(BF16)
                vt_sc[h, jb, HEAD_W:HEAD_W + ONES_ROWS, :] = jnp.ones((ONES_ROWS, tb), BF16)

    lane = lax.broadcasted_iota(jnp.int32, (tb, HEAD_W), 1)
    qz = []
    for h in range(N_HEADS):
        q = q_ref[:, h * HEAD_W:(h + 1) * HEAD_W]
        zero = jnp.zeros_like(q)
        slope = jnp.zeros((tb, POS_LANES), F32)
        for n, part in enumerate(_slope_parts(h)):
            slope = jnp.where(lane // 2 == n, part, slope)
        slope = slope.astype(BF16)
        q2 = jnp.concatenate([jnp.where(lane < DK, q, zero), jnp.where(lane >= DK, q, zero)], axis=0)
        qz.append(jnp.concatenate([q2, jnp.concatenate([slope, slope], axis=0)], axis=1))
        m_sc[h][...] = jnp.full(m_sc[h].shape, NEG, F32)
        acc_sc[h][...] = jnp.zeros(acc_sc[h].shape, F32)

    def steps(first_blk, nkb, masked):
        heads = range(N_HEADS)
        rows = pl.ds(pl.multiple_of(first_blk * tb, tb), nkb * tb)
        s = [lax.dot_general(kb_sc[h, rows, :], qz[h], (((1,), (1,)), ((), ())),
                             preferred_element_type=F32) for h in heads]
        if masked:
            key_i = lax.broadcasted_iota(jnp.int32, (nkb * tb, 2 * tb), 0)
            qry_i = lax.broadcasted_iota(jnp.int32, (nkb * tb, 2 * tb), 1)
            qry_i = jnp.where(qry_i >= tb, qry_i - tb, qry_i) + (nkb - 1) * tb
            s = [jnp.where(key_i <= qry_i, sh, NEG) for sh in s]
        m_old = [m_sc[h][...] for h in heads]
        m_new = [jnp.maximum(m_old[h], jnp.max(s[h], axis=0, keepdims=True)) for h in heads]
        p = [jnp.exp2(s[h] - m_new[h]).astype(BF16) for h in heads]
        for h in heads:
            pv = jnp.dot(vt_sc[h, first_blk], p[h][0:tb, :], preferred_element_type=F32)
            for kb in range(1, nkb):
                pv = pv + jnp.dot(vt_sc[h, first_blk + kb], p[h][kb * tb:(kb + 1) * tb, :],
                                  preferred_element_type=F32)
            acc_sc[h][...] = acc_sc[h][...] * jnp.exp2(m_old[h] - m_new[h]) + pv
            m_sc[h][...] = m_new[h]

    def body(jj, carry):
        steps(2 * jj, 2, False)
        return carry

    lax.fori_loop(0, i // 2, body, 0)

    @pl.when(i % 2 == 1)
    def _():
        steps(i - 1, 2, True)

    @pl.when(i % 2 == 0)
    def _():
        steps(i, 1, True)

    y = _mix_chunk_part(x_ref, ugc_ref, vn_ref, ws_ref, bsb_ref, wout_ref, cm_sc)

    lam = _lam_value(lq1, lk1, lq2, lk2, lam_init)
    for h in range(N_HEADS):
        sl = slice(h * HEAD_W, (h + 1) * HEAD_W)
        acc = acc_sc[h]
        inv = 1.0 / acc[HEAD_W:HEAD_W + 1, :]
        o1 = acc[0:HEAD_W, 0:tb] * inv[:, 0:tb]
        o2 = acc[0:HEAD_W, tb:2 * tb] * inv[:, tb:2 * tb]
        o = (o1 - lam * o2).T
        ms = jnp.mean(o * o, axis=-1, keepdims=True)
        att = o * lax.rsqrt(ms + EPS) * sg_ref[...] * (1.0 - lam_init)
        att_sc[:, sl] = (att * sga_ref[:, sl].astype(F32)).astype(BF16)
    _mix_attn_part(y, att_sc[...], wout_ref, fg_ref, o_ref, final)


def _prompt_attn_mix(q, kbuf, vbuf, layer, sga, sg, lams, lam_init, batch, seq,
                     x, ugc, vn, ws, bsb, wout, fg, final):
    tb = ATT_BLK
    nblk = seq // tb
    rowspec = lambda width: pl.BlockSpec((tb, width), lambda b, i: (b * nblk + i, 0))
    kvspec = pl.BlockSpec((None, seq * N_HEADS, HEAD_W), lambda b, i: (layer, b, 0))
    const = lambda shape, **kw: pl.BlockSpec(shape, lambda b, i: (0,) * len(shape), **kw)
    stats = ([pltpu.VMEM((1, 2 * tb), F32)] * N_HEADS
             + [pltpu.VMEM((HEAD_W + ONES_ROWS, 2 * tb), F32)] * N_HEADS)
    return pl.pallas_call(
        functools.partial(_prompt_attn_kernel, lam_init=lam_init, tb=tb, nblk=nblk, final=final),
        grid=(batch, nblk),
        in_specs=[rowspec(D_GRP), kvspec, kvspec, rowspec(D_GRP), const((1, HEAD_W))] + [const((1, DK))] * 4
                 + [rowspec(D_MODEL), rowspec(D_GRP), rowspec(D_GRP), const((N_HEADS, CHUNK, CHUNK)),
                    const((CHUNK, D_GRP)),
                    pl.BlockSpec((None, 2 * D_GRP, D_MODEL), lambda b, i: (layer, 0, 0),
                                 pipeline_mode=pl.Buffered(1)),
                    const((1, D_MODEL))],
        out_specs=rowspec(D_MODEL),
        out_shape=jax.ShapeDtypeStruct(x.shape, F32),
        scratch_shapes=[pltpu.VMEM((N_HEADS, seq, HEAD_W + POS_LANES), BF16),
                        pltpu.VMEM((N_HEADS, nblk, HEAD_W + ONES_ROWS, tb), BF16),
                        pltpu.VMEM((tb, D_GRP), BF16), pltpu.VMEM((tb, D_GRP), BF16)] + stats,
        compiler_params=pltpu.CompilerParams(
            dimension_semantics=("parallel", "arbitrary"),
            vmem_limit_bytes=VMEM_LIMIT),
        name="prompt_attn_mix",
    )(q, kbuf, vbuf, sga, sg, *lams, x, ugc, vn, ws, bsb, wout, fg)


def _sample_bias(n_pages, dec_seq):
    past = n_pages * PAGE
    n_pos = past + NEW_ROWS // N_HEADS
    t = np.arange(dec_seq)[:, None]
    kpos = np.arange(n_pos)[None, :]
    visible = (kpos < past) | (kpos - past <= t)
    bias = np.full((2, N_HEADS, dec_seq, n_pos, N_HEADS), NEG, np.float32)
    for h in range(N_HEADS):
        alibi = -np.float32(_alibi_slope(h)) * (past + t - kpos).astype(np.float32) * LOG2E
        bias[:, h, :, :, h] = np.where(visible, alibi, NEG)
    return bias.reshape(2 * N_HEADS * dec_seq, n_pos * N_HEADS)


def _paged_attention(pt_ref, step, n_steps, first_seq, q_ref, kn_ref, vn_ref, sga_ref, bias_ref, sg_ref,
                     lq1, lk1, lq2, lk2, ck_hbm, cv_hbm, o_ref, kbuf, vbuf, ksem, vsem,
                     *, layer, lam_init, n_pages, dec_seq, fillers):
    n_seq = len(fillers)
    n_rows = 2 * N_HEADS * dec_seq
    half = N_HEADS * dec_seq
    page_rows = PAGE * N_HEADS
    past_rows = n_pages * page_rows
    new_rows = dec_seq * N_HEADS
    slots = range(n_seq)

    def page_copies(seq_idx, slot):
        copies = []
        for j in range(n_pages):
            page = pt_ref[seq_idx * n_pages + j]
            copies.append(pltpu.make_async_copy(ck_hbm.at[layer, page], kbuf.at[slot, j], ksem.at[slot]))
            copies.append(pltpu.make_async_copy(cv_hbm.at[layer, page], vbuf.at[slot, j], vsem.at[slot]))
        return copies

    @pl.when(step == 0)
    def _():
        for slot in slots:
            for cp in page_copies(first_seq + slot, slot):
                cp.start()

    lane = lax.broadcasted_iota(jnp.int32, (half, HEAD_W), 1)
    lam = _lam_value(lq1, lk1, lq2, lk2, lam_init)
    pad = jnp.zeros((NEW_ROWS - new_rows, HEAD_W), F32)
    nt = (((1,), (1,)), ((), ()))

    def attend(slot, seq_idx):
        rows = slice(slot * dec_seq, (slot + 1) * dec_seq)
        new = slice(slot * new_rows, (slot + 1) * new_rows)
        k_new = jnp.concatenate([kn_ref[new, :], pad], axis=0)
        v_new = jnp.concatenate([vn_ref[new, :], pad], axis=0)
        qh = jnp.concatenate([q_ref[rows, h * HEAD_W:(h + 1) * HEAD_W] for h in range(N_HEADS)], axis=0)
        qrows = jnp.concatenate([jnp.where(lane < DK, qh, 0.0), jnp.where(lane >= DK, qh, 0.0)], axis=0)
        s = [lax.dot_general(qrows, kbuf[slot, j], nt, preferred_element_type=F32) for j in range(n_pages)]
        s.append(lax.dot_general(qrows, k_new, nt, preferred_element_type=F32))
        s = jnp.concatenate(s, axis=1) + bias_ref[...]
        yield
        p = jnp.exp2(s - jnp.max(s, axis=-1, keepdims=True))
        inv = 1.0 / jnp.sum(p, axis=-1, keepdims=True)
        yield
        pv = jnp.dot(p[:, past_rows:], v_new, preferred_element_type=F32)
        for j in range(n_pages):
            pv = pv + jnp.dot(p[:, j * page_rows:(j + 1) * page_rows], vbuf[slot, j],
                              preferred_element_type=F32)
        res = pv[0:half, :] * inv[0:half, :] - pv[half:n_rows, :] * (lam * inv[half:n_rows, :])

        @pl.when(step + 1 < n_steps)
        def _():
            for cp in page_copies(seq_idx + n_seq, slot):
                cp.start()

        yield
        for h in range(N_HEADS):
            sl = slice(h * HEAD_W, (h + 1) * HEAD_W)
            o = res[h * dec_seq:(h + 1) * dec_seq, :]
            ms = jnp.mean(o * o, axis=-1, keepdims=True)
            att = o * lax.rsqrt(ms + EPS) * sg_ref[...] * (1.0 - lam_init)
            o_ref[rows, sl] = (att * sga_ref[rows, sl]).astype(o_ref.dtype)

    for slot in slots:
        seq_idx = first_seq + n_seq * step + slot
        for cp in page_copies(seq_idx, slot):
            cp.wait()
        pending = list(fillers[slot])
        for _ in attend(slot, seq_idx):
            if pending:
                pending.pop(0)()
        for filler in pending:
            filler()


N_PAGED_IN = 12


def _paged_operands(qs, knbuf, vnbuf, sgas, cache_k, cache_v, sg, lams, layer, dec_seq, n_pages,
                    first_seq, n_seq, n_steps, step_of):
    bias = jnp.asarray(_sample_bias(n_pages, dec_seq))
    first_blk = first_seq // n_seq
    const = lambda shape, **kw: pl.BlockSpec(shape, lambda *i: (0, 0), **kw)
    seqspec = pl.BlockSpec((n_seq * dec_seq, D_GRP), lambda *i: (first_blk + step_of(*i), 0))
    newspec = pl.BlockSpec((None, n_seq * dec_seq * N_HEADS, HEAD_W),
                           lambda *i: (layer, first_blk + step_of(*i), 0))
    hbm = pl.BlockSpec(memory_space=pl.ANY)
    operands = [qs, knbuf, vnbuf, sgas, bias, sg, *lams, cache_k, cache_v]
    in_specs = ([seqspec, newspec, newspec, seqspec, const(bias.shape, pipeline_mode=pl.Buffered(1)),
                 const((1, HEAD_W))] + [const((1, DK))] * 4 + [hbm, hbm])
    assert len(operands) == len(in_specs) == N_PAGED_IN
    out_spec = pl.BlockSpec((n_seq * dec_seq, D_GRP), lambda *i: (step_of(*i), 0))
    out_shape = jax.ShapeDtypeStruct((n_steps * n_seq * dec_seq, D_GRP), F32)
    page_buf = pltpu.VMEM((n_seq, n_pages, PAGE * N_HEADS, HEAD_W), F32)
    scratch = [page_buf, page_buf, pltpu.SemaphoreType.DMA((n_seq,)), pltpu.SemaphoreType.DMA((n_seq,))]
    return operands, in_specs, out_spec, out_shape, scratch


def _inproj_decode_kernel(pt_ref, x_ref, g_ref, w_ref, lng_ref, lnb_ref, *rest,
                          n_alias, n_seq, first_seq, layer, lam_init, n_pages, dec_seq):
    paged_in, rest = rest[:N_PAGED_IN], rest[N_PAGED_IN + n_alias:]
    outs, att_ref, scratch = rest[:6], rest[6], rest[7:]
    stages = _inproj_stages(x_ref, g_ref, w_ref, lng_ref, lnb_ref, outs, False)
    per_slot = -(-len(stages) // n_seq)
    stages = stages[:per_slot] + stages[per_slot:][::-1]
    fillers = [stages[slot * per_slot:(slot + 1) * per_slot] for slot in range(n_seq)]
    _paged_attention(pt_ref, pl.program_id(0), pl.num_programs(0), first_seq, *paged_in, att_ref, *scratch,
                     layer=layer, lam_init=lam_init, n_pages=n_pages, dec_seq=dec_seq, fillers=fillers)


def _inproj_decode(x, g, w, lng, lnb, layer, depth, prev, pt, paged, first_seq, n_seq, dec_seq, n_pages,
                   lam_init):
    m = x.shape[0]
    tm = ROW_TILE // 2
    n_steps = m // tm
    const = lambda shape, **kw: pl.BlockSpec(shape, lambda t, pt: (0, 0), **kw)
    once = dict(pipeline_mode=pl.Buffered(1))
    p_operands, p_specs, p_out_spec, p_out_shape, p_scratch = _paged_operands(
        *paged, layer, dec_seq, n_pages, first_seq, n_seq, n_steps, lambda t, pt: t)
    out_shape, out_specs, stacked_idx = _inproj_out_specs(
        m, tm, BF16, layer, depth, prev is None, False, lambda t, pt: t)
    prev = list(prev) if prev is not None else []
    n_inputs = 6 + N_PAGED_IN
    aliases = {n_inputs + i: stacked_idx[i] for i in range(len(prev))}
    grid_spec = pltpu.PrefetchScalarGridSpec(
        num_scalar_prefetch=1,
        grid=(n_steps,),
        in_specs=[pl.BlockSpec((tm, D_MODEL), lambda t, pt: (t, 0)), const((1, D_MODEL)),
                  pl.BlockSpec((None, D_MODEL, N_GRP * D_GRP), lambda t, pt: (layer, 0, 0), **once),
                  const((1, D_GRP)), const((1, D_GRP))]
                 + p_specs + [pl.BlockSpec(memory_space=pl.ANY)] * len(prev),
        out_specs=out_specs + [p_out_spec],
        scratch_shapes=p_scratch,
    )
    return pl.pallas_call(
        functools.partial(_inproj_decode_kernel, n_alias=len(prev), n_seq=n_seq, first_seq=first_seq,
                          layer=layer, lam_init=lam_init, n_pages=n_pages, dec_seq=dec_seq),
        grid_spec=grid_spec,
        out_shape=out_shape + [p_out_shape],
        input_output_aliases=aliases,
        compiler_params=pltpu.CompilerParams(dimension_semantics=("arbitrary",),
                                             vmem_limit_bytes=VMEM_LIMIT),
        name="inproj_decode",
    )(pt, x, g, w, lng, lnb, *p_operands, *prev)


def _mix_chunk_part(x_ref, ugc_ref, vn_ref, ws_ref, bsb_ref, wout_ref, cm_sc):
    tm = x_ref.shape[0]
    nc = tm // CHUNK
    r = lax.broadcasted_iota(jnp.int32, (CHUNK, CHUNK), 0)
    c = lax.broadcasted_iota(jnp.int32, (CHUNK, CHUNK), 1)
    for h in range(N_HEADS):
        sl = slice(h * HEAD_W, (h + 1) * HEAD_W)
        ws = jnp.where(c <= r, ws_ref[h], 0.0).astype(BF16)
        vn = jnp.concatenate([vn_ref[ci * CHUNK:(ci + 1) * CHUNK, sl].astype(BF16)
                              for ci in range(nc)], axis=1)
        mixed = jnp.dot(ws, vn, preferred_element_type=F32)
        for ci in range(nc):
            rows = slice(ci * CHUNK, (ci + 1) * CHUNK)
            mc = mixed[:, ci * HEAD_W:(ci + 1) * HEAD_W] + bsb_ref[:, sl]
            cm_sc[rows, sl] = (ugc_ref[rows, sl].astype(F32) * mc).astype(BF16)
    return x_ref[...] + jnp.dot(cm_sc[...], wout_ref[D_GRP:2 * D_GRP, :], preferred_element_type=F32)


def _mix_attn_part(y, att, wout_ref, fg_ref, o_ref, final):
    y = y + jnp.dot(att, wout_ref[0:D_GRP, :], preferred_element_type=F32)
    if final:
        ms = jnp.mean(y * y, axis=-1, keepdims=True)
        y = y * lax.rsqrt(ms + EPS) * fg_ref[...]
    o_ref[...] = y


def _mix_out_kernel(x_ref, att_ref, ugc_ref, vn_ref, ws_ref, bsb_ref, wout_ref, fg_ref,
                    o_ref, cm_sc, *, final):
    y = _mix_chunk_part(x_ref, ugc_ref, vn_ref, ws_ref, bsb_ref, wout_ref, cm_sc)
    _mix_attn_part(y, att_ref[...].astype(BF16), wout_ref, fg_ref, o_ref, final)


def _mix_out(x, att, ugc, vn, ws, bsb, wout, layer, fg, final):
    m = x.shape[0]
    tm = ROW_TILE
    row = lambda width: pl.BlockSpec((tm, width), lambda i: (i, 0))
    return pl.pallas_call(
        functools.partial(_mix_out_kernel, final=final),
        grid=(m // tm,),
        in_specs=[row(D_MODEL), row(D_GRP), row(D_GRP), row(D_GRP),
                  pl.BlockSpec((N_HEADS, CHUNK, CHUNK), lambda i: (0, 0, 0)),
                  pl.BlockSpec((CHUNK, D_GRP), lambda i: (0, 0)),
                  pl.BlockSpec((None, 2 * D_GRP, D_MODEL), lambda i: (layer, 0, 0)),
                  pl.BlockSpec((1, D_MODEL), lambda i: (0, 0))],
        out_specs=row(D_MODEL),
        out_shape=jax.ShapeDtypeStruct((m, D_MODEL), F32),
        scratch_shapes=[pltpu.VMEM((tm, D_GRP), BF16)],
        compiler_params=pltpu.CompilerParams(dimension_semantics=("parallel",),
                                             vmem_limit_bytes=VMEM_LIMIT),
        name="mix_out",
    )(x, att, ugc, vn, ws, bsb, wout, fg)


def kernel(x_prompt, x_sample, cache_k, cache_v, page_table, norm_g, w_in, lam_q1, lam_k1,
           lam_q2, lam_k2, subln_g, ln_v_g, ln_v_b, w_s, b_s, w_out, final_g):
    batch, seq, _ = x_prompt.shape
    dec_batch, dec_seq, _ = x_sample.shape
    depth, n_pool = cache_k.shape[0], cache_k.shape[1]
    n_pages = page_table.shape[1]
    assert seq % ATT_BLK == 0 and (batch * seq) % ROW_TILE == 0 and (dec_batch * dec_seq) % ROW_TILE == 0
    assert CHUNK % dec_seq == 0 and dec_seq % 8 == 0 and ATT_BLK % PAGE == 0
    assert dec_seq * N_HEADS <= NEW_ROWS
    proj_steps = (batch * seq) // (ROW_TILE // 2)
    assert dec_batch % proj_steps == 0
    n_seq_proj = dec_batch // proj_steps

    hp = x_prompt.reshape(batch * seq, D_MODEL)
    hs = x_sample.reshape(dec_batch * dec_seq, D_MODEL)
    ck = cache_k.reshape(depth, n_pool, PAGE * N_HEADS, HEAD_W)
    cv = cache_v.reshape(depth, n_pool, PAGE * N_HEADS, HEAD_W)
    pt = page_table.reshape(-1)
    w_in_b = w_in.astype(BF16)
    w_out_b = w_out.astype(BF16)
    fg = final_g.reshape(1, D_MODEL)
    eye = jnp.eye(CHUNK // dec_seq, dtype=F32)

    prompt_bufs, sample_bufs = None, None
    for l in range(depth):
        lam_init = _lambda_init(l)
        g = norm_g[l].reshape(1, D_MODEL)
        lng = ln_v_g[l].reshape(1, D_GRP)
        lnb = ln_v_b[l].reshape(1, D_GRP)
        sg = subln_g[l].reshape(1, HEAD_W)
        lams = [a[l].reshape(1, DK) for a in (lam_q1, lam_k1, lam_q2, lam_k2)]
        final = l == depth - 1

        qs, ksbuf, vsbuf, sgas, ugcs, vns, vnsbuf = _inproj(hs, g, w_in_b, lng, lnb, F32, l, depth,
                                                            sample_bufs, True)
        sample_bufs = (ksbuf, vsbuf, vnsbuf)
        paged = (qs, ksbuf, vsbuf, sgas, ck, cv, sg, lams)
        q, kbuf, vbuf, sga, ugc, vn, atts = _inproj_decode(
            hp, g, w_in_b, lng, lnb, l, depth, prompt_bufs,
            pt, paged, 0, n_seq_proj, dec_seq, n_pages, lam_init)
        prompt_bufs = (kbuf, vbuf)

        bsb = jnp.repeat(b_s[l].T, HEAD_W, axis=1)
        hp = _prompt_attn_mix(q, kbuf, vbuf, l, sga, sg, lams, lam_init, batch, seq,
                              hp, ugc, vn, w_s[l], bsb, w_out_b, fg, final)

        ws8 = w_s[l][:, :dec_seq, :dec_seq]
        ws_bd = (eye[None, :, None, :, None] * ws8[:, None, :, None, :]).reshape(N_HEADS, CHUNK, CHUNK)
        bsb = jnp.tile(jnp.repeat(b_s[l][:, :dec_seq].T, HEAD_W, axis=1), (CHUNK // dec_seq, 1))
        hs = _mix_out(hs, atts, ugcs, vns, ws_bd, bsb, w_out_b, l, fg, final)

    shape_p = (depth, batch, seq, N_HEADS, HEAD_W)
    shape_s = (depth, dec_batch, dec_seq, N_HEADS, HEAD_W)
    return (hp.reshape(batch, seq, D_MODEL),
            hs.reshape(dec_batch, dec_seq, D_MODEL),
            prompt_bufs[0].reshape(shape_p),
            prompt_bufs[1].reshape(shape_p),
            sample_bufs[0].reshape(shape_s),
            sample_bufs[1].reshape(shape_s),
            sample_bufs[2].reshape(shape_s))
```

```python
import functools
import math

import numpy as np
import jax
import jax.numpy as jnp
from jax import lax
from jax.experimental import pallas as pl
from jax.experimental.pallas import tpu as pltpu

F32 = jnp.float32
BF16 = jnp.bfloat16

D_MODEL = 1024
N_HEADS = 4
DK = 64
HEAD_W = 128
D_GRP = N_HEADS * HEAD_W
N_GRP = 7
CHUNK = 128
PAGE = 128
EPS = 1e-6
NEG = -1e30
ONES_ROWS = 16
POS_LANES = 128
SLOPE_PARTS = 3
LOG2E = np.float32(math.log2(math.e))

ROW_TILE = 512
ATT_BLK = 256
NEW_ROWS = 128
VMEM_LIMIT = 56 * 1024 * 1024


def _lambda_init(layer):
    return 0.8 - 0.6 * math.exp(-0.3 * layer)


def _alibi_slope(h):
    return 2.0 ** (-8.0 * (h + 1) / N_HEADS)


def _slope_parts(h):
    target = np.float32(_alibi_slope(h)) * LOG2E
    rest, parts = target, []
    for _ in range(SLOPE_PARTS):
        part = np.asarray(rest, dtype=BF16).astype(np.float32)
        parts.append(float(part))
        rest = np.float32(rest - part)
    assert rest == 0.0
    return parts


def _lam_value(lq1, lk1, lq2, lk2, lam_init):
    a = jnp.sum(lq1[...] * lk1[...], axis=-1, keepdims=True)
    b = jnp.sum(lq2[...] * lk2[...], axis=-1, keepdims=True)
    return jnp.exp(a) - jnp.exp(b) + lam_init


def _silu(x):
    return x * jax.nn.sigmoid(x)


def _head_rows(h, n_rows, first_row=0):
    return pl.ds(first_row * N_HEADS + h, n_rows, stride=N_HEADS)


def _inproj_stages(x_ref, g_ref, w_ref, lng_ref, lnb_ref, outs, with_vn_out):
    q_ref, k_ref, v_ref, sga_ref, ugc_ref, vn_ref = outs[:6]
    tm = x_ref.shape[0]

    normed = []

    def proj(i):
        if not normed:
            x = x_ref[...]
            ms = jnp.mean(x * x, axis=-1, keepdims=True)
            normed.append((x * lax.rsqrt(ms + EPS) * g_ref[...]).astype(BF16))
        return jnp.dot(normed[0], w_ref[:, i * D_GRP:(i + 1) * D_GRP], preferred_element_type=F32)

    def store_heads(ref, c, val):
        if len(ref.shape) == 3:
            for d in range(ref.shape[0]):
                ref[d, _head_rows(c, tm), :] = val
        else:
            ref[_head_rows(c, tm), :] = val

    def store_all_heads(ref, val):
        for c in range(N_HEADS):
            store_heads(ref, c, val[:, c * HEAD_W:(c + 1) * HEAD_W])

    def q_stage():
        q_ref[...] = (proj(0) * float(np.float32(DK ** -0.5) * LOG2E)).astype(q_ref.dtype)

    def k_stage():
        store_all_heads(k_ref, proj(1))

    def v_stage():
        store_all_heads(v_ref, proj(2))

    def gate_stage():
        sga_ref[...] = _silu(proj(3)).astype(sga_ref.dtype)

    def ugc_stage():
        ugc_ref[...] = (proj(4) * _silu(proj(6))).astype(ugc_ref.dtype)

    def vn_stage():
        vc = proj(5)
        for c in range(N_HEADS):
            sl = slice(c * HEAD_W, (c + 1) * HEAD_W)
            blk = vc[:, sl]
            mu = jnp.mean(blk, axis=-1, keepdims=True)
            xc = blk - mu
            var = jnp.mean(xc * xc, axis=-1, keepdims=True)
            vn = xc * lax.rsqrt(var + EPS) * lng_ref[:, sl] + lnb_ref[:, sl]
            vn_ref[:, sl] = vn.astype(vn_ref.dtype)
            if with_vn_out:
                store_heads(outs[6], c, vn)

    return [q_stage, k_stage, v_stage, gate_stage, ugc_stage, vn_stage]


def _inproj_kernel(*refs, n_alias, with_vn_out):
    outs = refs[5 + n_alias:]
    for stage in _inproj_stages(*refs[:5], outs, with_vn_out):
        stage()


def _inproj_out_specs(m, tm, act_dtype, layer, depth, first, with_vn_out, index):
    row = pl.BlockSpec((tm, D_GRP), lambda *i: (index(*i), 0))
    act = jax.ShapeDtypeStruct((m, D_GRP), act_dtype)
    stacked = jax.ShapeDtypeStruct((depth, m * N_HEADS, HEAD_W), F32)
    if first:
        stacked_spec = pl.BlockSpec((depth, tm * N_HEADS, HEAD_W), lambda *i: (0, index(*i), 0))
    else:
        stacked_spec = pl.BlockSpec((None, tm * N_HEADS, HEAD_W), lambda *i: (layer, index(*i), 0))
    extra = 1 if with_vn_out else 0
    out_shape = [act, stacked, stacked, act, act, act] + [stacked] * extra
    out_specs = [row, stacked_spec, stacked_spec, row, row, row] + [stacked_spec] * extra
    return out_shape, out_specs, [1, 2] + [6] * extra


def _inproj(x, g, w, lng, lnb, act_dtype, layer, depth, prev, with_vn_out):
    m = x.shape[0]
    tm = ROW_TILE
    const = lambda shape: pl.BlockSpec(shape, lambda i: (0, 0))
    out_shape, out_specs, stacked_idx = _inproj_out_specs(
        m, tm, act_dtype, layer, depth, prev is None, with_vn_out, lambda i: i)
    prev = list(prev) if prev is not None else []
    aliases = {5 + n: stacked_idx[n] for n in range(len(prev))}
    return pl.pallas_call(
        functools.partial(_inproj_kernel, n_alias=len(prev), with_vn_out=with_vn_out),
        grid=(m // tm,),
        in_specs=[pl.BlockSpec((tm, D_MODEL), lambda i: (i, 0)), const((1, D_MODEL)),
                  pl.BlockSpec((None, D_MODEL, N_GRP * D_GRP), lambda i: (layer, 0, 0)),
                  const((1, D_GRP)), const((1, D_GRP))]
                 + [pl.BlockSpec(memory_space=pl.ANY)] * len(prev),
        out_specs=out_specs,
        out_shape=out_shape,
        input_output_aliases=aliases,
        compiler_params=pltpu.CompilerParams(dimension_semantics=("parallel",),
                                             vmem_limit_bytes=VMEM_LIMIT),
        name="inproj",
    )(x, g, w, lng, lnb, *prev)


def _prompt_attn_kernel(pt_ref, q_ref, k_ref, v_ref, sga_ref, sg_ref, lq1, lk1, lq2, lk2,
                        x_ref, ugc_ref, vn_ref, ws_ref, bsb_ref, wout_ref, fg_ref, *rest,
                        lam_init, tb, nblk, final, first_seq, layer, n_pages, dec_seq):
    paged_in, rest = rest[:N_PAGED_IN], rest[N_PAGED_IN:]
    o_ref, att_s_ref, kb_sc, vt_sc, cm_sc, att_sc = rest[:6]
    m_sc = rest[6:6 + N_HEADS]
    acc_sc = rest[6 + N_HEADS:6 + 2 * N_HEADS]
    page_scratch = rest[6 + 2 * N_HEADS:]
    i = pl.program_id(1)

    paged = _Paged(pt_ref, pl.program_id(0) * nblk + i, pl.num_programs(0) * nblk, first_seq,
                   *paged_in, att_s_ref, *page_scratch, n_seq=1, layer=layer, lam_init=lam_init,
                   n_pages=n_pages, dec_seq=dec_seq, lookahead=False)
    paged.begin()

    @pl.when(i == 0)
    def _():
        pos = lax.broadcasted_iota(jnp.int32, (tb, POS_LANES), 0)
        lane = lax.broadcasted_iota(jnp.int32, (tb, POS_LANES), 1)
        for jb in range(nblk):
            kpos = pos + jb * tb
            hi = (kpos // PAGE) * PAGE
            aug = jnp.where(lane >= 2 * SLOPE_PARTS, 0, jnp.where(lane % 2 == 0, hi, kpos - hi))
            aug = aug.astype(F32).astype(BF16)
            for h in range(N_HEADS):
                rows = _head_rows(h, tb, jb * tb)
                kb_sc[h, jb * tb:(jb + 1) * tb, 0:HEAD_W] = k_ref[rows, :].astype(BF16)
                kb_sc[h, jb * tb:(jb + 1) * tb, HEAD_W:HEAD_W + POS_LANES] = aug
                vt_sc[h, jb, 0:HEAD_W, :] = v_ref[rows, :].T.astype(BF16)
                vt_sc[h, jb, HEAD_W:HEAD_W + ONES_ROWS, :] = jnp.ones((ONES_ROWS, tb), BF16)

    lane = lax.broadcasted_iota(jnp.int32, (tb, HEAD_W), 1)
    qz = []
    for h in range(N_HEADS):
        q = q_ref[:, h * HEAD_W:(h + 1) * HEAD_W]
        zero = jnp.zeros_like(q)
        slope = jnp.zeros((tb, POS_LANES), F32)
        for n, part in enumerate(_slope_parts(h)):
            slope = jnp.where(lane // 2 == n, part, slope)
        slope = slope.astype(BF16)
        q2 = jnp.concatenate([jnp.where(lane < DK, q, zero), jnp.where(lane >= DK, q, zero)], axis=0)
        qz.append(jnp.concatenate([q2, jnp.concatenate([slope, slope], axis=0)], axis=1))
        m_sc[h][...] = jnp.full(m_sc[h].shape, NEG, F32)
        acc_sc[h][...] = jnp.zeros(acc_sc[h].shape, F32)

    def steps(first_blk, nkb, masked):
        heads = range(N_HEADS)
        rows = pl.ds(pl.multiple_of(first_blk * tb, tb), nkb * tb)
        rider = paged.attend(0, False) if masked else iter(())
        s = [lax.dot_general(kb_sc[h, rows, :], qz[h], (((1,), (1,)), ((), ())),
                             preferred_element_type=F32) for h in heads]
        next(rider, None)
        if masked:
            key_i = lax.broadcasted_iota(jnp.int32, (nkb * tb, 2 * tb), 0)
            qry_i = lax.broadcasted_iota(jnp.int32, (nkb * tb, 2 * tb), 1)
            qry_i = jnp.where(qry_i >= tb, qry_i - tb, qry_i) + (nkb - 1) * tb
            s = [jnp.where(key_i <= qry_i, sh, NEG) for sh in s]
        m_old = [m_sc[h][...] for h in heads]
        m_new = [jnp.maximum(m_old[h], jnp.max(s[h], axis=0, keepdims=True)) for h in heads]
        p = [jnp.exp2(s[h] - m_new[h]).astype(BF16) for h in heads]
        next(rider, None)
        for h in heads:
            pv = jnp.dot(vt_sc[h, first_blk], p[h][0:tb, :], preferred_element_type=F32)
            for kb in range(1, nkb):
                pv = pv + jnp.dot(vt_sc[h, first_blk + kb], p[h][kb * tb:(kb + 1) * tb, :],
                                  preferred_element_type=F32)
            acc_sc[h][...] = acc_sc[h][...] * jnp.exp2(m_old[h] - m_new[h]) + pv
            m_sc[h][...] = m_new[h]
        for _ in rider:
            pass

    def body(jj, carry):
        steps(2 * jj, 2, False)
        return carry

    lax.fori_loop(0, i // 2, body, 0)
    paged.wait(0)

    @pl.when(i % 2 == 1)
    def _():
        steps(i - 1, 2, True)

    @pl.when(i % 2 == 0)
    def _():
        steps(i, 1, True)

    paged.request_next(0)

    y = _mix_chunk_part(x_ref, ugc_ref, vn_ref, ws_ref, bsb_ref, wout_ref, cm_sc)

    lam = _lam_value(lq1, lk1, lq2, lk2, lam_init)
    for h in range(N_HEADS):
        sl = slice(h * HEAD_W, (h + 1) * HEAD_W)
        acc = acc_sc[h]
        inv = 1.0 / acc[HEAD_W:HEAD_W + 1, :]
        o1 = acc[0:HEAD_W, 0:tb] * inv[:, 0:tb]
        o2 = acc[0:HEAD_W, tb:2 * tb] * inv[:, tb:2 * tb]
        o = (o1 - lam * o2).T
        ms = jnp.mean(o * o, axis=-1, keepdims=True)
        att = o * lax.rsqrt(ms + EPS) * sg_ref[...] * (1.0 - lam_init)
        att_sc[:, sl] = (att * sga_ref[:, sl].astype(F32)).astype(BF16)
    _mix_attn_part(y, att_sc[...], wout_ref, fg_ref, o_ref, final)


def _prompt_attn_mix(q, kbuf, vbuf, layer, sga, sg, lams, lam_init, batch, seq,
                     x, ugc, vn, ws, bsb, wout, fg, final, pt, paged, first_seq, dec_seq, n_pages):
    tb = ATT_BLK
    nblk = seq // tb
    rowspec = lambda width: pl.BlockSpec((tb, width), lambda b, i, pt: (b * nblk + i, 0))
    kvspec = pl.BlockSpec((None, seq * N_HEADS, HEAD_W), lambda b, i, pt: (layer, b, 0))
    const = lambda shape, **kw: pl.BlockSpec(shape, lambda b, i, pt: (0,) * len(shape), **kw)
    stats = ([pltpu.VMEM((1, 2 * tb), F32)] * N_HEADS
             + [pltpu.VMEM((HEAD_W + ONES_ROWS, 2 * tb), F32)] * N_HEADS)
    p_operands, p_specs, p_out_spec, p_out_shape, p_scratch = _paged_operands(
        *paged, layer, dec_seq, n_pages, first_seq, 1, batch * nblk, lambda b, i, pt: b * nblk + i, False)
    grid_spec = pltpu.PrefetchScalarGridSpec(
        num_scalar_prefetch=1,
        grid=(batch, nblk),
        in_specs=[rowspec(D_GRP), kvspec, kvspec, rowspec(D_GRP), const((1, HEAD_W))] + [const((1, DK))] * 4
                 + [rowspec(D_MODEL), rowspec(D_GRP), rowspec(D_GRP), const((N_HEADS, CHUNK, CHUNK)),
                    const((CHUNK, D_GRP)),
                    pl.BlockSpec((None, 2 * D_GRP, D_MODEL), lambda b, i, pt: (layer, 0, 0),
                                 pipeline_mode=pl.Buffered(1)),
                    const((1, D_MODEL))] + p_specs,
        out_specs=[rowspec(D_MODEL), p_out_spec],
        scratch_shapes=[pltpu.VMEM((N_HEADS, seq, HEAD_W + POS_LANES), BF16),
                        pltpu.VMEM((N_HEADS, nblk, HEAD_W + ONES_ROWS, tb), BF16),
                        pltpu.VMEM((tb, D_GRP), BF16), pltpu.VMEM((tb, D_GRP), BF16)] + stats + p_scratch,
    )
    return pl.pallas_call(
        functools.partial(_prompt_attn_kernel, lam_init=lam_init, tb=tb, nblk=nblk, final=final,
                          first_seq=first_seq, layer=layer, n_pages=n_pages, dec_seq=dec_seq),
        grid_spec=grid_spec,
        out_shape=[jax.ShapeDtypeStruct(x.shape, F32), p_out_shape],
        compiler_params=pltpu.CompilerParams(
            dimension_semantics=("arbitrary", "arbitrary"),
            vmem_limit_bytes=VMEM_LIMIT),
        name="prompt_attn_mix",
    )(pt, q, kbuf, vbuf, sga, sg, *lams, x, ugc, vn, ws, bsb, wout, fg, *p_operands)


def _sample_bias(n_pages, dec_seq):
    past = n_pages * PAGE
    n_pos = past + NEW_ROWS // N_HEADS
    t = np.arange(dec_seq)[:, None]
    kpos = np.arange(n_pos)[None, :]
    visible = (kpos < past) | (kpos - past <= t)
    bias = np.full((2, N_HEADS, dec_seq, n_pos, N_HEADS), NEG, np.float32)
    for h in range(N_HEADS):
        alibi = -np.float32(_alibi_slope(h)) * (past + t - kpos).astype(np.float32) * LOG2E
        bias[:, h, :, :, h] = np.where(visible, alibi, NEG)
    return bias.reshape(2 * N_HEADS * dec_seq, n_pos * N_HEADS)


class _Paged:
    def __init__(self, pt_ref, step, n_steps, first_seq, q_ref, kn_ref, vn_ref, sga_ref, bias_ref, sg_ref,
                 lq1, lk1, lq2, lk2, ck_hbm, cv_hbm, o_ref, kbuf, vbuf, ksem, vsem,
                 *, n_seq, layer, lam_init, n_pages, dec_seq, lookahead):
        assert n_seq == 1 or not lookahead
        self.__dict__.update({name: val for name, val in locals().items() if name != "self"})
        self.lam = _lam_value(lq1, lk1, lq2, lk2, lam_init)

    def seq_of(self, slot):
        return self.first_seq + self.n_seq * self.step + slot

    def buf_of(self, slot):
        return self.step % 2 if self.lookahead else slot

    def page_copies(self, seq_idx, buf):
        copies = []
        for j in range(self.n_pages):
            page = self.pt_ref[seq_idx * self.n_pages + j]
            copies.append(pltpu.make_async_copy(self.ck_hbm.at[self.layer, page], self.kbuf.at[buf, j],
                                                self.ksem.at[buf]))
            copies.append(pltpu.make_async_copy(self.cv_hbm.at[self.layer, page], self.vbuf.at[buf, j],
                                                self.vsem.at[buf]))
        return copies

    def begin(self):
        @pl.when(self.step == 0)
        def _():
            for slot in range(self.n_seq):
                for cp in self.page_copies(self.first_seq + slot, slot):
                    cp.start()

        if self.lookahead:
            @pl.when(self.step + 1 < self.n_steps)
            def _():
                for cp in self.page_copies(self.seq_of(0) + 1, (self.step + 1) % 2):
                    cp.start()

    def wait(self, slot):
        for cp in self.page_copies(self.seq_of(slot), self.buf_of(slot)):
            cp.wait()

    def request_next(self, slot):
        if not self.lookahead:
            @pl.when(self.step + 1 < self.n_steps)
            def _():
                for cp in self.page_copies(self.seq_of(slot) + self.n_seq, self.buf_of(slot)):
                    cp.start()

    def attend(self, slot, request_inside):
        dec_seq, n_pages = self.dec_seq, self.n_pages
        n_rows = 2 * N_HEADS * dec_seq
        half = N_HEADS * dec_seq
        page_rows = PAGE * N_HEADS
        past_rows = n_pages * page_rows
        new_rows = dec_seq * N_HEADS
        buf = self.buf_of(slot)
        kbuf, vbuf = self.kbuf, self.vbuf
        lane = lax.broadcasted_iota(jnp.int32, (half, HEAD_W), 1)
        pad = jnp.zeros((NEW_ROWS - new_rows, HEAD_W), F32)
        nt = (((1,), (1,)), ((), ()))
        rows = slice(slot * dec_seq, (slot + 1) * dec_seq)
        new = slice(slot * new_rows, (slot + 1) * new_rows)
        k_new = jnp.concatenate([self.kn_ref[new, :], pad], axis=0)
        v_new = jnp.concatenate([self.vn_ref[new, :], pad], axis=0)
        qh = jnp.concatenate([self.q_ref[rows, h * HEAD_W:(h + 1) * HEAD_W] for h in range(N_HEADS)], axis=0)
        qrows = jnp.concatenate([jnp.where(lane < DK, qh, 0.0), jnp.where(lane >= DK, qh, 0.0)], axis=0)
        s = [lax.dot_general(qrows, kbuf[buf, j], nt, preferred_element_type=F32) for j in range(n_pages)]
        s.append(lax.dot_general(qrows, k_new, nt, preferred_element_type=F32))
        s = jnp.concatenate(s, axis=1) + self.bias_ref[...]
        yield
        p = jnp.exp2(s - jnp.max(s, axis=-1, keepdims=True))
        inv = 1.0 / jnp.sum(p, axis=-1, keepdims=True)
        yield
        pv = jnp.dot(p[:, past_rows:], v_new, preferred_element_type=F32)
        for j in range(n_pages):
            pv = pv + jnp.dot(p[:, j * page_rows:(j + 1) * page_rows], vbuf[buf, j],
                              preferred_element_type=F32)
        res = pv[0:half, :] * inv[0:half, :] - pv[half:n_rows, :] * (self.lam * inv[half:n_rows, :])
        if request_inside:
            self.request_next(slot)
        yield
        for h in range(N_HEADS):
            sl = slice(h * HEAD_W, (h + 1) * HEAD_W)
            o = res[h * dec_seq:(h + 1) * dec_seq, :]
            ms = jnp.mean(o * o, axis=-1, keepdims=True)
            att = o * lax.rsqrt(ms + EPS) * self.sg_ref[...] * (1.0 - self.lam_init)
            self.o_ref[rows, sl] = (att * self.sga_ref[rows, sl]).astype(self.o_ref.dtype)

    def run(self, fillers):
        self.begin()
        for slot in range(self.n_seq):
            self.wait(slot)
            pending = list(fillers[slot])
            for _ in self.attend(slot, True):
                if pending:
                    pending.pop(0)()
            for filler in pending:
                filler()


N_PAGED_IN = 12


def _paged_operands(qs, knbuf, vnbuf, sgas, cache_k, cache_v, sg, lams, layer, dec_seq, n_pages,
                    first_seq, n_seq, n_steps, step_of, lookahead):
    bias = jnp.asarray(_sample_bias(n_pages, dec_seq))
    first_blk = first_seq // n_seq
    const = lambda shape, **kw: pl.BlockSpec(shape, lambda *i: (0, 0), **kw)
    seqspec = pl.BlockSpec((n_seq * dec_seq, D_GRP), lambda *i: (first_blk + step_of(*i), 0))
    newspec = pl.BlockSpec((None, n_seq * dec_seq * N_HEADS, HEAD_W),
                           lambda *i: (layer, first_blk + step_of(*i), 0))
    hbm = pl.BlockSpec(memory_space=pl.ANY)
    operands = [qs, knbuf, vnbuf, sgas, bias, sg, *lams, cache_k, cache_v]
    in_specs = ([seqspec, newspec, newspec, seqspec, const(bias.shape, pipeline_mode=pl.Buffered(1)),
                 const((1, HEAD_W))] + [const((1, DK))] * 4 + [hbm, hbm])
    assert len(operands) == len(in_specs) == N_PAGED_IN
    out_spec = pl.BlockSpec((n_seq * dec_seq, D_GRP), lambda *i: (step_of(*i), 0))
    out_shape = jax.ShapeDtypeStruct((n_steps * n_seq * dec_seq, D_GRP), F32)
    n_bufs = 2 if lookahead else n_seq
    page_buf = pltpu.VMEM((n_bufs, n_pages, PAGE * N_HEADS, HEAD_W), F32)
    scratch = [page_buf, page_buf, pltpu.SemaphoreType.DMA((n_bufs,)), pltpu.SemaphoreType.DMA((n_bufs,))]
    return operands, in_specs, out_spec, out_shape, scratch


def _inproj_decode_kernel(pt_ref, x_ref, g_ref, w_ref, lng_ref, lnb_ref, *rest,
                          n_alias, n_seq, first_seq, layer, lam_init, n_pages, dec_seq):
    paged_in, rest = rest[:N_PAGED_IN], rest[N_PAGED_IN + n_alias:]
    outs, att_ref, scratch = rest[:6], rest[6], rest[7:]
    stages = _inproj_stages(x_ref, g_ref, w_ref, lng_ref, lnb_ref, outs, False)
    per_slot = -(-len(stages) // n_seq)
    stages = stages[:per_slot] + stages[per_slot:][::-1]
    fillers = [stages[slot * per_slot:(slot + 1) * per_slot] for slot in range(n_seq)]
    _Paged(pt_ref, pl.program_id(0), pl.num_programs(0), first_seq, *paged_in, att_ref, *scratch,
           n_seq=n_seq, layer=layer, lam_init=lam_init, n_pages=n_pages, dec_seq=dec_seq,
           lookahead=n_seq == 1).run(fillers)


def _inproj_decode(x, g, w, lng, lnb, layer, depth, prev, pt, paged, first_seq, n_seq, dec_seq, n_pages,
                   lam_init):
    m = x.shape[0]
    tm = ROW_TILE // 2
    n_steps = m // tm
    const = lambda shape, **kw: pl.BlockSpec(shape, lambda t, pt: (0, 0), **kw)
    once = dict(pipeline_mode=pl.Buffered(1))
    p_operands, p_specs, p_out_spec, p_out_shape, p_scratch = _paged_operands(
        *paged, layer, dec_seq, n_pages, first_seq, n_seq, n_steps, lambda t, pt: t, n_seq == 1)
    out_shape, out_specs, stacked_idx = _inproj_out_specs(
        m, tm, BF16, layer, depth, prev is None, False, lambda t, pt: t)
    prev = list(prev) if prev is not None else []
    n_inputs = 6 + N_PAGED_IN
    aliases = {n_inputs + i: stacked_idx[i] for i in range(len(prev))}
    grid_spec = pltpu.PrefetchScalarGridSpec(
        num_scalar_prefetch=1,
        grid=(n_steps,),
        in_specs=[pl.BlockSpec((tm, D_MODEL), lambda t, pt: (t, 0)), const((1, D_MODEL)),
                  pl.BlockSpec((None, D_MODEL, N_GRP * D_GRP), lambda t, pt: (layer, 0, 0), **once),
                  const((1, D_GRP)), const((1, D_GRP))]
                 + p_specs + [pl.BlockSpec(memory_space=pl.ANY)] * len(prev),
        out_specs=out_specs + [p_out_spec],
        scratch_shapes=p_scratch,
    )
    return pl.pallas_call(
        functools.partial(_inproj_decode_kernel, n_alias=len(prev), n_seq=n_seq, first_seq=first_seq,
                          layer=layer, lam_init=lam_init, n_pages=n_pages, dec_seq=dec_seq),
        grid_spec=grid_spec,
        out_shape=out_shape + [p_out_shape],
        input_output_aliases=aliases,
        compiler_params=pltpu.CompilerParams(dimension_semantics=("arbitrary",),
                                             vmem_limit_bytes=VMEM_LIMIT),
        name="inproj_decode",
    )(pt, x, g, w, lng, lnb, *p_operands, *prev)


def _mix_chunk_part(x_ref, ugc_ref, vn_ref, ws_ref, bsb_ref, wout_ref, cm_sc):
    tm = x_ref.shape[0]
    nc = tm // CHUNK
    r = lax.broadcasted_iota(jnp.int32, (CHUNK, CHUNK), 0)
    c = lax.broadcasted_iota(jnp.int32, (CHUNK, CHUNK), 1)
    for h in range(N_HEADS):
        sl = slice(h * HEAD_W, (h + 1) * HEAD_W)
        ws = jnp.where(c <= r, ws_ref[h], 0.0).astype(BF16)
        vn = jnp.concatenate([vn_ref[ci * CHUNK:(ci + 1) * CHUNK, sl].astype(BF16)
                              for ci in range(nc)], axis=1)
        mixed = jnp.dot(ws, vn, preferred_element_type=F32)
        for ci in range(nc):
            rows = slice(ci * CHUNK, (ci + 1) * CHUNK)
            mc = mixed[:, ci * HEAD_W:(ci + 1) * HEAD_W] + bsb_ref[:, sl]
            cm_sc[rows, sl] = (ugc_ref[rows, sl].astype(F32) * mc).astype(BF16)
    return x_ref[...] + jnp.dot(cm_sc[...], wout_ref[D_GRP:2 * D_GRP, :], preferred_element_type=F32)


def _mix_attn_part(y, att, wout_ref, fg_ref, o_ref, final):
    y = y + jnp.dot(att, wout_ref[0:D_GRP, :], preferred_element_type=F32)
    if final:
        ms = jnp.mean(y * y, axis=-1, keepdims=True)
        y = y * lax.rsqrt(ms + EPS) * fg_ref[...]
    o_ref[...] = y


def _mix_out_kernel(x_ref, att_ref, ugc_ref, vn_ref, ws_ref, bsb_ref, wout_ref, fg_ref,
                    o_ref, cm_sc, *, final):
    y = _mix_chunk_part(x_ref, ugc_ref, vn_ref, ws_ref, bsb_ref, wout_ref, cm_sc)
    _mix_attn_part(y, att_ref[...].astype(BF16), wout_ref, fg_ref, o_ref, final)


def _mix_out(x, att, ugc, vn, ws, bsb, wout, layer, fg, final):
    m = x.shape[0]
    tm = ROW_TILE
    row = lambda width: pl.BlockSpec((tm, width), lambda i: (i, 0))
    return pl.pallas_call(
        functools.partial(_mix_out_kernel, final=final),
        grid=(m // tm,),
        in_specs=[row(D_MODEL), row(D_GRP), row(D_GRP), row(D_GRP),
                  pl.BlockSpec((N_HEADS, CHUNK, CHUNK), lambda i: (0, 0, 0)),
                  pl.BlockSpec((CHUNK, D_GRP), lambda i: (0, 0)),
                  pl.BlockSpec((None, 2 * D_GRP, D_MODEL), lambda i: (layer, 0, 0)),
                  pl.BlockSpec((1, D_MODEL), lambda i: (0, 0))],
        out_specs=row(D_MODEL),
        out_shape=jax.ShapeDtypeStruct((m, D_MODEL), F32),
        scratch_shapes=[pltpu.VMEM((tm, D_GRP), BF16)],
        compiler_params=pltpu.CompilerParams(dimension_semantics=("parallel",),
                                             vmem_limit_bytes=VMEM_LIMIT),
        name="mix_out",
    )(x, att, ugc, vn, ws, bsb, wout, fg)


def kernel(x_prompt, x_sample, cache_k, cache_v, page_table, norm_g, w_in, lam_q1, lam_k1,
           lam_q2, lam_k2, subln_g, ln_v_g, ln_v_b, w_s, b_s, w_out, final_g):
    batch, seq, _ = x_prompt.shape
    dec_batch, dec_seq, _ = x_sample.shape
    depth, n_pool = cache_k.shape[0], cache_k.shape[1]
    n_pages = page_table.shape[1]
    assert seq % ATT_BLK == 0 and (batch * seq) % ROW_TILE == 0 and (dec_batch * dec_seq) % ROW_TILE == 0
    assert CHUNK % dec_seq == 0 and dec_seq % 8 == 0 and ATT_BLK % PAGE == 0
    assert dec_seq * N_HEADS <= NEW_ROWS
    n_seq_attn = batch * (seq // ATT_BLK)
    proj_steps = (batch * seq) // (ROW_TILE // 2)
    assert dec_batch > n_seq_attn and (dec_batch - n_seq_attn) % proj_steps == 0
    n_seq_proj = (dec_batch - n_seq_attn) // proj_steps

    hp = x_prompt.reshape(batch * seq, D_MODEL)
    hs = x_sample.reshape(dec_batch * dec_seq, D_MODEL)
    ck = cache_k.reshape(depth, n_pool, PAGE * N_HEADS, HEAD_W)
    cv = cache_v.reshape(depth, n_pool, PAGE * N_HEADS, HEAD_W)
    pt = page_table.reshape(-1)
    w_in_b = w_in.astype(BF16)
    w_out_b = w_out.astype(BF16)
    fg = final_g.reshape(1, D_MODEL)
    eye = jnp.eye(CHUNK // dec_seq, dtype=F32)

    prompt_bufs, sample_bufs = None, None
    for l in range(depth):
        lam_init = _lambda_init(l)
        g = norm_g[l].reshape(1, D_MODEL)
        lng = ln_v_g[l].reshape(1, D_GRP)
        lnb = ln_v_b[l].reshape(1, D_GRP)
        sg = subln_g[l].reshape(1, HEAD_W)
        lams = [a[l].reshape(1, DK) for a in (lam_q1, lam_k1, lam_q2, lam_k2)]
        final = l == depth - 1

        qs, ksbuf, vsbuf, sgas, ugcs, vns, vnsbuf = _inproj(hs, g, w_in_b, lng, lnb, F32, l, depth,
                                                            sample_bufs, True)
        sample_bufs = (ksbuf, vsbuf, vnsbuf)
        paged = (qs, ksbuf, vsbuf, sgas, ck, cv, sg, lams)
        q, kbuf, vbuf, sga, ugc, vn, atts_proj = _inproj_decode(
            hp, g, w_in_b, lng, lnb, l, depth, prompt_bufs,
            pt, paged, 0, n_seq_proj, dec_seq, n_pages, lam_init)
        prompt_bufs = (kbuf, vbuf)

        bsb = jnp.repeat(b_s[l].T, HEAD_W, axis=1)
        hp, atts_attn = _prompt_attn_mix(q, kbuf, vbuf, l, sga, sg, lams, lam_init, batch, seq,
                                         hp, ugc, vn, w_s[l], bsb, w_out_b, fg, final,
                                         pt, paged, dec_batch - n_seq_attn, dec_seq, n_pages)
        atts = jnp.concatenate([atts_proj, atts_attn], axis=0)

        ws8 = w_s[l][:, :dec_seq, :dec_seq]
        ws_bd = (eye[None, :, None, :, None] * ws8[:, None, :, None, :]).reshape(N_HEADS, CHUNK, CHUNK)
        bsb = jnp.tile(jnp.repeat(b_s[l][:, :dec_seq].T, HEAD_W, axis=1), (CHUNK // dec_seq, 1))
        hs = _mix_out(hs, atts, ugcs, vns, ws_bd, bsb, w_out_b, l, fg, final)

    shape_p = (depth, batch, seq, N_HEADS, HEAD_W)
    shape_s = (depth, dec_batch, dec_seq, N_HEADS, HEAD_W)
    return (hp.reshape(batch, seq, D_MODEL),
            hs.reshape(dec_batch, dec_seq, D_MODEL),
            prompt_bufs[0].reshape(shape_p),
            prompt_bufs[1].reshape(shape_p),
            sample_bufs[0].reshape(shape_s),
            sample_bufs[1].reshape(shape_s),
            sample_bufs[2].reshape(shape_s))
```

```python
import functools
import math

import numpy as np
import jax
import jax.numpy as jnp
from jax import lax
from jax.experimental import pallas as pl
from jax.experimental.pallas import tpu as pltpu

F32 = jnp.float32
BF16 = jnp.bfloat16

D_MODEL = 1024
N_HEADS = 4
DK = 64
HEAD_W = 128
D_GRP = N_HEADS * HEAD_W
N_GRP = 7
CHUNK = 128
PAGE = 128
EPS = 1e-6
NEG = -1e30
ONES_ROWS = 16
POS_LANES = 128
SLOPE_PARTS = 3
LOG2E = np.float32(math.log2(math.e))

ROW_TILE = 512
ATT_BLK = 256
NEW_ROWS = 128
N_DMA_QUEUES = 2
VMEM_LIMIT = 56 * 1024 * 1024


def _lambda_init(layer):
    return 0.8 - 0.6 * math.exp(-0.3 * layer)


def _alibi_slope(h):
    return 2.0 ** (-8.0 * (h + 1) / N_HEADS)


def _slope_parts(h):
    target = np.float32(_alibi_slope(h)) * LOG2E
    rest, parts = target, []
    for _ in range(SLOPE_PARTS):
        part = np.asarray(rest, dtype=BF16).astype(np.float32)
        parts.append(float(part))
        rest = np.float32(rest - part)
    assert rest == 0.0
    return parts


def _lam_value(lq1, lk1, lq2, lk2, lam_init):
    a = jnp.sum(lq1[...] * lk1[...], axis=-1, keepdims=True)
    b = jnp.sum(lq2[...] * lk2[...], axis=-1, keepdims=True)
    return jnp.exp(a) - jnp.exp(b) + lam_init


def _silu(x):
    return x * jax.nn.sigmoid(x)


def _head_rows(h, n_rows, first_row=0):
    return pl.ds(first_row * N_HEADS + h, n_rows, stride=N_HEADS)


def _inproj_stages(x_ref, g_ref, w_ref, lng_ref, lnb_ref, outs, with_vn_out):
    q_ref, k_ref, v_ref, sga_ref, ugc_ref, vn_ref = outs[:6]
    tm = x_ref.shape[0]

    normed = []

    def proj(i):
        if not normed:
            x = x_ref[...]
            ms = jnp.mean(x * x, axis=-1, keepdims=True)
            normed.append((x * lax.rsqrt(ms + EPS) * g_ref[...]).astype(BF16))
        return jnp.dot(normed[0], w_ref[:, i * D_GRP:(i + 1) * D_GRP], preferred_element_type=F32)

    def store_heads(ref, c, val):
        if len(ref.shape) == 3:
            for d in range(ref.shape[0]):
                ref[d, _head_rows(c, tm), :] = val
        else:
            ref[_head_rows(c, tm), :] = val

    def store_all_heads(ref, val):
        for c in range(N_HEADS):
            store_heads(ref, c, val[:, c * HEAD_W:(c + 1) * HEAD_W])

    def q_stage():
        q_ref[...] = (proj(0) * float(np.float32(DK ** -0.5) * LOG2E)).astype(q_ref.dtype)

    def k_stage():
        store_all_heads(k_ref, proj(1))

    def v_stage():
        store_all_heads(v_ref, proj(2))

    def gate_stage():
        sga_ref[...] = _silu(proj(3)).astype(sga_ref.dtype)

    def ugc_stage():
        ugc_ref[...] = (proj(4) * _silu(proj(6))).astype(ugc_ref.dtype)

    def vn_stage():
        vc = proj(5)
        for c in range(N_HEADS):
            sl = slice(c * HEAD_W, (c + 1) * HEAD_W)
            blk = vc[:, sl]
            mu = jnp.mean(blk, axis=-1, keepdims=True)
            xc = blk - mu
            var = jnp.mean(xc * xc, axis=-1, keepdims=True)
            vn = xc * lax.rsqrt(var + EPS) * lng_ref[:, sl] + lnb_ref[:, sl]
            vn_ref[:, sl] = vn.astype(vn_ref.dtype)
            if with_vn_out:
                store_heads(outs[6], c, vn)

    return [q_stage, k_stage, v_stage, gate_stage, ugc_stage, vn_stage]


def _inproj_kernel(*refs, n_alias, with_vn_out):
    outs = refs[5 + n_alias:]
    for stage in _inproj_stages(*refs[:5], outs, with_vn_out):
        stage()


def _inproj_out_specs(m, tm, act_dtype, layer, depth, first, with_vn_out, index):
    row = pl.BlockSpec((tm, D_GRP), lambda *i: (index(*i), 0))
    act = jax.ShapeDtypeStruct((m, D_GRP), act_dtype)
    stacked = jax.ShapeDtypeStruct((depth, m * N_HEADS, HEAD_W), F32)
    if first:
        stacked_spec = pl.BlockSpec((depth, tm * N_HEADS, HEAD_W), lambda *i: (0, index(*i), 0))
    else:
        stacked_spec = pl.BlockSpec((None, tm * N_HEADS, HEAD_W), lambda *i: (layer, index(*i), 0))
    extra = 1 if with_vn_out else 0
    out_shape = [act, stacked, stacked, act, act, act] + [stacked] * extra
    out_specs = [row, stacked_spec, stacked_spec, row, row, row] + [stacked_spec] * extra
    return out_shape, out_specs, [1, 2] + [6] * extra


def _inproj(x, g, w, lng, lnb, act_dtype, layer, depth, prev, with_vn_out):
    m = x.shape[0]
    tm = ROW_TILE
    const = lambda shape: pl.BlockSpec(shape, lambda i: (0, 0))
    out_shape, out_specs, stacked_idx = _inproj_out_specs(
        m, tm, act_dtype, layer, depth, prev is None, with_vn_out, lambda i: i)
    prev = list(prev) if prev is not None else []
    aliases = {5 + n: stacked_idx[n] for n in range(len(prev))}
    return pl.pallas_call(
        functools.partial(_inproj_kernel, n_alias=len(prev), with_vn_out=with_vn_out),
        grid=(m // tm,),
        in_specs=[pl.BlockSpec((tm, D_MODEL), lambda i: (i, 0)), const((1, D_MODEL)),
                  pl.BlockSpec((None, D_MODEL, N_GRP * D_GRP), lambda i: (layer, 0, 0)),
                  const((1, D_GRP)), const((1, D_GRP))]
                 + [pl.BlockSpec(memory_space=pl.ANY)] * len(prev),
        out_specs=out_specs,
        out_shape=out_shape,
        input_output_aliases=aliases,
        compiler_params=pltpu.CompilerParams(dimension_semantics=("parallel",),
                                             vmem_limit_bytes=VMEM_LIMIT),
        name="inproj",
    )(x, g, w, lng, lnb, *prev)


def _prompt_attn_kernel(q_ref, k_ref, v_ref, sga_ref, sg_ref, lq1, lk1, lq2, lk2,
                        x_ref, ugc_ref, vn_ref, ws_ref, bsb_ref, wout_ref, fg_ref,
                        o_ref, kb_sc, vt_sc, cm_sc, att_sc, *stat_sc, lam_init, tb, nblk, final):
    m_sc = stat_sc[:N_HEADS]
    acc_sc = stat_sc[N_HEADS:]
    i = pl.program_id(1)

    @pl.when(i == 0)
    def _():
        pos = lax.broadcasted_iota(jnp.int32, (tb, POS_LANES), 0)
        lane = lax.broadcasted_iota(jnp.int32, (tb, POS_LANES), 1)
        for jb in range(nblk):
            kpos = pos + jb * tb
            hi = (kpos // PAGE) * PAGE
            aug = jnp.where(lane >= 2 * SLOPE_PARTS, 0, jnp.where(lane % 2 == 0, hi, kpos - hi))
            aug = aug.astype(F32).astype(BF16)
            for h in range(N_HEADS):
                rows = _head_rows(h, tb, jb * tb)
                kb_sc[h, jb * tb:(jb + 1) * tb, 0:HEAD_W] = k_ref[rows, :].astype(BF16)
                kb_sc[h, jb * tb:(jb + 1) * tb, HEAD_W:HEAD_W + POS_LANES] = aug
                vt_sc[h, jb, 0:HEAD_W, :] = v_ref[rows, :].T.astype(BF16)
                vt_sc[h, jb, HEAD_W:HEAD_W + ONES_ROWS, :] = jnp.ones((ONES_ROWS, tb), BF16)

    lane = lax.broadcasted_iota(jnp.int32, (tb, HEAD_W), 1)
    qz = []
    for h in range(N_HEADS):
        q = q_ref[:, h * HEAD_W:(h + 1) * HEAD_W]
        zero = jnp.zeros_like(q)
        slope = jnp.zeros((tb, POS_LANES), F32)
        for n, part in enumerate(_slope_parts(h)):
            slope = jnp.where(lane // 2 == n, part, slope)
        slope = slope.astype(BF16)
        q2 = jnp.concatenate([jnp.where(lane < DK, q, zero), jnp.where(lane >= DK, q, zero)], axis=0)
        qz.append(jnp.concatenate([q2, jnp.concatenate([slope, slope], axis=0)], axis=1))
        m_sc[h][...] = jnp.full(m_sc[h].shape, NEG, F32)
        acc_sc[h][...] = jnp.zeros(acc_sc[h].shape, F32)

    def steps(first_blk, nkb, masked):
        heads = range(N_HEADS)
        rows = pl.ds(pl.multiple_of(first_blk * tb, tb), nkb * tb)
        s = [lax.dot_general(kb_sc[h, rows, :], qz[h], (((1,), (1,)), ((), ())),
                             preferred_element_type=F32) for h in heads]
        if masked:
            key_i = lax.broadcasted_iota(jnp.int32, (nkb * tb, 2 * tb), 0)
            qry_i = lax.broadcasted_iota(jnp.int32, (nkb * tb, 2 * tb), 1)
            qry_i = jnp.where(qry_i >= tb, qry_i - tb, qry_i) + (nkb - 1) * tb
            s = [jnp.where(key_i <= qry_i, sh, NEG) for sh in s]
        m_old = [m_sc[h][...] for h in heads]
        m_new = [jnp.maximum(m_old[h], jnp.max(s[h], axis=0, keepdims=True)) for h in heads]
        p = [jnp.exp2(s[h] - m_new[h]).astype(BF16) for h in heads]
        for h in heads:
            pv = jnp.dot(vt_sc[h, first_blk], p[h][0:tb, :], preferred_element_type=F32)
            for kb in range(1, nkb):
                pv = pv + jnp.dot(vt_sc[h, first_blk + kb], p[h][kb * tb:(kb + 1) * tb, :],
                                  preferred_element_type=F32)
            acc_sc[h][...] = acc_sc[h][...] * jnp.exp2(m_old[h] - m_new[h]) + pv
            m_sc[h][...] = m_new[h]

    def body(jj, carry):
        steps(2 * jj, 2, False)
        return carry

    lax.fori_loop(0, i // 2, body, 0)

    @pl.when(i % 2 == 1)
    def _():
        steps(i - 1, 2, True)

    @pl.when(i % 2 == 0)
    def _():
        steps(i, 1, True)

    y = _mix_chunk_part(x_ref, ugc_ref, vn_ref, ws_ref, bsb_ref, wout_ref, cm_sc)

    lam = _lam_value(lq1, lk1, lq2, lk2, lam_init)
    for h in range(N_HEADS):
        sl = slice(h * HEAD_W, (h + 1) * HEAD_W)
        acc = acc_sc[h]
        inv = 1.0 / acc[HEAD_W:HEAD_W + 1, :]
        o1 = acc[0:HEAD_W, 0:tb] * inv[:, 0:tb]
        o2 = acc[0:HEAD_W, tb:2 * tb] * inv[:, tb:2 * tb]
        o = (o1 - lam * o2).T
        ms = jnp.mean(o * o, axis=-1, keepdims=True)
        att = o * lax.rsqrt(ms + EPS) * sg_ref[...] * (1.0 - lam_init)
        att_sc[:, sl] = (att * sga_ref[:, sl].astype(F32)).astype(BF16)
    _mix_attn_part(y, att_sc[...], wout_ref, fg_ref, o_ref, final)


def _prompt_attn_mix(q, kbuf, vbuf, layer, sga, sg, lams, lam_init, batch, seq,
                     x, ugc, vn, ws, bsb, wout, fg, final):
    tb = ATT_BLK
    nblk = seq // tb
    rowspec = lambda width: pl.BlockSpec((tb, width), lambda b, i: (b * nblk + i, 0))
    kvspec = pl.BlockSpec((None, seq * N_HEADS, HEAD_W), lambda b, i: (layer, b, 0))
    const = lambda shape, **kw: pl.BlockSpec(shape, lambda b, i: (0,) * len(shape), **kw)
    stats = ([pltpu.VMEM((1, 2 * tb), F32)] * N_HEADS
             + [pltpu.VMEM((HEAD_W + ONES_ROWS, 2 * tb), F32)] * N_HEADS)
    return pl.pallas_call(
        functools.partial(_prompt_attn_kernel, lam_init=lam_init, tb=tb, nblk=nblk, final=final),
        grid=(batch, nblk),
        in_specs=[rowspec(D_GRP), kvspec, kvspec, rowspec(D_GRP), const((1, HEAD_W))] + [const((1, DK))] * 4
                 + [rowspec(D_MODEL), rowspec(D_GRP), rowspec(D_GRP), const((N_HEADS, CHUNK, CHUNK)),
                    const((CHUNK, D_GRP)),
                    pl.BlockSpec((None, 2 * D_GRP, D_MODEL), lambda b, i: (layer, 0, 0),
                                 pipeline_mode=pl.Buffered(1)),
                    const((1, D_MODEL))],
        out_specs=rowspec(D_MODEL),
        out_shape=jax.ShapeDtypeStruct(x.shape, F32),
        scratch_shapes=[pltpu.VMEM((N_HEADS, seq, HEAD_W + POS_LANES), BF16),
                        pltpu.VMEM((N_HEADS, nblk, HEAD_W + ONES_ROWS, tb), BF16),
                        pltpu.VMEM((tb, D_GRP), BF16), pltpu.VMEM((tb, D_GRP), BF16)] + stats,
        compiler_params=pltpu.CompilerParams(
            dimension_semantics=("parallel", "arbitrary"),
            vmem_limit_bytes=VMEM_LIMIT),
        name="prompt_attn_mix",
    )(q, kbuf, vbuf, sga, sg, *lams, x, ugc, vn, ws, bsb, wout, fg)


def _sample_bias(n_pages, dec_seq):
    past = n_pages * PAGE
    n_pos = past + NEW_ROWS // N_HEADS
    t = np.arange(dec_seq)[:, None]
    kpos = np.arange(n_pos)[None, :]
    visible = (kpos < past) | (kpos - past <= t)
    bias = np.full((2, N_HEADS, dec_seq, n_pos, N_HEADS), NEG, np.float32)
    for h in range(N_HEADS):
        alibi = -np.float32(_alibi_slope(h)) * (past + t - kpos).astype(np.float32) * LOG2E
        bias[:, h, :, :, h] = np.where(visible, alibi, NEG)
    return bias.reshape(2 * N_HEADS * dec_seq, n_pos * N_HEADS)


def _paged_attention(pt_ref, step, n_steps, first_seq, q_ref, kn_ref, vn_ref, sga_ref, bias_ref, sg_ref,
                     lq1, lk1, lq2, lk2, ck_hbm, cv_hbm, o_ref, kbuf, vbuf, ksem, vsem,
                     *, layer, lam_init, n_pages, dec_seq, fillers):
    n_seq = len(fillers)
    n_rows = 2 * N_HEADS * dec_seq
    half = N_HEADS * dec_seq
    page_rows = PAGE * N_HEADS
    past_rows = n_pages * page_rows
    new_rows = dec_seq * N_HEADS
    slots = range(n_seq)

    def page_copies(seq_idx, slot):
        copies = []
        for j in range(n_pages):
            page = pt_ref[seq_idx * n_pages + j]
            copies.append(pltpu.make_async_copy(ck_hbm.at[layer, page], kbuf.at[slot, j], ksem.at[slot]))
            copies.append(pltpu.make_async_copy(cv_hbm.at[layer, page], vbuf.at[slot, j], vsem.at[slot]))
        return copies

    def start_pages(seq_idx, slot):
        for n, cp in enumerate(page_copies(seq_idx, slot)):
            cp.start(priority=n % N_DMA_QUEUES)

    @pl.when(step == 0)
    def _():
        for slot in slots:
            start_pages(first_seq + slot, slot)

    lane = lax.broadcasted_iota(jnp.int32, (half, HEAD_W), 1)
    lam = _lam_value(lq1, lk1, lq2, lk2, lam_init)
    pad = jnp.zeros((NEW_ROWS - new_rows, HEAD_W), F32)
    nt = (((1,), (1,)), ((), ()))

    def attend(slot, seq_idx):
        rows = slice(slot * dec_seq, (slot + 1) * dec_seq)
        new = slice(slot * new_rows, (slot + 1) * new_rows)
        k_new = jnp.concatenate([kn_ref[new, :], pad], axis=0)
        v_new = jnp.concatenate([vn_ref[new, :], pad], axis=0)
        qh = jnp.concatenate([q_ref[rows, h * HEAD_W:(h + 1) * HEAD_W] for h in range(N_HEADS)], axis=0)
        qrows = jnp.concatenate([jnp.where(lane < DK, qh, 0.0), jnp.where(lane >= DK, qh, 0.0)], axis=0)
        s = [lax.dot_general(qrows, kbuf[slot, j], nt, preferred_element_type=F32) for j in range(n_pages)]
        s.append(lax.dot_general(qrows, k_new, nt, preferred_element_type=F32))
        s = jnp.concatenate(s, axis=1) + bias_ref[...]
        yield
        p = jnp.exp2(s - jnp.max(s, axis=-1, keepdims=True))
        inv = 1.0 / jnp.sum(p, axis=-1, keepdims=True)
        yield
        pv = jnp.dot(p[:, past_rows:], v_new, preferred_element_type=F32)
        for j in range(n_pages):
            pv = pv + jnp.dot(p[:, j * page_rows:(j + 1) * page_rows], vbuf[slot, j],
                              preferred_element_type=F32)
        res = pv[0:half, :] * inv[0:half, :] - pv[half:n_rows, :] * (lam * inv[half:n_rows, :])

        @pl.when(step + 1 < n_steps)
        def _():
            start_pages(seq_idx + n_seq, slot)

        yield
        for h in range(N_HEADS):
            sl = slice(h * HEAD_W, (h + 1) * HEAD_W)
            o = res[h * dec_seq:(h + 1) * dec_seq, :]
            ms = jnp.mean(o * o, axis=-1, keepdims=True)
            att = o * lax.rsqrt(ms + EPS) * sg_ref[...] * (1.0 - lam_init)
            o_ref[rows, sl] = (att * sga_ref[rows, sl]).astype(o_ref.dtype)

    for slot in slots:
        seq_idx = first_seq + n_seq * step + slot
        for cp in page_copies(seq_idx, slot):
            cp.wait()
        pending = list(fillers[slot])
        for _ in attend(slot, seq_idx):
            if pending:
                pending.pop(0)()
        for filler in pending:
            filler()


N_PAGED_IN = 12


def _paged_operands(qs, knbuf, vnbuf, sgas, cache_k, cache_v, sg, lams, layer, dec_seq, n_pages,
                    first_seq, n_seq, n_steps, step_of):
    bias = jnp.asarray(_sample_bias(n_pages, dec_seq))
    first_blk = first_seq // n_seq
    const = lambda shape, **kw: pl.BlockSpec(shape, lambda *i: (0, 0), **kw)
    seqspec = pl.BlockSpec((n_seq * dec_seq, D_GRP), lambda *i: (first_blk + step_of(*i), 0))
    newspec = pl.BlockSpec((None, n_seq * dec_seq * N_HEADS, HEAD_W),
                           lambda *i: (layer, first_blk + step_of(*i), 0))
    hbm = pl.BlockSpec(memory_space=pl.ANY)
    operands = [qs, knbuf, vnbuf, sgas, bias, sg, *lams, cache_k, cache_v]
    in_specs = ([seqspec, newspec, newspec, seqspec, const(bias.shape, pipeline_mode=pl.Buffered(1)),
                 const((1, HEAD_W))] + [const((1, DK))] * 4 + [hbm, hbm])
    assert len(operands) == len(in_specs) == N_PAGED_IN
    out_spec = pl.BlockSpec((n_seq * dec_seq, D_GRP), lambda *i: (step_of(*i), 0))
    out_shape = jax.ShapeDtypeStruct((n_steps * n_seq * dec_seq, D_GRP), F32)
    page_buf = pltpu.VMEM((n_seq, n_pages, PAGE * N_HEADS, HEAD_W), F32)
    scratch = [page_buf, page_buf, pltpu.SemaphoreType.DMA((n_seq,)), pltpu.SemaphoreType.DMA((n_seq,))]
    return operands, in_specs, out_spec, out_shape, scratch


def _inproj_decode_kernel(pt_ref, x_ref, g_ref, w_ref, lng_ref, lnb_ref, *rest,
                          n_alias, n_seq, first_seq, layer, lam_init, n_pages, dec_seq):
    paged_in, rest = rest[:N_PAGED_IN], rest[N_PAGED_IN + n_alias:]
    outs, att_ref, scratch = rest[:6], rest[6], rest[7:]
    stages = _inproj_stages(x_ref, g_ref, w_ref, lng_ref, lnb_ref, outs, False)
    per_slot = -(-len(stages) // n_seq)
    stages = stages[:per_slot] + stages[per_slot:][::-1]
    fillers = [stages[slot * per_slot:(slot + 1) * per_slot] for slot in range(n_seq)]
    _paged_attention(pt_ref, pl.program_id(0), pl.num_programs(0), first_seq, *paged_in, att_ref, *scratch,
                     layer=layer, lam_init=lam_init, n_pages=n_pages, dec_seq=dec_seq, fillers=fillers)


def _inproj_decode(x, g, w, lng, lnb, layer, depth, prev, pt, paged, first_seq, n_seq, dec_seq, n_pages,
                   lam_init):
    m = x.shape[0]
    tm = ROW_TILE // 2
    n_steps = m // tm
    const = lambda shape, **kw: pl.BlockSpec(shape, lambda t, pt: (0, 0), **kw)
    once = dict(pipeline_mode=pl.Buffered(1))
    p_operands, p_specs, p_out_spec, p_out_shape, p_scratch = _paged_operands(
        *paged, layer, dec_seq, n_pages, first_seq, n_seq, n_steps, lambda t, pt: t)
    out_shape, out_specs, stacked_idx = _inproj_out_specs(
        m, tm, BF16, layer, depth, prev is None, False, lambda t, pt: t)
    prev = list(prev) if prev is not None else []
    n_inputs = 6 + N_PAGED_IN
    aliases = {n_inputs + i: stacked_idx[i] for i in range(len(prev))}
    grid_spec = pltpu.PrefetchScalarGridSpec(
        num_scalar_prefetch=1,
        grid=(n_steps,),
        in_specs=[pl.BlockSpec((tm, D_MODEL), lambda t, pt: (t, 0)), const((1, D_MODEL)),
                  pl.BlockSpec((None, D_MODEL, N_GRP * D_GRP), lambda t, pt: (layer, 0, 0), **once),
                  const((1, D_GRP)), const((1, D_GRP))]
                 + p_specs + [pl.BlockSpec(memory_space=pl.ANY)] * len(prev),
        out_specs=out_specs + [p_out_spec],
        scratch_shapes=p_scratch,
    )
    return pl.pallas_call(
        functools.partial(_inproj_decode_kernel, n_alias=len(prev), n_seq=n_seq, first_seq=first_seq,
                          layer=layer, lam_init=lam_init, n_pages=n_pages, dec_seq=dec_seq),
        grid_spec=grid_spec,
        out_shape=out_shape + [p_out_shape],
        input_output_aliases=aliases,
        compiler_params=pltpu.CompilerParams(dimension_semantics=("arbitrary",),
                                             vmem_limit_bytes=VMEM_LIMIT),
        name="inproj_decode",
    )(pt, x, g, w, lng, lnb, *p_operands, *prev)


def _mix_chunk_part(x_ref, ugc_ref, vn_ref, ws_ref, bsb_ref, wout_ref, cm_sc):
    tm = x_ref.shape[0]
    nc = tm // CHUNK
    r = lax.broadcasted_iota(jnp.int32, (CHUNK, CHUNK), 0)
    c = lax.broadcasted_iota(jnp.int32, (CHUNK, CHUNK), 1)
    for h in range(N_HEADS):
        sl = slice(h * HEAD_W, (h + 1) * HEAD_W)
        ws = jnp.where(c <= r, ws_ref[h], 0.0).astype(BF16)
        vn = jnp.concatenate([vn_ref[ci * CHUNK:(ci + 1) * CHUNK, sl].astype(BF16)
                              for ci in range(nc)], axis=1)
        mixed = jnp.dot(ws, vn, preferred_element_type=F32)
        for ci in range(nc):
            rows = slice(ci * CHUNK, (ci + 1) * CHUNK)
            mc = mixed[:, ci * HEAD_W:(ci + 1) * HEAD_W] + bsb_ref[:, sl]
            cm_sc[rows, sl] = (ugc_ref[rows, sl].astype(F32) * mc).astype(BF16)
    return x_ref[...] + jnp.dot(cm_sc[...], wout_ref[D_GRP:2 * D_GRP, :], preferred_element_type=F32)


def _mix_attn_part(y, att, wout_ref, fg_ref, o_ref, final):
    y = y + jnp.dot(att, wout_ref[0:D_GRP, :], preferred_element_type=F32)
    if final:
        ms = jnp.mean(y * y, axis=-1, keepdims=True)
        y = y * lax.rsqrt(ms + EPS) * fg_ref[...]
    o_ref[...] = y


def _mix_out_kernel(x_ref, att_ref, ugc_ref, vn_ref, ws_ref, bsb_ref, wout_ref, fg_ref,
                    o_ref, cm_sc, *, final):
    y = _mix_chunk_part(x_ref, ugc_ref, vn_ref, ws_ref, bsb_ref, wout_ref, cm_sc)
    _mix_attn_part(y, att_ref[...].astype(BF16), wout_ref, fg_ref, o_ref, final)


def _mix_out(x, att, ugc, vn, ws, bsb, wout, layer, fg, final):
    m = x.shape[0]
    tm = ROW_TILE
    row = lambda width: pl.BlockSpec((tm, width), lambda i: (i, 0))
    return pl.pallas_call(
        functools.partial(_mix_out_kernel, final=final),
        grid=(m // tm,),
        in_specs=[row(D_MODEL), row(D_GRP), row(D_GRP), row(D_GRP),
                  pl.BlockSpec((N_HEADS, CHUNK, CHUNK), lambda i: (0, 0, 0)),
                  pl.BlockSpec((CHUNK, D_GRP), lambda i: (0, 0)),
                  pl.BlockSpec((None, 2 * D_GRP, D_MODEL), lambda i: (layer, 0, 0)),
                  pl.BlockSpec((1, D_MODEL), lambda i: (0, 0))],
        out_specs=row(D_MODEL),
        out_shape=jax.ShapeDtypeStruct((m, D_MODEL), F32),
        scratch_shapes=[pltpu.VMEM((tm, D_GRP), BF16)],
        compiler_params=pltpu.CompilerParams(dimension_semantics=("parallel",),
                                             vmem_limit_bytes=VMEM_LIMIT),
        name="mix_out",
    )(x, att, ugc, vn, ws, bsb, wout, fg)


def kernel(x_prompt, x_sample, cache_k, cache_v, page_table, norm_g, w_in, lam_q1, lam_k1,
           lam_q2, lam_k2, subln_g, ln_v_g, ln_v_b, w_s, b_s, w_out, final_g):
    batch, seq, _ = x_prompt.shape
    dec_batch, dec_seq, _ = x_sample.shape
    depth, n_pool = cache_k.shape[0], cache_k.shape[1]
    n_pages = page_table.shape[1]
    assert seq % ATT_BLK == 0 and (batch * seq) % ROW_TILE == 0 and (dec_batch * dec_seq) % ROW_TILE == 0
    assert CHUNK % dec_seq == 0 and dec_seq % 8 == 0 and ATT_BLK % PAGE == 0
    assert dec_seq * N_HEADS <= NEW_ROWS
    proj_steps = (batch * seq) // (ROW_TILE // 2)
    assert dec_batch % proj_steps == 0
    n_seq_proj = dec_batch // proj_steps

    hp = x_prompt.reshape(batch * seq, D_MODEL)
    hs = x_sample.reshape(dec_batch * dec_seq, D_MODEL)
    ck = cache_k.reshape(depth, n_pool, PAGE * N_HEADS, HEAD_W)
    cv = cache_v.reshape(depth, n_pool, PAGE * N_HEADS, HEAD_W)
    pt = page_table.reshape(-1)
    w_in_b = w_in.astype(BF16)
    w_out_b = w_out.astype(BF16)
    fg = final_g.reshape(1, D_MODEL)
    eye = jnp.eye(CHUNK // dec_seq, dtype=F32)

    prompt_bufs, sample_bufs = None, None
    for l in range(depth):
        lam_init = _lambda_init(l)
        g = norm_g[l].reshape(1, D_MODEL)
        lng = ln_v_g[l].reshape(1, D_GRP)
        lnb = ln_v_b[l].reshape(1, D_GRP)
        sg = subln_g[l].reshape(1, HEAD_W)
        lams = [a[l].reshape(1, DK) for a in (lam_q1, lam_k1, lam_q2, lam_k2)]
        final = l == depth - 1

        qs, ksbuf, vsbuf, sgas, ugcs, vns, vnsbuf = _inproj(hs, g, w_in_b, lng, lnb, F32, l, depth,
                                                            sample_bufs, True)
        sample_bufs = (ksbuf, vsbuf, vnsbuf)
        paged = (qs, ksbuf, vsbuf, sgas, ck, cv, sg, lams)
        q, kbuf, vbuf, sga, ugc, vn, atts = _inproj_decode(
            hp, g, w_in_b, lng, lnb, l, depth, prompt_bufs,
            pt, paged, 0, n_seq_proj, dec_seq, n_pages, lam_init)
        prompt_bufs = (kbuf, vbuf)

        bsb = jnp.repeat(b_s[l].T, HEAD_W, axis=1)
        hp = _prompt_attn_mix(q, kbuf, vbuf, l, sga, sg, lams, lam_init, batch, seq,
                              hp, ugc, vn, w_s[l], bsb, w_out_b, fg, final)

        ws8 = w_s[l][:, :dec_seq, :dec_seq]
        ws_bd = (eye[None, :, None, :, None] * ws8[:, None, :, None, :]).reshape(N_HEADS, CHUNK, CHUNK)
        bsb = jnp.tile(jnp.repeat(b_s[l][:, :dec_seq].T, HEAD_W, axis=1), (CHUNK // dec_seq, 1))
        hs = _mix_out(hs, atts, ugcs, vns, ws_bd, bsb, w_out_b, l, fg, final)

    shape_p = (depth, batch, seq, N_HEADS, HEAD_W)
    shape_s = (depth, dec_batch, dec_seq, N_HEADS, HEAD_W)
    return (hp.reshape(batch, seq, D_MODEL),
            hs.reshape(dec_batch, dec_seq, D_MODEL),
            prompt_bufs[0].reshape(shape_p),
            prompt_bufs[1].reshape(shape_p),
            sample_bufs[0].reshape(shape_s),
            sample_bufs[1].reshape(shape_s),
            sample_bufs[2].reshape(shape_s))
```

```python
import functools
import math

import numpy as np
import jax
import jax.numpy as jnp
from jax import lax
from jax.experimental import pallas as pl
from jax.experimental.pallas import tpu as pltpu

F32 = jnp.float32
BF16 = jnp.bfloat16

D_MODEL = 1024
N_HEADS = 4
DK = 64
HEAD_W = 128
D_GRP = N_HEADS * HEAD_W
N_GRP = 7
CHUNK = 128
PAGE = 128
EPS = 1e-6
NEG = -1e30
ONES_ROWS = 16
POS_LANES = 128
SLOPE_PARTS = 3
LOG2E = np.float32(math.log2(math.e))

ROW_TILE = 512
ATT_BLK = 256
NEW_ROWS = 128
PAGE_DMA_QUEUE = 1
VMEM_LIMIT = 56 * 1024 * 1024


def _lambda_init(layer):
    return 0.8 - 0.6 * math.exp(-0.3 * layer)


def _alibi_slope(h):
    return 2.0 ** (-8.0 * (h + 1) / N_HEADS)


def _slope_parts(h):
    target = np.float32(_alibi_slope(h)) * LOG2E
    rest, parts = target, []
    for _ in range(SLOPE_PARTS):
        part = np.asarray(rest, dtype=BF16).astype(np.float32)
        parts.append(float(part))
        rest = np.float32(rest - part)
    assert rest == 0.0
    return parts


def _lam_value(lq1, lk1, lq2, lk2, lam_init):
    a = jnp.sum(lq1[...] * lk1[...], axis=-1, keepdims=True)
    b = jnp.sum(lq2[...] * lk2[...], axis=-1, keepdims=True)
    return jnp.exp(a) - jnp.exp(b) + lam_init


def _silu(x):
    return x * jax.nn.sigmoid(x)


def _head_rows(h, n_rows, first_row=0):
    return pl.ds(first_row * N_HEADS + h, n_rows, stride=N_HEADS)


def _inproj_stages(x_ref, g_ref, w_ref, lng_ref, lnb_ref, outs, with_vn_out):
    q_ref, k_ref, v_ref, sga_ref, ugc_ref, vn_ref = outs[:6]
    tm = x_ref.shape[0]

    normed = []

    def proj(i):
        if not normed:
            x = x_ref[...]
            ms = jnp.mean(x * x, axis=-1, keepdims=True)
            normed.append((x * lax.rsqrt(ms + EPS) * g_ref[...]).astype(BF16))
        return jnp.dot(normed[0], w_ref[:, i * D_GRP:(i + 1) * D_GRP], preferred_element_type=F32)

    def store_heads(ref, c, val):
        if len(ref.shape) == 3:
            for d in range(ref.shape[0]):
                ref[d, _head_rows(c, tm), :] = val
        else:
            ref[_head_rows(c, tm), :] = val

    def store_all_heads(ref, val):
        for c in range(N_HEADS):
            store_heads(ref, c, val[:, c * HEAD_W:(c + 1) * HEAD_W])

    def q_stage():
        q_ref[...] = (proj(0) * float(np.float32(DK ** -0.5) * LOG2E)).astype(q_ref.dtype)

    def k_stage():
        store_all_heads(k_ref, proj(1))

    def v_stage():
        store_all_heads(v_ref, proj(2))

    def gate_stage():
        sga_ref[...] = _silu(proj(3)).astype(sga_ref.dtype)

    def ugc_stage():
        ugc_ref[...] = (proj(4) * _silu(proj(6))).astype(ugc_ref.dtype)

    def vn_stage():
        vc = proj(5)
        for c in range(N_HEADS):
            sl = slice(c * HEAD_W, (c + 1) * HEAD_W)
            blk = vc[:, sl]
            mu = jnp.mean(blk, axis=-1, keepdims=True)
            xc = blk - mu
            var = jnp.mean(xc * xc, axis=-1, keepdims=True)
            vn = xc * lax.rsqrt(var + EPS) * lng_ref[:, sl] + lnb_ref[:, sl]
            vn_ref[:, sl] = vn.astype(vn_ref.dtype)
            if with_vn_out:
                store_heads(outs[6], c, vn)

    return [q_stage, k_stage, v_stage, gate_stage, ugc_stage, vn_stage]


def _inproj_kernel(*refs, n_alias, with_vn_out):
    outs = refs[5 + n_alias:]
    for stage in _inproj_stages(*refs[:5], outs, with_vn_out):
        stage()


def _inproj_out_specs(m, tm, act_dtype, layer, depth, first, with_vn_out, index):
    row = pl.BlockSpec((tm, D_GRP), lambda *i: (index(*i), 0))
    act = jax.ShapeDtypeStruct((m, D_GRP), act_dtype)
    stacked = jax.ShapeDtypeStruct((depth, m * N_HEADS, HEAD_W), F32)
    if first:
        stacked_spec = pl.BlockSpec((depth, tm * N_HEADS, HEAD_W), lambda *i: (0, index(*i), 0))
    else:
        stacked_spec = pl.BlockSpec((None, tm * N_HEADS, HEAD_W), lambda *i: (layer, index(*i), 0))
    extra = 1 if with_vn_out else 0
    out_shape = [act, stacked, stacked, act, act, act] + [stacked] * extra
    out_specs = [row, stacked_spec, stacked_spec, row, row, row] + [stacked_spec] * extra
    return out_shape, out_specs, [1, 2] + [6] * extra


def _inproj(x, g, w, lng, lnb, act_dtype, layer, depth, prev, with_vn_out):
    m = x.shape[0]
    tm = ROW_TILE
    const = lambda shape: pl.BlockSpec(shape, lambda i: (0, 0))
    out_shape, out_specs, stacked_idx = _inproj_out_specs(
        m, tm, act_dtype, layer, depth, prev is None, with_vn_out, lambda i: i)
    prev = list(prev) if prev is not None else []
    aliases = {5 + n: stacked_idx[n] for n in range(len(prev))}
    return pl.pallas_call(
        functools.partial(_inproj_kernel, n_alias=len(prev), with_vn_out=with_vn_out),
        grid=(m // tm,),
        in_specs=[pl.BlockSpec((tm, D_MODEL), lambda i: (i, 0)), const((1, D_MODEL)),
                  pl.BlockSpec((None, D_MODEL, N_GRP * D_GRP), lambda i: (layer, 0, 0)),
                  const((1, D_GRP)), const((1, D_GRP))]
                 + [pl.BlockSpec(memory_space=pl.ANY)] * len(prev),
        out_specs=out_specs,
        out_shape=out_shape,
        input_output_aliases=aliases,
        compiler_params=pltpu.CompilerParams(dimension_semantics=("parallel",),
                                             vmem_limit_bytes=VMEM_LIMIT),
        name="inproj",
    )(x, g, w, lng, lnb, *prev)


def _prompt_attn_kernel(q_ref, k_ref, v_ref, sga_ref, sg_ref, lq1, lk1, lq2, lk2,
                        x_ref, ugc_ref, vn_ref, ws_ref, bsb_ref, wout_ref, fg_ref,
                        o_ref, kb_sc, vt_sc, cm_sc, att_sc, *stat_sc, lam_init, tb, nblk, final):
    m_sc = stat_sc[:N_HEADS]
    acc_sc = stat_sc[N_HEADS:]
    i = pl.program_id(1)

    @pl.when(i == 0)
    def _():
        pos = lax.broadcasted_iota(jnp.int32, (tb, POS_LANES), 0)
        lane = lax.broadcasted_iota(jnp.int32, (tb, POS_LANES), 1)
        for jb in range(nblk):
            kpos = pos + jb * tb
            hi = (kpos // PAGE) * PAGE
            aug = jnp.where(lane >= 2 * SLOPE_PARTS, 0, jnp.where(lane % 2 == 0, hi, kpos - hi))
            aug = aug.astype(F32).astype(BF16)
            for h in range(N_HEADS):
                rows = _head_rows(h, tb, jb * tb)
                kb_sc[h, jb * tb:(jb + 1) * tb, 0:HEAD_W] = k_ref[rows, :].astype(BF16)
                kb_sc[h, jb * tb:(jb + 1) * tb, HEAD_W:HEAD_W + POS_LANES] = aug
                vt_sc[h, jb, 0:HEAD_W, :] = v_ref[rows, :].T.astype(BF16)
                vt_sc[h, jb, HEAD_W:HEAD_W + ONES_ROWS, :] = jnp.ones((ONES_ROWS, tb), BF16)

    lane = lax.broadcasted_iota(jnp.int32, (tb, HEAD_W), 1)
    qz = []
    for h in range(N_HEADS):
        q = q_ref[:, h * HEAD_W:(h + 1) * HEAD_W]
        zero = jnp.zeros_like(q)
        slope = jnp.zeros((tb, POS_LANES), F32)
        for n, part in enumerate(_slope_parts(h)):
            slope = jnp.where(lane // 2 == n, part, slope)
        slope = slope.astype(BF16)
        q2 = jnp.concatenate([jnp.where(lane < DK, q, zero), jnp.where(lane >= DK, q, zero)], axis=0)
        qz.append(jnp.concatenate([q2, jnp.concatenate([slope, slope], axis=0)], axis=1))
        m_sc[h][...] = jnp.full(m_sc[h].shape, NEG, F32)
        acc_sc[h][...] = jnp.zeros(acc_sc[h].shape, F32)

    def steps(first_blk, nkb, masked):
        heads = range(N_HEADS)
        rows = pl.ds(pl.multiple_of(first_blk * tb, tb), nkb * tb)
        s = [lax.dot_general(kb_sc[h, rows, :], qz[h], (((1,), (1,)), ((), ())),
                             preferred_element_type=F32) for h in heads]
        if masked:
            key_i = lax.broadcasted_iota(jnp.int32, (nkb * tb, 2 * tb), 0)
            qry_i = lax.broadcasted_iota(jnp.int32, (nkb * tb, 2 * tb), 1)
            qry_i = jnp.where(qry_i >= tb, qry_i - tb, qry_i) + (nkb - 1) * tb
            s = [jnp.where(key_i <= qry_i, sh, NEG) for sh in s]
        m_old = [m_sc[h][...] for h in heads]
        m_new = [jnp.maximum(m_old[h], jnp.max(s[h], axis=0, keepdims=True)) for h in heads]
        p = [jnp.exp2(s[h] - m_new[h]).astype(BF16) for h in heads]
        for h in heads:
            pv = jnp.dot(vt_sc[h, first_blk], p[h][0:tb, :], preferred_element_type=F32)
            for kb in range(1, nkb):
                pv = pv + jnp.dot(vt_sc[h, first_blk + kb], p[h][kb * tb:(kb + 1) * tb, :],
                                  preferred_element_type=F32)
            acc_sc[h][...] = acc_sc[h][...] * jnp.exp2(m_old[h] - m_new[h]) + pv
            m_sc[h][...] = m_new[h]

    def body(jj, carry):
        steps(2 * jj, 2, False)
        return carry

    lax.fori_loop(0, i // 2, body, 0)

    @pl.when(i % 2 == 1)
    def _():
        steps(i - 1, 2, True)

    @pl.when(i % 2 == 0)
    def _():
        steps(i, 1, True)

    y = _mix_chunk_part(x_ref, ugc_ref, vn_ref, ws_ref, bsb_ref, wout_ref, cm_sc)

    lam = _lam_value(lq1, lk1, lq2, lk2, lam_init)
    for h in range(N_HEADS):
        sl = slice(h * HEAD_W, (h + 1) * HEAD_W)
        acc = acc_sc[h]
        inv = 1.0 / acc[HEAD_W:HEAD_W + 1, :]
        o1 = acc[0:HEAD_W, 0:tb] * inv[:, 0:tb]
        o2 = acc[0:HEAD_W, tb:2 * tb] * inv[:, tb:2 * tb]
        o = (o1 - lam * o2).T
        ms = jnp.mean(o * o, axis=-1, keepdims=True)
        att = o * lax.rsqrt(ms + EPS) * sg_ref[...] * (1.0 - lam_init)
        att_sc[:, sl] = (att * sga_ref[:, sl].astype(F32)).astype(BF16)
    _mix_attn_part(y, att_sc[...], wout_ref, fg_ref, o_ref, final)


def _prompt_attn_mix(q, kbuf, vbuf, layer, sga, sg, lams, lam_init, batch, seq,
                     x, ugc, vn, ws, bsb, wout, fg, final):
    tb = ATT_BLK
    nblk = seq // tb
    rowspec = lambda width: pl.BlockSpec((tb, width), lambda b, i: (b * nblk + i, 0))
    kvspec = pl.BlockSpec((None, seq * N_HEADS, HEAD_W), lambda b, i: (layer, b, 0))
    const = lambda shape, **kw: pl.BlockSpec(shape, lambda b, i: (0,) * len(shape), **kw)
    stats = ([pltpu.VMEM((1, 2 * tb), F32)] * N_HEADS
             + [pltpu.VMEM((HEAD_W + ONES_ROWS, 2 * tb), F32)] * N_HEADS)
    return pl.pallas_call(
        functools.partial(_prompt_attn_kernel, lam_init=lam_init, tb=tb, nblk=nblk, final=final),
        grid=(batch, nblk),
        in_specs=[rowspec(D_GRP), kvspec, kvspec, rowspec(D_GRP), const((1, HEAD_W))] + [const((1, DK))] * 4
                 + [rowspec(D_MODEL), rowspec(D_GRP), rowspec(D_GRP), const((N_HEADS, CHUNK, CHUNK)),
                    const((CHUNK, D_GRP)),
                    pl.BlockSpec((None, 2 * D_GRP, D_MODEL), lambda b, i: (layer, 0, 0),
                                 pipeline_mode=pl.Buffered(1)),
                    const((1, D_MODEL))],
        out_specs=rowspec(D_MODEL),
        out_shape=jax.ShapeDtypeStruct(x.shape, F32),
        scratch_shapes=[pltpu.VMEM((N_HEADS, seq, HEAD_W + POS_LANES), BF16),
                        pltpu.VMEM((N_HEADS, nblk, HEAD_W + ONES_ROWS, tb), BF16),
                        pltpu.VMEM((tb, D_GRP), BF16), pltpu.VMEM((tb, D_GRP), BF16)] + stats,
        compiler_params=pltpu.CompilerParams(
            dimension_semantics=("parallel", "arbitrary"),
            vmem_limit_bytes=VMEM_LIMIT),
        name="prompt_attn_mix",
    )(q, kbuf, vbuf, sga, sg, *lams, x, ugc, vn, ws, bsb, wout, fg)


def _sample_bias(n_pages, dec_seq):
    past = n_pages * PAGE
    n_pos = past + NEW_ROWS // N_HEADS
    t = np.arange(dec_seq)[:, None]
    kpos = np.arange(n_pos)[None, :]
    visible = (kpos < past) | (kpos - past <= t)
    bias = np.full((2, N_HEADS, dec_seq, n_pos, N_HEADS), NEG, np.float32)
    for h in range(N_HEADS):
        alibi = -np.float32(_alibi_slope(h)) * (past + t - kpos).astype(np.float32) * LOG2E
        bias[:, h, :, :, h] = np.where(visible, alibi, NEG)
    return bias.reshape(2 * N_HEADS * dec_seq, n_pos * N_HEADS)


def _paged_attention(pt_ref, step, n_steps, first_seq, q_ref, kn_ref, vn_ref, sga_ref, bias_ref, sg_ref,
                     lq1, lk1, lq2, lk2, ck_hbm, cv_hbm, o_ref, kbuf, vbuf, ksem, vsem,
                     *, layer, lam_init, n_pages, dec_seq, fillers):
    n_seq = len(fillers)
    n_rows = 2 * N_HEADS * dec_seq
    half = N_HEADS * dec_seq
    page_rows = PAGE * N_HEADS
    past_rows = n_pages * page_rows
    new_rows = dec_seq * N_HEADS
    slots = range(n_seq)

    def page_copies(seq_idx, slot):
        copies = []
        for j in range(n_pages):
            page = pt_ref[seq_idx * n_pages + j]
            copies.append(pltpu.make_async_copy(ck_hbm.at[layer, page], kbuf.at[slot, j], ksem.at[slot]))
            copies.append(pltpu.make_async_copy(cv_hbm.at[layer, page], vbuf.at[slot, j], vsem.at[slot]))
        return copies

    @pl.when(step == 0)
    def _():
        for slot in slots:
            for cp in page_copies(first_seq + slot, slot):
                cp.start(priority=PAGE_DMA_QUEUE)

    lane = lax.broadcasted_iota(jnp.int32, (half, HEAD_W), 1)
    lam = _lam_value(lq1, lk1, lq2, lk2, lam_init)
    pad = jnp.zeros((NEW_ROWS - new_rows, HEAD_W), F32)
    nt = (((1,), (1,)), ((), ()))

    def attend(slot, seq_idx):
        rows = slice(slot * dec_seq, (slot + 1) * dec_seq)
        new = slice(slot * new_rows, (slot + 1) * new_rows)
        k_new = jnp.concatenate([kn_ref[new, :], pad], axis=0)
        v_new = jnp.concatenate([vn_ref[new, :], pad], axis=0)
        qh = jnp.concatenate([q_ref[rows, h * HEAD_W:(h + 1) * HEAD_W] for h in range(N_HEADS)], axis=0)
        qrows = jnp.concatenate([jnp.where(lane < DK, qh, 0.0), jnp.where(lane >= DK, qh, 0.0)], axis=0)
        s = [lax.dot_general(qrows, kbuf[slot, j], nt, preferred_element_type=F32) for j in range(n_pages)]
        s.append(lax.dot_general(qrows, k_new, nt, preferred_element_type=F32))
        s = jnp.concatenate(s, axis=1) + bias_ref[...]
        yield
        p = jnp.exp2(s - jnp.max(s, axis=-1, keepdims=True))
        inv = 1.0 / jnp.sum(p, axis=-1, keepdims=True)
        yield
        pv = jnp.dot(p[:, past_rows:], v_new, preferred_element_type=F32)
        for j in range(n_pages):
            pv = pv + jnp.dot(p[:, j * page_rows:(j + 1) * page_rows], vbuf[slot, j],
                              preferred_element_type=F32)
        res = pv[0:half, :] * inv[0:half, :] - pv[half:n_rows, :] * (lam * inv[half:n_rows, :])

        @pl.when(step + 1 < n_steps)
        def _():
            for cp in page_copies(seq_idx + n_seq, slot):
                cp.start(priority=PAGE_DMA_QUEUE)

        yield
        for h in range(N_HEADS):
            sl = slice(h * HEAD_W, (h + 1) * HEAD_W)
            o = res[h * dec_seq:(h + 1) * dec_seq, :]
            ms = jnp.mean(o * o, axis=-1, keepdims=True)
            att = o * lax.rsqrt(ms + EPS) * sg_ref[...] * (1.0 - lam_init)
            o_ref[rows, sl] = (att * sga_ref[rows, sl]).astype(o_ref.dtype)

    for slot in slots:
        seq_idx = first_seq + n_seq * step + slot
        for cp in page_copies(seq_idx, slot):
            cp.wait()
        pending = list(fillers[slot])
        for _ in attend(slot, seq_idx):
            if pending:
                pending.pop(0)()
        for filler in pending:
            filler()


N_PAGED_IN = 12


def _paged_operands(qs, knbuf, vnbuf, sgas, cache_k, cache_v, sg, lams, layer, dec_seq, n_pages,
                    first_seq, n_seq, n_steps, step_of):
    bias = jnp.asarray(_sample_bias(n_pages, dec_seq))
    first_blk = first_seq // n_seq
    const = lambda shape, **kw: pl.BlockSpec(shape, lambda *i: (0, 0), **kw)
    seqspec = pl.BlockSpec((n_seq * dec_seq, D_GRP), lambda *i: (first_blk + step_of(*i), 0))
    newspec = pl.BlockSpec((None, n_seq * dec_seq * N_HEADS, HEAD_W),
                           lambda *i: (layer, first_blk + step_of(*i), 0))
    hbm = pl.BlockSpec(memory_space=pl.ANY)
    operands = [qs, knbuf, vnbuf, sgas, bias, sg, *lams, cache_k, cache_v]
    in_specs = ([seqspec, newspec, newspec, seqspec, const(bias.shape, pipeline_mode=pl.Buffered(1)),
                 const((1, HEAD_W))] + [const((1, DK))] * 4 + [hbm, hbm])
    assert len(operands) == len(in_specs) == N_PAGED_IN
    out_spec = pl.BlockSpec((n_seq * dec_seq, D_GRP), lambda *i: (step_of(*i), 0))
    out_shape = jax.ShapeDtypeStruct((n_steps * n_seq * dec_seq, D_GRP), F32)
    page_buf = pltpu.VMEM((n_seq, n_pages, PAGE * N_HEADS, HEAD_W), F32)
    scratch = [page_buf, page_buf, pltpu.SemaphoreType.DMA((n_seq,)), pltpu.SemaphoreType.DMA((n_seq,))]
    return operands, in_specs, out_spec, out_shape, scratch


def _inproj_decode_kernel(pt_ref, x_ref, g_ref, w_ref, lng_ref, lnb_ref, *rest,
                          n_alias, n_seq, first_seq, layer, lam_init, n_pages, dec_seq):
    paged_in, rest = rest[:N_PAGED_IN], rest[N_PAGED_IN + n_alias:]
    outs, att_ref, scratch = rest[:6], rest[6], rest[7:]
    stages = _inproj_stages(x_ref, g_ref, w_ref, lng_ref, lnb_ref, outs, False)
    per_slot = -(-len(stages) // n_seq)
    stages = stages[:per_slot] + stages[per_slot:][::-1]
    fillers = [stages[slot * per_slot:(slot + 1) * per_slot] for slot in range(n_seq)]
    _paged_attention(pt_ref, pl.program_id(0), pl.num_programs(0), first_seq, *paged_in, att_ref, *scratch,
                     layer=layer, lam_init=lam_init, n_pages=n_pages, dec_seq=dec_seq, fillers=fillers)


def _inproj_decode(x, g, w, lng, lnb, layer, depth, prev, pt, paged, first_seq, n_seq, dec_seq, n_pages,
                   lam_init):
    m = x.shape[0]
    tm = ROW_TILE // 2
    n_steps = m // tm
    const = lambda shape, **kw: pl.BlockSpec(shape, lambda t, pt: (0, 0), **kw)
    once = dict(pipeline_mode=pl.Buffered(1))
    p_operands, p_specs, p_out_spec, p_out_shape, p_scratch = _paged_operands(
        *paged, layer, dec_seq, n_pages, first_seq, n_seq, n_steps, lambda t, pt: t)
    out_shape, out_specs, stacked_idx = _inproj_out_specs(
        m, tm, BF16, layer, depth, prev is None, False, lambda t, pt: t)
    prev = list(prev) if prev is not None else []
    n_inputs = 6 + N_PAGED_IN
    aliases = {n_inputs + i: stacked_idx[i] for i in range(len(prev))}
    grid_spec = pltpu.PrefetchScalarGridSpec(
        num_scalar_prefetch=1,
        grid=(n_steps,),
        in_specs=[pl.BlockSpec((tm, D_MODEL), lambda t, pt: (t, 0)), const((1, D_MODEL)),
                  pl.BlockSpec((None, D_MODEL, N_GRP * D_GRP), lambda t, pt: (layer, 0, 0), **once),
                  const((1, D_GRP)), const((1, D_GRP))]
                 + p_specs + [pl.BlockSpec(memory_space=pl.ANY)] * len(prev),
        out_specs=out_specs + [p_out_spec],
        scratch_shapes=p_scratch,
    )
    return pl.pallas_call(
        functools.partial(_inproj_decode_kernel, n_alias=len(prev), n_seq=n_seq, first_seq=first_seq,
                          layer=layer, lam_init=lam_init, n_pages=n_pages, dec_seq=dec_seq),
        grid_spec=grid_spec,
        out_shape=out_shape + [p_out_shape],
        input_output_aliases=aliases,
        compiler_params=pltpu.CompilerParams(dimension_semantics=("arbitrary",),
                                             vmem_limit_bytes=VMEM_LIMIT),
        name="inproj_decode",
    )(pt, x, g, w, lng, lnb, *p_operands, *prev)


def _mix_chunk_part(x_ref, ugc_ref, vn_ref, ws_ref, bsb_ref, wout_ref, cm_sc):
    tm = x_ref.shape[0]
    nc = tm // CHUNK
    r = lax.broadcasted_iota(jnp.int32, (CHUNK, CHUNK), 0)
    c = lax.broadcasted_iota(jnp.int32, (CHUNK, CHUNK), 1)
    for h in range(N_HEADS):
        sl = slice(h * HEAD_W, (h + 1) * HEAD_W)
        ws = jnp.where(c <= r, ws_ref[h], 0.0).astype(BF16)
        vn = jnp.concatenate([vn_ref[ci * CHUNK:(ci + 1) * CHUNK, sl].astype(BF16)
                              for ci in range(nc)], axis=1)
        mixed = jnp.dot(ws, vn, preferred_element_type=F32)
        for ci in range(nc):
            rows = slice(ci * CHUNK, (ci + 1) * CHUNK)
            mc = mixed[:, ci * HEAD_W:(ci + 1) * HEAD_W] + bsb_ref[:, sl]
            cm_sc[rows, sl] = (ugc_ref[rows, sl].astype(F32) * mc).astype(BF16)
    return x_ref[...] + jnp.dot(cm_sc[...], wout_ref[D_GRP:2 * D_GRP, :], preferred_element_type=F32)


def _mix_attn_part(y, att, wout_ref, fg_ref, o_ref, final):
    y = y + jnp.dot(att, wout_ref[0:D_GRP, :], preferred_element_type=F32)
    if final:
        ms = jnp.mean(y * y, axis=-1, keepdims=True)
        y = y * lax.rsqrt(ms + EPS) * fg_ref[...]
    o_ref[...] = y


def _mix_out_kernel(x_ref, att_ref, ugc_ref, vn_ref, ws_ref, bsb_ref, wout_ref, fg_ref,
                    o_ref, cm_sc, *, final):
    y = _mix_chunk_part(x_ref, ugc_ref, vn_ref, ws_ref, bsb_ref, wout_ref, cm_sc)
    _mix_attn_part(y, att_ref[...].astype(BF16), wout_ref, fg_ref, o_ref, final)


def _mix_out(x, att, ugc, vn, ws, bsb, wout, layer, fg, final):
    m = x.shape[0]
    tm = ROW_TILE
    row = lambda width: pl.BlockSpec((tm, width), lambda i: (i, 0))
    return pl.pallas_call(
        functools.partial(_mix_out_kernel, final=final),
        grid=(m // tm,),
        in_specs=[row(D_MODEL), row(D_GRP), row(D_GRP), row(D_GRP),
                  pl.BlockSpec((N_HEADS, CHUNK, CHUNK), lambda i: (0, 0, 0)),
                  pl.BlockSpec((CHUNK, D_GRP), lambda i: (0, 0)),
                  pl.BlockSpec((None, 2 * D_GRP, D_MODEL), lambda i: (layer, 0, 0)),
                  pl.BlockSpec((1, D_MODEL), lambda i: (0, 0))],
        out_specs=row(D_MODEL),
        out_shape=jax.ShapeDtypeStruct((m, D_MODEL), F32),
        scratch_shapes=[pltpu.VMEM((tm, D_GRP), BF16)],
        compiler_params=pltpu.CompilerParams(dimension_semantics=("parallel",),
                                             vmem_limit_bytes=VMEM_LIMIT),
        name="mix_out",
    )(x, att, ugc, vn, ws, bsb, wout, fg)


def kernel(x_prompt, x_sample, cache_k, cache_v, page_table, norm_g, w_in, lam_q1, lam_k1,
           lam_q2, lam_k2, subln_g, ln_v_g, ln_v_b, w_s, b_s, w_out, final_g):
    batch, seq, _ = x_prompt.shape
    dec_batch, dec_seq, _ = x_sample.shape
    depth, n_pool = cache_k.shape[0], cache_k.shape[1]
    n_pages = page_table.shape[1]
    assert seq % ATT_BLK == 0 and (batch * seq) % ROW_TILE == 0 and (dec_batch * dec_seq) % ROW_TILE == 0
    assert CHUNK % dec_seq == 0 and dec_seq % 8 == 0 and ATT_BLK % PAGE == 0
    assert dec_seq * N_HEADS <= NEW_ROWS
    proj_steps = (batch * seq) // (ROW_TILE // 2)
    assert dec_batch % proj_steps == 0
    n_seq_proj = dec_batch // proj_steps

    hp = x_prompt.reshape(batch * seq, D_MODEL)
    hs = x_sample.reshape(dec_batch * dec_seq, D_MODEL)
    ck = cache_k.reshape(depth, n_pool, PAGE * N_HEADS, HEAD_W)
    cv = cache_v.reshape(depth, n_pool, PAGE * N_HEADS, HEAD_W)
    pt = page_table.reshape(-1)
    w_in_b = w_in.astype(BF16)
    w_out_b = w_out.astype(BF16)
    fg = final_g.reshape(1, D_MODEL)
    eye = jnp.eye(CHUNK // dec_seq, dtype=F32)

    prompt_bufs, sample_bufs = None, None
    for l in range(depth):
        lam_init = _lambda_init(l)
        g = norm_g[l].reshape(1, D_MODEL)
        lng = ln_v_g[l].reshape(1, D_GRP)
        lnb = ln_v_b[l].reshape(1, D_GRP)
        sg = subln_g[l].reshape(1, HEAD_W)
        lams = [a[l].reshape(1, DK) for a in (lam_q1, lam_k1, lam_q2, lam_k2)]
        final = l == depth - 1

        qs, ksbuf, vsbuf, sgas, ugcs, vns, vnsbuf = _inproj(hs, g, w_in_b, lng, lnb, F32, l, depth,
                                                            sample_bufs, True)
        sample_bufs = (ksbuf, vsbuf, vnsbuf)
        paged = (qs, ksbuf, vsbuf, sgas, ck, cv, sg, lams)
        q, kbuf, vbuf, sga, ugc, vn, atts = _inproj_decode(
            hp, g, w_in_b, lng, lnb, l, depth, prompt_bufs,
            pt, paged, 0, n_seq_proj, dec_seq, n_pages, lam_init)
        prompt_bufs = (kbuf, vbuf)

        bsb = jnp.repeat(b_s[l].T, HEAD_W, axis=1)
        hp = _prompt_attn_mix(q, kbuf, vbuf, l, sga, sg, lams, lam_init, batch, seq,
                              hp, ugc, vn, w_s[l], bsb, w_out_b, fg, final)

        ws8 = w_s[l][:, :dec_seq, :dec_seq]
        ws_bd = (eye[None, :, None, :, None] * ws8[:, None, :, None, :]).reshape(N_HEADS, CHUNK, CHUNK)
        bsb = jnp.tile(jnp.repeat(b_s[l][:, :dec_seq].T, HEAD_W, axis=1), (CHUNK // dec_seq, 1))
        hs = _mix_out(hs, atts, ugcs, vns, ws_bd, bsb, w_out_b, l, fg, final)

    shape_p = (depth, batch, seq, N_HEADS, HEAD_W)
    shape_s = (depth, dec_batch, dec_seq, N_HEADS, HEAD_W)
    return (hp.reshape(batch, seq, D_MODEL),
            hs.reshape(dec_batch, dec_seq, D_MODEL),
            prompt_bufs[0].reshape(shape_p),
            prompt_bufs[1].reshape(shape_p),
            sample_bufs[0].reshape(shape_s),
            sample_bufs[1].reshape(shape_s),
            sample_bufs[2].reshape(shape_s))
```

```python
import functools
import math

import numpy as np
import jax
import jax.numpy as jnp
from jax import lax
from jax.experimental import pallas as pl
from jax.experimental.pallas import tpu as pltpu

F32 = jnp.float32
BF16 = jnp.bfloat16

D_MODEL = 1024
N_HEADS = 4
DK = 64
HEAD_W = 128
D_GRP = N_HEADS * HEAD_W
N_GRP = 7
CHUNK = 128
PAGE = 128
EPS = 1e-6
NEG = -1e30
ONES_ROWS = 16
POS_LANES = 128
SLOPE_PARTS = 3
LOG2E = np.float32(math.log2(math.e))

ROW_TILE = 512
ATT_BLK = 256
NEW_ROWS = 128
PAGE_BUFS = 2
VMEM_LIMIT = 56 * 1024 * 1024


def _lambda_init(layer):
    return 0.8 - 0.6 * math.exp(-0.3 * layer)


def _alibi_slope(h):
    return 2.0 ** (-8.0 * (h + 1) / N_HEADS)


def _slope_parts(h):
    target = np.float32(_alibi_slope(h)) * LOG2E
    rest, parts = target, []
    for _ in range(SLOPE_PARTS):
        part = np.asarray(rest, dtype=BF16).astype(np.float32)
        parts.append(float(part))
        rest = np.float32(rest - part)
    assert rest == 0.0
    return parts


def _lam_value(lq1, lk1, lq2, lk2, lam_init):
    a = jnp.sum(lq1[...] * lk1[...], axis=-1, keepdims=True)
    b = jnp.sum(lq2[...] * lk2[...], axis=-1, keepdims=True)
    return jnp.exp(a) - jnp.exp(b) + lam_init


def _silu(x):
    return x * jax.nn.sigmoid(x)


def _head_rows(h, n_rows, first_row=0):
    return pl.ds(first_row * N_HEADS + h, n_rows, stride=N_HEADS)


def _inproj_stages(x_ref, g_ref, w_ref, lng_ref, lnb_ref, outs, with_vn_out):
    q_ref, k_ref, v_ref, sga_ref, ugc_ref, vn_ref = outs[:6]
    tm = x_ref.shape[0]

    normed = []

    def proj(i):
        if not normed:
            x = x_ref[...]
            ms = jnp.mean(x * x, axis=-1, keepdims=True)
            normed.append((x * lax.rsqrt(ms + EPS) * g_ref[...]).astype(BF16))
        return jnp.dot(normed[0], w_ref[:, i * D_GRP:(i + 1) * D_GRP], preferred_element_type=F32)

    def store_heads(ref, c, val):
        if len(ref.shape) == 3:
            for d in range(ref.shape[0]):
                ref[d, _head_rows(c, tm), :] = val
        else:
            ref[_head_rows(c, tm), :] = val

    def store_all_heads(ref, val):
        for c in range(N_HEADS):
            store_heads(ref, c, val[:, c * HEAD_W:(c + 1) * HEAD_W])

    def q_stage():
        q_ref[...] = (proj(0) * float(np.float32(DK ** -0.5) * LOG2E)).astype(q_ref.dtype)

    def k_stage():
        store_all_heads(k_ref, proj(1))

    def v_stage():
        store_all_heads(v_ref, proj(2))

    def gate_stage():
        sga_ref[...] = _silu(proj(3)).astype(sga_ref.dtype)

    def ugc_stage():
        ugc_ref[...] = (proj(4) * _silu(proj(6))).astype(ugc_ref.dtype)

    def vn_stage():
        vc = proj(5)
        for c in range(N_HEADS):
            sl = slice(c * HEAD_W, (c + 1) * HEAD_W)
            blk = vc[:, sl]
            mu = jnp.mean(blk, axis=-1, keepdims=True)
            xc = blk - mu
            var = jnp.mean(xc * xc, axis=-1, keepdims=True)
            vn = xc * lax.rsqrt(var + EPS) * lng_ref[:, sl] + lnb_ref[:, sl]
            vn_ref[:, sl] = vn.astype(vn_ref.dtype)
            if with_vn_out:
                store_heads(outs[6], c, vn)

    return [q_stage, k_stage, v_stage, gate_stage, ugc_stage, vn_stage]


def _inproj_kernel(*refs, n_alias, with_vn_out):
    outs = refs[5 + n_alias:]
    for stage in _inproj_stages(*refs[:5], outs, with_vn_out):
        stage()


def _inproj_out_specs(m, tm, act_dtype, layer, depth, first, with_vn_out, index):
    row = pl.BlockSpec((tm, D_GRP), lambda *i: (index(*i), 0))
    act = jax.ShapeDtypeStruct((m, D_GRP), act_dtype)
    stacked = jax.ShapeDtypeStruct((depth, m * N_HEADS, HEAD_W), F32)
    if first:
        stacked_spec = pl.BlockSpec((depth, tm * N_HEADS, HEAD_W), lambda *i: (0, index(*i), 0))
    else:
        stacked_spec = pl.BlockSpec((None, tm * N_HEADS, HEAD_W), lambda *i: (layer, index(*i), 0))
    extra = 1 if with_vn_out else 0
    out_shape = [act, stacked, stacked, act, act, act] + [stacked] * extra
    out_specs = [row, stacked_spec, stacked_spec, row, row, row] + [stacked_spec] * extra
    return out_shape, out_specs, [1, 2] + [6] * extra


def _inproj(x, g, w, lng, lnb, act_dtype, layer, depth, prev, with_vn_out):
    m = x.shape[0]
    tm = ROW_TILE
    const = lambda shape: pl.BlockSpec(shape, lambda i: (0, 0))
    out_shape, out_specs, stacked_idx = _inproj_out_specs(
        m, tm, act_dtype, layer, depth, prev is None, with_vn_out, lambda i: i)
    prev = list(prev) if prev is not None else []
    aliases = {5 + n: stacked_idx[n] for n in range(len(prev))}
    return pl.pallas_call(
        functools.partial(_inproj_kernel, n_alias=len(prev), with_vn_out=with_vn_out),
        grid=(m // tm,),
        in_specs=[pl.BlockSpec((tm, D_MODEL), lambda i: (i, 0)), const((1, D_MODEL)),
                  pl.BlockSpec((None, D_MODEL, N_GRP * D_GRP), lambda i: (layer, 0, 0)),
                  const((1, D_GRP)), const((1, D_GRP))]
                 + [pl.BlockSpec(memory_space=pl.ANY)] * len(prev),
        out_specs=out_specs,
        out_shape=out_shape,
        input_output_aliases=aliases,
        compiler_params=pltpu.CompilerParams(dimension_semantics=("parallel",),
                                             vmem_limit_bytes=VMEM_LIMIT),
        name="inproj",
    )(x, g, w, lng, lnb, *prev)


def _prompt_attn_kernel(q_ref, k_ref, v_ref, sga_ref, sg_ref, lq1, lk1, lq2, lk2,
                        x_ref, ugc_ref, vn_ref, ws_ref, bsb_ref, wout_ref, fg_ref,
                        o_ref, kb_sc, vt_sc, cm_sc, att_sc, *stat_sc, lam_init, tb, nblk, final):
    m_sc = stat_sc[:N_HEADS]
    acc_sc = stat_sc[N_HEADS:]
    i = pl.program_id(1)

    @pl.when(i == 0)
    def _():
        pos = lax.broadcasted_iota(jnp.int32, (tb, POS_LANES), 0)
        lane = lax.broadcasted_iota(jnp.int32, (tb, POS_LANES), 1)
        for jb in range(nblk):
            kpos = pos + jb * tb
            hi = (kpos // PAGE) * PAGE
            aug = jnp.where(lane >= 2 * SLOPE_PARTS, 0, jnp.where(lane % 2 == 0, hi, kpos - hi))
            aug = aug.astype(F32).astype(BF16)
            for h in range(N_HEADS):
                rows = _head_rows(h, tb, jb * tb)
                kb_sc[h, jb * tb:(jb + 1) * tb, 0:HEAD_W] = k_ref[rows, :].astype(BF16)
                kb_sc[h, jb * tb:(jb + 1) * tb, HEAD_W:HEAD_W + POS_LANES] = aug
                vt_sc[h, jb, 0:HEAD_W, :] = v_ref[rows, :].T.astype(BF16)
                vt_sc[h, jb, HEAD_W:HEAD_W + ONES_ROWS, :] = jnp.ones((ONES_ROWS, tb), BF16)

    lane = lax.broadcasted_iota(jnp.int32, (tb, HEAD_W), 1)
    qz = []
    for h in range(N_HEADS):
        q = q_ref[:, h * HEAD_W:(h + 1) * HEAD_W]
        zero = jnp.zeros_like(q)
        slope = jnp.zeros((tb, POS_LANES), F32)
        for n, part in enumerate(_slope_parts(h)):
            slope = jnp.where(lane // 2 == n, part, slope)
        slope = slope.astype(BF16)
        q2 = jnp.concatenate([jnp.where(lane < DK, q, zero), jnp.where(lane >= DK, q, zero)], axis=0)
        qz.append(jnp.concatenate([q2, jnp.concatenate([slope, slope], axis=0)], axis=1))
        m_sc[h][...] = jnp.full(m_sc[h].shape, NEG, F32)
        acc_sc[h][...] = jnp.zeros(acc_sc[h].shape, F32)

    def steps(first_blk, nkb, masked):
        heads = range(N_HEADS)
        rows = pl.ds(pl.multiple_of(first_blk * tb, tb), nkb * tb)
        s = [lax.dot_general(kb_sc[h, rows, :], qz[h], (((1,), (1,)), ((), ())),
                             preferred_element_type=F32) for h in heads]
        if masked:
            key_i = lax.broadcasted_iota(jnp.int32, (nkb * tb, 2 * tb), 0)
            qry_i = lax.broadcasted_iota(jnp.int32, (nkb * tb, 2 * tb), 1)
            qry_i = jnp.where(qry_i >= tb, qry_i - tb, qry_i) + (nkb - 1) * tb
            s = [jnp.where(key_i <= qry_i, sh, NEG) for sh in s]
        m_old = [m_sc[h][...] for h in heads]
        m_new = [jnp.maximum(m_old[h], jnp.max(s[h], axis=0, keepdims=True)) for h in heads]
        p = [jnp.exp2(s[h] - m_new[h]).astype(BF16) for h in heads]
        for h in heads:
            pv = jnp.dot(vt_sc[h, first_blk], p[h][0:tb, :], preferred_element_type=F32)
            for kb in range(1, nkb):
                pv = pv + jnp.dot(vt_sc[h, first_blk + kb], p[h][kb * tb:(kb + 1) * tb, :],
                                  preferred_element_type=F32)
            acc_sc[h][...] = acc_sc[h][...] * jnp.exp2(m_old[h] - m_new[h]) + pv
            m_sc[h][...] = m_new[h]

    def body(jj, carry):
        steps(2 * jj, 2, False)
        return carry

    lax.fori_loop(0, i // 2, body, 0)

    @pl.when(i % 2 == 1)
    def _():
        steps(i - 1, 2, True)

    @pl.when(i % 2 == 0)
    def _():
        steps(i, 1, True)

    y = _mix_chunk_part(x_ref, ugc_ref, vn_ref, ws_ref, bsb_ref, wout_ref, cm_sc)

    lam = _lam_value(lq1, lk1, lq2, lk2, lam_init)
    for h in range(N_HEADS):
        sl = slice(h * HEAD_W, (h + 1) * HEAD_W)
        acc = acc_sc[h]
        inv = 1.0 / acc[HEAD_W:HEAD_W + 1, :]
        o1 = acc[0:HEAD_W, 0:tb] * inv[:, 0:tb]
        o2 = acc[0:HEAD_W, tb:2 * tb] * inv[:, tb:2 * tb]
        o = (o1 - lam * o2).T
        ms = jnp.mean(o * o, axis=-1, keepdims=True)
        att = o * lax.rsqrt(ms + EPS) * sg_ref[...] * (1.0 - lam_init)
        att_sc[:, sl] = (att * sga_ref[:, sl].astype(F32)).astype(BF16)
    _mix_attn_part(y, att_sc[...], wout_ref, fg_ref, o_ref, final)


def _prompt_attn_mix(q, kbuf, vbuf, layer, sga, sg, lams, lam_init, batch, seq,
                     x, ugc, vn, ws, bsb, wout, fg, final):
    tb = ATT_BLK
    nblk = seq // tb
    rowspec = lambda width: pl.BlockSpec((tb, width), lambda b, i: (b * nblk + i, 0))
    kvspec = pl.BlockSpec((None, seq * N_HEADS, HEAD_W), lambda b, i: (layer, b, 0))
    const = lambda shape, **kw: pl.BlockSpec(shape, lambda b, i: (0,) * len(shape), **kw)
    stats = ([pltpu.VMEM((1, 2 * tb), F32)] * N_HEADS
             + [pltpu.VMEM((HEAD_W + ONES_ROWS, 2 * tb), F32)] * N_HEADS)
    return pl.pallas_call(
        functools.partial(_prompt_attn_kernel, lam_init=lam_init, tb=tb, nblk=nblk, final=final),
        grid=(batch, nblk),
        in_specs=[rowspec(D_GRP), kvspec, kvspec, rowspec(D_GRP), const((1, HEAD_W))] + [const((1, DK))] * 4
                 + [rowspec(D_MODEL), rowspec(D_GRP), rowspec(D_GRP), const((N_HEADS, CHUNK, CHUNK)),
                    const((CHUNK, D_GRP)),
                    pl.BlockSpec((None, 2 * D_GRP, D_MODEL), lambda b, i: (layer, 0, 0),
                                 pipeline_mode=pl.Buffered(1)),
                    const((1, D_MODEL))],
        out_specs=rowspec(D_MODEL),
        out_shape=jax.ShapeDtypeStruct(x.shape, F32),
        scratch_shapes=[pltpu.VMEM((N_HEADS, seq, HEAD_W + POS_LANES), BF16),
                        pltpu.VMEM((N_HEADS, nblk, HEAD_W + ONES_ROWS, tb), BF16),
                        pltpu.VMEM((tb, D_GRP), BF16), pltpu.VMEM((tb, D_GRP), BF16)] + stats,
        compiler_params=pltpu.CompilerParams(
            dimension_semantics=("parallel", "arbitrary"),
            vmem_limit_bytes=VMEM_LIMIT),
        name="prompt_attn_mix",
    )(q, kbuf, vbuf, sga, sg, *lams, x, ugc, vn, ws, bsb, wout, fg)


def _sample_bias(n_pages, dec_seq):
    past = n_pages * PAGE
    n_pos = past + NEW_ROWS // N_HEADS
    t = np.arange(dec_seq)[:, None]
    kpos = np.arange(n_pos)[None, :]
    visible = (kpos < past) | (kpos - past <= t)
    bias = np.full((2, N_HEADS, dec_seq, n_pos, N_HEADS), NEG, np.float32)
    for h in range(N_HEADS):
        alibi = -np.float32(_alibi_slope(h)) * (past + t - kpos).astype(np.float32) * LOG2E
        bias[:, h, :, :, h] = np.where(visible, alibi, NEG)
    return bias.reshape(2 * N_HEADS * dec_seq, n_pos * N_HEADS)


def _paged_attention(pt_ref, step, n_steps, first_seq, q_ref, kn_ref, vn_ref, sga_ref, bias_ref, sg_ref,
                     lq1, lk1, lq2, lk2, ck_hbm, cv_hbm, o_ref, kbuf, vbuf, ksem, vsem,
                     *, layer, lam_init, n_pages, dec_seq, fillers):
    n_seq = len(fillers)
    n_rows = 2 * N_HEADS * dec_seq
    half = N_HEADS * dec_seq
    page_rows = PAGE * N_HEADS
    past_rows = n_pages * page_rows
    new_rows = dec_seq * N_HEADS
    slots = range(n_seq)

    def buf_of(slot, at_step):
        return PAGE_BUFS * slot + at_step % PAGE_BUFS

    def page_copies(seq_idx, buf):
        copies = []
        for j in range(n_pages):
            page = pt_ref[seq_idx * n_pages + j]
            copies.append(pltpu.make_async_copy(ck_hbm.at[layer, page], kbuf.at[buf, j], ksem.at[buf]))
            copies.append(pltpu.make_async_copy(cv_hbm.at[layer, page], vbuf.at[buf, j], vsem.at[buf]))
        return copies

    @pl.when(step == 0)
    def _():
        for slot in slots:
            for cp in page_copies(first_seq + slot, buf_of(slot, 0)):
                cp.start()

    @pl.when(step + 1 < n_steps)
    def _():
        for slot in slots:
            for cp in page_copies(first_seq + n_seq * (step + 1) + slot, buf_of(slot, step + 1)):
                cp.start()

    lane = lax.broadcasted_iota(jnp.int32, (half, HEAD_W), 1)
    lam = _lam_value(lq1, lk1, lq2, lk2, lam_init)
    pad = jnp.zeros((NEW_ROWS - new_rows, HEAD_W), F32)
    nt = (((1,), (1,)), ((), ()))

    def attend(slot, seq_idx):
        buf = buf_of(slot, step)
        rows = slice(slot * dec_seq, (slot + 1) * dec_seq)
        new = slice(slot * new_rows, (slot + 1) * new_rows)
        k_new = jnp.concatenate([kn_ref[new, :], pad], axis=0)
        v_new = jnp.concatenate([vn_ref[new, :], pad], axis=0)
        qh = jnp.concatenate([q_ref[rows, h * HEAD_W:(h + 1) * HEAD_W] for h in range(N_HEADS)], axis=0)
        qrows = jnp.concatenate([jnp.where(lane < DK, qh, 0.0), jnp.where(lane >= DK, qh, 0.0)], axis=0)
        s = [lax.dot_general(qrows, kbuf[buf, j], nt, preferred_element_type=F32) for j in range(n_pages)]
        s.append(lax.dot_general(qrows, k_new, nt, preferred_element_type=F32))
        s = jnp.concatenate(s, axis=1) + bias_ref[...]
        yield
        p = jnp.exp2(s - jnp.max(s, axis=-1, keepdims=True))
        inv = 1.0 / jnp.sum(p, axis=-1, keepdims=True)
        yield
        pv = jnp.dot(p[:, past_rows:], v_new, preferred_element_type=F32)
        for j in range(n_pages):
            pv = pv + jnp.dot(p[:, j * page_rows:(j + 1) * page_rows], vbuf[buf, j],
                              preferred_element_type=F32)
        res = pv[0:half, :] * inv[0:half, :] - pv[half:n_rows, :] * (lam * inv[half:n_rows, :])
        yield
        for h in range(N_HEADS):
            sl = slice(h * HEAD_W, (h + 1) * HEAD_W)
            o = res[h * dec_seq:(h + 1) * dec_seq, :]
            ms = jnp.mean(o * o, axis=-1, keepdims=True)
            att = o * lax.rsqrt(ms + EPS) * sg_ref[...] * (1.0 - lam_init)
            o_ref[rows, sl] = (att * sga_ref[rows, sl]).astype(o_ref.dtype)

    for slot in slots:
        seq_idx = first_seq + n_seq * step + slot
        for cp in page_copies(seq_idx, buf_of(slot, step)):
            cp.wait()
        pending = list(fillers[slot])
        for _ in attend(slot, seq_idx):
            if pending:
                pending.pop(0)()
        for filler in pending:
            filler()


N_PAGED_IN = 12


def _paged_operands(qs, knbuf, vnbuf, sgas, cache_k, cache_v, sg, lams, layer, dec_seq, n_pages,
                    first_seq, n_seq, n_steps, step_of):
    bias = jnp.asarray(_sample_bias(n_pages, dec_seq))
    first_blk = first_seq // n_seq
    const = lambda shape, **kw: pl.BlockSpec(shape, lambda *i: (0, 0), **kw)
    seqspec = pl.BlockSpec((n_seq * dec_seq, D_GRP), lambda *i: (first_blk + step_of(*i), 0))
    newspec = pl.BlockSpec((None, n_seq * dec_seq * N_HEADS, HEAD_W),
                           lambda *i: (layer, first_blk + step_of(*i), 0))
    hbm = pl.BlockSpec(memory_space=pl.ANY)
    operands = [qs, knbuf, vnbuf, sgas, bias, sg, *lams, cache_k, cache_v]
    in_specs = ([seqspec, newspec, newspec, seqspec, const(bias.shape, pipeline_mode=pl.Buffered(1)),
                 const((1, HEAD_W))] + [const((1, DK))] * 4 + [hbm, hbm])
    assert len(operands) == len(in_specs) == N_PAGED_IN
    out_spec = pl.BlockSpec((n_seq * dec_seq, D_GRP), lambda *i: (step_of(*i), 0))
    out_shape = jax.ShapeDtypeStruct((n_steps * n_seq * dec_seq, D_GRP), F32)
    n_bufs = PAGE_BUFS * n_seq
    page_buf = pltpu.VMEM((n_bufs, n_pages, PAGE * N_HEADS, HEAD_W), F32)
    scratch = [page_buf, page_buf, pltpu.SemaphoreType.DMA((n_bufs,)), pltpu.SemaphoreType.DMA((n_bufs,))]
    return operands, in_specs, out_spec, out_shape, scratch


def _inproj_decode_kernel(pt_ref, x_ref, g_ref, w_ref, lng_ref, lnb_ref, *rest,
                          n_alias, n_seq, first_seq, layer, lam_init, n_pages, dec_seq):
    paged_in, rest = rest[:N_PAGED_IN], rest[N_PAGED_IN + n_alias:]
    outs, att_ref, scratch = rest[:6], rest[6], rest[7:]
    stages = _inproj_stages(x_ref, g_ref, w_ref, lng_ref, lnb_ref, outs, False)
    per_slot = -(-len(stages) // n_seq)
    stages = stages[:per_slot] + stages[per_slot:][::-1]
    fillers = [stages[slot * per_slot:(slot + 1) * per_slot] for slot in range(n_seq)]
    _paged_attention(pt_ref, pl.program_id(0), pl.num_programs(0), first_seq, *paged_in, att_ref, *scratch,
                     layer=layer, lam_init=lam_init, n_pages=n_pages, dec_seq=dec_seq, fillers=fillers)


def _inproj_decode(x, g, w, lng, lnb, layer, depth, prev, pt, paged, first_seq, n_seq, dec_seq, n_pages,
                   lam_init):
    m = x.shape[0]
    tm = ROW_TILE // 2
    n_steps = m // tm
    const = lambda shape, **kw: pl.BlockSpec(shape, lambda t, pt: (0, 0), **kw)
    once = dict(pipeline_mode=pl.Buffered(1))
    p_operands, p_specs, p_out_spec, p_out_shape, p_scratch = _paged_operands(
        *paged, layer, dec_seq, n_pages, first_seq, n_seq, n_steps, lambda t, pt: t)
    out_shape, out_specs, stacked_idx = _inproj_out_specs(
        m, tm, BF16, layer, depth, prev is None, False, lambda t, pt: t)
    prev = list(prev) if prev is not None else []
    n_inputs = 6 + N_PAGED_IN
    aliases = {n_inputs + i: stacked_idx[i] for i in range(len(prev))}
    grid_spec = pltpu.PrefetchScalarGridSpec(
        num_scalar_prefetch=1,
        grid=(n_steps,),
        in_specs=[pl.BlockSpec((tm, D_MODEL), lambda t, pt: (t, 0)), const((1, D_MODEL)),
                  pl.BlockSpec((None, D_MODEL, N_GRP * D_GRP), lambda t, pt: (layer, 0, 0), **once),
                  const((1, D_GRP)), const((1, D_GRP))]
                 + p_specs + [pl.BlockSpec(memory_space=pl.ANY)] * len(prev),
        out_specs=out_specs + [p_out_spec],
        scratch_shapes=p_scratch,
    )
    return pl.pallas_call(
        functools.partial(_inproj_decode_kernel, n_alias=len(prev), n_seq=n_seq, first_seq=first_seq,
                          layer=layer, lam_init=lam_init, n_pages=n_pages, dec_seq=dec_seq),
        grid_spec=grid_spec,
        out_shape=out_shape + [p_out_shape],
        input_output_aliases=aliases,
        compiler_params=pltpu.CompilerParams(dimension_semantics=("arbitrary",),
                                             vmem_limit_bytes=VMEM_LIMIT),
        name="inproj_decode",
    )(pt, x, g, w, lng, lnb, *p_operands, *prev)


def _mix_chunk_part(x_ref, ugc_ref, vn_ref, ws_ref, bsb_ref, wout_ref, cm_sc):
    tm = x_ref.shape[0]
    nc = tm // CHUNK
    r = lax.broadcasted_iota(jnp.int32, (CHUNK, CHUNK), 0)
    c = lax.broadcasted_iota(jnp.int32, (CHUNK, CHUNK), 1)
    for h in range(N_HEADS):
        sl = slice(h * HEAD_W, (h + 1) * HEAD_W)
        ws = jnp.where(c <= r, ws_ref[h], 0.0).astype(BF16)
        vn = jnp.concatenate([vn_ref[ci * CHUNK:(ci + 1) * CHUNK, sl].astype(BF16)
                              for ci in range(nc)], axis=1)
        mixed = jnp.dot(ws, vn, preferred_element_type=F32)
        for ci in range(nc):
            rows = slice(ci * CHUNK, (ci + 1) * CHUNK)
            mc = mixed[:, ci * HEAD_W:(ci + 1) * HEAD_W] + bsb_ref[:, sl]
            cm_sc[rows, sl] = (ugc_ref[rows, sl].astype(F32) * mc).astype(BF16)
    return x_ref[...] + jnp.dot(cm_sc[...], wout_ref[D_GRP:2 * D_GRP, :], preferred_element_type=F32)


def _mix_attn_part(y, att, wout_ref, fg_ref, o_ref, final):
    y = y + jnp.dot(att, wout_ref[0:D_GRP, :], preferred_element_type=F32)
    if final:
        ms = jnp.mean(y * y, axis=-1, keepdims=True)
        y = y * lax.rsqrt(ms + EPS) * fg_ref[...]
    o_ref[...] = y


def _mix_out_kernel(x_ref, att_ref, ugc_ref, vn_ref, ws_ref, bsb_ref, wout_ref, fg_ref,
                    o_ref, cm_sc, *, final):
    y = _mix_chunk_part(x_ref, ugc_ref, vn_ref, ws_ref, bsb_ref, wout_ref, cm_sc)
    _mix_attn_part(y, att_ref[...].astype(BF16), wout_ref, fg_ref, o_ref, final)


def _mix_out(x, att, ugc, vn, ws, bsb, wout, layer, fg, final):
    m = x.shape[0]
    tm = ROW_TILE
    row = lambda width: pl.BlockSpec((tm, width), lambda i: (i, 0))
    return pl.pallas_call(
        functools.partial(_mix_out_kernel, final=final),
        grid=(m // tm,),
        in_specs=[row(D_MODEL), row(D_GRP), row(D_GRP), row(D_GRP),
                  pl.BlockSpec((N_HEADS, CHUNK, CHUNK), lambda i: (0, 0, 0)),
                  pl.BlockSpec((CHUNK, D_GRP), lambda i: (0, 0)),
                  pl.BlockSpec((None, 2 * D_GRP, D_MODEL), lambda i: (layer, 0, 0)),
                  pl.BlockSpec((1, D_MODEL), lambda i: (0, 0))],
        out_specs=row(D_MODEL),
        out_shape=jax.ShapeDtypeStruct((m, D_MODEL), F32),
        scratch_shapes=[pltpu.VMEM((tm, D_GRP), BF16)],
        compiler_params=pltpu.CompilerParams(dimension_semantics=("parallel",),
                                             vmem_limit_bytes=VMEM_LIMIT),
        name="mix_out",
    )(x, att, ugc, vn, ws, bsb, wout, fg)


def kernel(x_prompt, x_sample, cache_k, cache_v, page_table, norm_g, w_in, lam_q1, lam_k1,
           lam_q2, lam_k2, subln_g, ln_v_g, ln_v_b, w_s, b_s, w_out, final_g):
    batch, seq, _ = x_prompt.shape
    dec_batch, dec_seq, _ = x_sample.shape
    depth, n_pool = cache_k.shape[0], cache_k.shape[1]
    n_pages = page_table.shape[1]
    assert seq % ATT_BLK == 0 and (batch * seq) % ROW_TILE == 0 and (dec_batch * dec_seq) % ROW_TILE == 0
    assert CHUNK % dec_seq == 0 and dec_seq % 8 == 0 and ATT_BLK % PAGE == 0
    assert dec_seq * N_HEADS <= NEW_ROWS
    proj_steps = (batch * seq) // (ROW_TILE // 2)
    assert dec_batch % proj_steps == 0
    n_seq_proj = dec_batch // proj_steps

    hp = x_prompt.reshape(batch * seq, D_MODEL)
    hs = x_sample.reshape(dec_batch * dec_seq, D_MODEL)
    ck = cache_k.reshape(depth, n_pool, PAGE * N_HEADS, HEAD_W)
    cv = cache_v.reshape(depth, n_pool, PAGE * N_HEADS, HEAD_W)
    pt = page_table.reshape(-1)
    w_in_b = w_in.astype(BF16)
    w_out_b = w_out.astype(BF16)
    fg = final_g.reshape(1, D_MODEL)
    eye = jnp.eye(CHUNK // dec_seq, dtype=F32)

    prompt_bufs, sample_bufs = None, None
    for l in range(depth):
        lam_init = _lambda_init(l)
        g = norm_g[l].reshape(1, D_MODEL)
        lng = ln_v_g[l].reshape(1, D_GRP)
        lnb = ln_v_b[l].reshape(1, D_GRP)
        sg = subln_g[l].reshape(1, HEAD_W)
        lams = [a[l].reshape(1, DK) for a in (lam_q1, lam_k1, lam_q2, lam_k2)]
        final = l == depth - 1

        qs, ksbuf, vsbuf, sgas, ugcs, vns, vnsbuf = _inproj(hs, g, w_in_b, lng, lnb, F32, l, depth,
                                                            sample_bufs, True)
        sample_bufs = (ksbuf, vsbuf, vnsbuf)
        paged = (qs, ksbuf, vsbuf, sgas, ck, cv, sg, lams)
        q, kbuf, vbuf, sga, ugc, vn, atts = _inproj_decode(
            hp, g, w_in_b, lng, lnb, l, depth, prompt_bufs,
            pt, paged, 0, n_seq_proj, dec_seq, n_pages, lam_init)
        prompt_bufs = (kbuf, vbuf)

        bsb = jnp.repeat(b_s[l].T, HEAD_W, axis=1)
        hp = _prompt_attn_mix(q, kbuf, vbuf, l, sga, sg, lams, lam_init, batch, seq,
                              hp, ugc, vn, w_s[l], bsb, w_out_b, fg, final)

        ws8 = w_s[l][:, :dec_seq, :dec_seq]
        ws_bd = (eye[None, :, None, :, None] * ws8[:, None, :, None, :]).reshape(N_HEADS, CHUNK, CHUNK)
        bsb = jnp.tile(jnp.repeat(b_s[l][:, :dec_seq].T, HEAD_W, axis=1), (CHUNK // dec_seq, 1))
        hs = _mix_out(hs, atts, ugcs, vns, ws_bd, bsb, w_out_b, l, fg, final)

    shape_p = (depth, batch, seq, N_HEADS, HEAD_W)
    shape_s = (depth, dec_batch, dec_seq, N_HEADS, HEAD_W)
    return (hp.reshape(batch, seq, D_MODEL),
            hs.reshape(dec_batch, dec_seq, D_MODEL),
            prompt_bufs[0].reshape(shape_p),
            prompt_bufs[1].reshape(shape_p),
            sample_bufs[0].reshape(shape_s),
            sample_bufs[1].reshape(shape_s),
            sample_bufs[2].reshape(shape_s))
```
